```python
import jax, jax.numpy as jnp
from jax import lax
import numpy as np

D_MODEL = 1024
BATCH = 2
SEQ = 8192
DEPTH = 2
DEC_BATCH = 16
DEC_SEQ = 16
PAST_LEN = 1024

CHUNK = 64
QBLOCK = 128
H_SB = 8
D_SB = 64
W_SB = H_SB * D_SB
H_GLA = 4
DK_GLA = 64
DV_GLA = 128
WK_GLA = H_GLA * DK_GLA
WV_GLA = H_GLA * DV_GLA
GATE_RANK = 16
GATE_TAU = 16.0
D_FF = 4 * D_MODEL
N_MOD = 6
EPS = 1e-6
IN_SIZES = (W_SB, W_SB, W_SB, WK_GLA, WK_GLA, WV_GLA, GATE_RANK, WV_GLA)
N_IN = W_SB * 3 + WK_GLA * 2 + WV_GLA * 2 + GATE_RANK

kernel_name = "hymba_stickbreaking_gla_adaln_stream"


def _rmsnorm(x, g):
    x32 = x.astype(jnp.float32)
    y = x32 * lax.rsqrt(jnp.mean(x32 * x32, axis=-1, keepdims=True) + EPS)
    return (y * g.astype(jnp.float32)).astype(x.dtype)


def _split_points():
    pts, acc = [], 0
    for s in IN_SIZES[:-1]:
        acc += s
        pts.append(acc)
    return pts


def _sb_block(qb, qpos, k, v, kpos):
    z = jnp.einsum('bqhd,bkhd->bhqk', qb, k).astype(jnp.float32) * (D_SB ** -0.5)
    mask = kpos[None, :] < qpos[:, None]
    l_neg = jnp.where(mask, jax.nn.log_sigmoid(-z), 0.0)
    between = lax.cumsum(l_neg, axis=3, reverse=True) - l_neg
    w = jnp.where(mask, jnp.exp(jax.nn.log_sigmoid(z) + between), 0.0)
    return jnp.einsum('bhqk,bkhd->bqhd', w.astype(v.dtype), v)


def _stick_breaking(q, k, v, q_offset):
    B, T, H, d = q.shape
    kpos = jnp.arange(k.shape[1])
    qb = min(QBLOCK, T)
    nb = T // qb
    qs = q.reshape(B, nb, qb, H, d).swapaxes(0, 1)
    qpos = (q_offset + jnp.arange(T)).reshape(nb, qb)
    out = lax.map(lambda a: _sb_block(a[0], a[1], k, v, kpos), (qs, qpos))
    return out.swapaxes(0, 1).reshape(B, T, H, d)


def _gla_chunk(S, inp):
    q, k, v, g = inp
    L = q.shape[1]
    b = jnp.cumsum(g, axis=1)
    causal = jnp.tril(jnp.ones((L, L), dtype=bool))[None, :, :, None, None]
    diff = b[:, :, None] - b[:, None, :]
    decay = jnp.exp(jnp.where(causal, diff, -jnp.inf))
    att = jnp.einsum('bthd,bshd,btshd->bhts', q, k, decay)
    o = jnp.einsum('bhts,bshv->bthv', att, v) + jnp.einsum('bthd,bhdv->bthv', q * jnp.exp(b), S)
    b_last = b[:, -1]
    S_new = jnp.exp(b_last)[..., None] * S + jnp.einsum(
        'bshd,bshv->bhdv', k * jnp.exp(b_last[:, None] - b), v)
    return S_new, o


def _gla(q, k, v, g, S0):
    B, T = q.shape[:2]
    L = min(CHUNK, T)
    nc = T // L

    def to_chunks(a):
        return a.reshape(B, nc, L, *a.shape[2:]).swapaxes(0, 1)

    S, o = lax.scan(_gla_chunk, S0, (to_chunks(q), to_chunks(k), to_chunks(v), to_chunks(g)))
    return o.swapaxes(0, 1).reshape(B, T, H_GLA, DV_GLA), S


def _mixer(h, p, l, k_past, v_past, S0):
    B, T, _ = h.shape
    proj = h @ p['w_in'][l]
    qa, ka, va, qg, kg, vg, gr, og = jnp.split(proj, _split_points(), axis=-1)
    qa = _rmsnorm(qa.reshape(B, T, H_SB, D_SB), p['q_norm'][l])
    ka = _rmsnorm(ka.reshape(B, T, H_SB, D_SB), p['k_norm'][l])
    va = va.reshape(B, T, H_SB, D_SB)
    if k_past is None:
        k_all, v_all, q_offset = ka, va, 0
    else:
        k_all = jnp.concatenate([k_past.astype(ka.dtype), ka], axis=1)
        v_all = jnp.concatenate([v_past.astype(va.dtype), va], axis=1)
        q_offset = k_past.shape[1]
    a_out = _stick_breaking(qa, k_all, v_all, q_offset).reshape(B, T, W_SB)
    f32 = jnp.float32
    qg = qg.reshape(B, T, H_GLA, DK_GLA).astype(f32) * (DK_GLA ** -0.5)
    kg = kg.reshape(B, T, H_GLA, DK_GLA).astype(f32)
    vg = vg.reshape(B, T, H_GLA, DV_GLA).astype(f32)
    logf = jax.nn.log_sigmoid((gr @ p['w_gate'][l] + p['b_gate'][l]).astype(f32)) / GATE_TAU
    o, S = _gla(qg, kg, vg, logf.reshape(B, T, H_GLA, DK_GLA), S0)
    o = _rmsnorm(o, p['gla_norm'][l]).astype(h.dtype) * jax.nn.silu(og.reshape(B, T, H_GLA, DV_GLA))
    merged = jnp.concatenate([a_out, o.reshape(B, T, WV_GLA)], axis=-1)
    return merged @ p['w_out'][l], ka, va, S


def _layer(x, c, p, l, k_past, v_past, S0):
    mod = jax.nn.silu(c) @ p['w_ada'][l] + p['b_ada'][l]
    sh1, sc1, g1, sh2, sc2, g2 = jnp.split(mod[:, None, :], N_MOD, axis=-1)
    h = _rmsnorm(x, p['norm_mix'][l]) * (1.0 + sc1) + sh1
    m, k_new, v_new, S = _mixer(h, p, l, k_past, v_past, S0)
    x = x + g1 * m
    h = _rmsnorm(x, p['norm_mlp'][l]) * (1.0 + sc2) + sh2
    x = x + g2 * (jnp.square(jax.nn.relu(h @ p['w_up'][l])) @ p['w_down'][l])
    return x, k_new, v_new, S


def setup_inputs(seed: int = 0) -> dict:
    key = jax.random.key(seed)
    ks = jax.random.split(key, 20)
    n = jax.random.normal
    f = jnp.float32
    return {
        'x_prompt': n(ks[0], (BATCH, SEQ, D_MODEL), f),
        'x_sample': n(ks[1], (DEC_BATCH, DEC_SEQ, D_MODEL), f),
        'c_prompt': n(ks[2], (BATCH, D_MODEL), f),
        'c_sample': n(ks[3], (DEC_BATCH, D_MODEL), f),
        'cache_k': n(ks[4], (DEPTH, DEC_BATCH, PAST_LEN, H_SB, D_SB), f),
        'cache_v': n(ks[5], (DEPTH, DEC_BATCH, PAST_LEN, H_SB, D_SB), f),
        'state_gla': 0.3 * n(ks[6], (DEPTH, DEC_BATCH, H_GLA, DK_GLA, DV_GLA), f),
        'w_ada': 0.5 * D_MODEL ** -0.5 * n(ks[7], (DEPTH, D_MODEL, N_MOD * D_MODEL), f),
        'b_ada': 0.01 * n(ks[8], (DEPTH, N_MOD * D_MODEL), f),
        'norm_mix': 1.0 + 0.05 * n(ks[9], (DEPTH, D_MODEL), f),
        'norm_mlp': 1.0 + 0.05 * n(ks[10], (DEPTH, D_MODEL), f),
        'w_in': D_MODEL ** -0.5 * n(ks[11], (DEPTH, D_MODEL, N_IN), f),
        'q_norm': 1.0 + 0.05 * n(ks[12], (DEPTH, D_SB), f),
        'k_norm': 1.0 + 0.05 * n(ks[13], (DEPTH, D_SB), f),
        'w_gate': GATE_RANK ** -0.5 * n(ks[14], (DEPTH, GATE_RANK, WK_GLA), f),
        'b_gate': 0.01 * n(ks[15], (DEPTH, WK_GLA), f),
        'gla_norm': 1.0 + 0.05 * n(ks[16], (DEPTH, DV_GLA), f),
        'w_out': D_MODEL ** -0.5 * n(ks[17], (DEPTH, W_SB + WV_GLA, D_MODEL), f),
        'w_up': D_MODEL ** -0.5 * n(ks[18], (DEPTH, D_MODEL, D_FF), f),
        'w_down': D_FF ** -0.5 * n(ks[19], (DEPTH, D_FF, D_MODEL), f),
    }


def reference(x_prompt, x_sample, c_prompt, c_sample, cache_k, cache_v, state_gla,
              w_ada, b_ada, norm_mix, norm_mlp, w_in, q_norm, k_norm, w_gate, b_gate,
              gla_norm, w_out, w_up, w_down):
    p = {'w_ada': w_ada, 'b_ada': b_ada, 'norm_mix': norm_mix, 'norm_mlp': norm_mlp,
         'w_in': w_in, 'q_norm': q_norm, 'k_norm': k_norm, 'w_gate': w_gate, 'b_gate': b_gate,
         'gla_norm': gla_norm, 'w_out': w_out, 'w_up': w_up, 'w_down': w_down}
    xp, xs = x_prompt, x_sample
    kp_l, vp_l, sp_l, ks_l, vs_l, ss_l = [], [], [], [], [], []
    for l in range(DEPTH):
        S0 = jnp.zeros((xp.shape[0], H_GLA, DK_GLA, DV_GLA), jnp.float32)
        xp, kp, vp, sp = _layer(xp, c_prompt, p, l, None, None, S0)
        xs, kn, vn, sn = _layer(xs, c_sample, p, l, cache_k[l], cache_v[l],
                                state_gla[l].astype(jnp.float32))
        kp_l.append(kp); vp_l.append(vp); sp_l.append(sp.astype(xp.dtype))
        ks_l.append(kn); vs_l.append(vn); ss_l.append(sn.astype(xs.dtype))
    k_prompt = jnp.stack(kp_l)
    v_prompt = jnp.stack(vp_l)
    gla_state_prompt = jnp.stack(sp_l)
    k_sample_new = jnp.stack(ks_l)
    v_sample_new = jnp.stack(vs_l)
    gla_state_sample = jnp.stack(ss_l)
    return (xp, xs, k_prompt, v_prompt, gla_state_prompt, k_sample_new, v_sample_new, gla_state_sample)
```

```python
import functools

import numpy as np
import jax
import jax.numpy as jnp
from jax import lax
from jax.experimental import pallas as pl
from jax.experimental.pallas import tpu as pltpu

D_MODEL = 1024
DEPTH = 2
H_SB = 8
D_SB = 64
W_SB = H_SB * D_SB
H_GLA = 4
DK_GLA = 64
DV_GLA = 128
WK_GLA = H_GLA * DK_GLA
WV_GLA = H_GLA * DV_GLA
GATE_RANK = 16
GATE_TAU = 16.0
D_FF = 4 * D_MODEL
N_MOD = 6
EPS = 1e-6

LANES = 128
KB = 128
GATE_PAD = LANES
N_IN_PAD = 3 * W_SB + 2 * WK_GLA + 2 * WV_GLA + GATE_PAD
VMEM_LIMIT = 56 * 1024 * 1024
LOG_ZERO = -104.0

f32 = jnp.float32
bf16 = jnp.bfloat16


def _params(sem):
    return pltpu.CompilerParams(dimension_semantics=sem, vmem_limit_bytes=VMEM_LIMIT)


def _log_sigmoid(x):
    return jnp.minimum(x, 0.0) - jnp.log1p(jnp.exp(-jnp.abs(x)))


def _ada_kernel(c_ref, w_ref, b_ref, o_ref):
    c = c_ref[...]
    s = (c * jax.nn.sigmoid(c)).astype(bf16)
    o_ref[...] = jnp.dot(s, w_ref[...].astype(bf16), preferred_element_type=f32) + b_ref[...]


def _ada_call(c_all, w_ada, b_ada):
    rows = c_all.shape[0]
    return pl.pallas_call(
        _ada_kernel,
        grid=(DEPTH, N_MOD),
        in_specs=[
            pl.BlockSpec((rows, D_MODEL), lambda l, p: (0, 0)),
            pl.BlockSpec((None, D_MODEL, D_MODEL), lambda l, p: (l, 0, p)),
            pl.BlockSpec((None, 1, D_MODEL), lambda l, p: (l, 0, p)),
        ],
        out_specs=pl.BlockSpec((None, rows, D_MODEL), lambda l, p: (l, 0, p)),
        out_shape=jax.ShapeDtypeStruct((DEPTH, rows, N_MOD * D_MODEL), f32),
        compiler_params=_params(("parallel", "parallel")),
        name="ada_mod",
    )(c_all, w_ada, b_ada.reshape(DEPTH, 1, N_MOD * D_MODEL))


def _mod_spec(rows_per_block, blocks_per_batch, piece):
    return pl.BlockSpec((None, rows_per_block, D_MODEL),
                        lambda i: (i // blocks_per_batch, 0, piece))


def _in_kernel(x_ref, sh_ref, sc_ref, nw_ref, w_ref, gm_ref, qn_ref, kn_ref, wg_ref, bg_ref,
               q_ref, k_ref, v_ref, qg_ref, kg_ref, vg_ref, g_ref, og_ref):
    x = x_ref[...]
    ms = jnp.mean(x * x, axis=-1, keepdims=True)
    y = x * lax.rsqrt(ms + EPS) * nw_ref[...]
    h = (y * (1.0 + sc_ref[...]) + sh_ref[...]).astype(bf16)

    def proj(lo, hi):
        return jnp.dot(h, w_ref[:, lo:hi], preferred_element_type=f32)

    def head_norm(t, gain):
        m = jnp.dot((t * t).astype(bf16), gm_ref[...], preferred_element_type=f32)
        return t * lax.rsqrt(m + EPS) * gain

    o = 0
    q = head_norm(proj(o, o + W_SB), qn_ref[...])
    q_ref[...] = (q * (D_SB ** -0.5)).astype(bf16)
    o += W_SB
    k_ref[...] = head_norm(proj(o, o + W_SB), kn_ref[...])
    o += W_SB
    v_ref[...] = proj(o, o + W_SB)
    o += W_SB
    qg_ref[...] = proj(o, o + WK_GLA) * (DK_GLA ** -0.5)
    o += WK_GLA
    kg_ref[...] = proj(o, o + WK_GLA)
    o += WK_GLA
    vg_ref[...] = proj(o, o + WV_GLA)
    o += WV_GLA
    og_ref[...] = proj(o, o + WV_GLA)
    o += WV_GLA
    gr = proj(o, o + GATE_PAD).astype(bf16)
    xg = jnp.dot(gr, wg_ref[...], preferred_element_type=f32) + bg_ref[...]
    g_ref[...] = _log_sigmoid(xg) * (1.0 / GATE_TAU)


def _in_call(x2d, mod, lw, tm, blocks_per_batch):
    n = x2d.shape[0]
    r = mod.shape[1]
    row = lambda w: pl.BlockSpec((tm, w), lambda i: (i, 0))
    const = lambda a: pl.BlockSpec(a.shape, lambda i: (0,) * a.ndim)
    outs = [(W_SB, bf16), (W_SB, f32), (W_SB, f32), (WK_GLA, f32), (WK_GLA, f32),
            (WV_GLA, f32), (WK_GLA, f32), (WV_GLA, f32)]
    return pl.pallas_call(
        _in_kernel,
        grid=(n // tm,),
        in_specs=[row(D_MODEL), _mod_spec(r, blocks_per_batch, 0), _mod_spec(r, blocks_per_batch, 1),
                  const(lw["norm_mix"]), const(lw["w_in"]), const(lw["gmat"]), const(lw["q_norm"]),
                  const(lw["k_norm"]), const(lw["w_gate"]), const(lw["b_gate"])],
        out_specs=[row(w) for w, _ in outs],
        out_shape=[jax.ShapeDtypeStruct((n, w), dt) for w, dt in outs],
        compiler_params=_params(("parallel",)),
        name="in_proj",
    )(x2d, mod, mod, lw["norm_mix"], lw["w_in"], lw["gmat"], lw["q_norm"], lw["k_norm"],
      lw["w_gate"], lw["b_gate"])


def _sb_kernel(q_ref, k_ref, v_ref, cw_ref, o_ref, acc_ref, carry_ref, *, tq, q_off):
    qi = pl.program_id(2)
    lane = lax.broadcasted_iota(jnp.int32, (tq, LANES), 1)
    row = lax.broadcasted_iota(jnp.int32, (tq, LANES), 0)
    qpos = q_off + qi * tq + row
    first_head = lane < D_SB
    q2 = q_ref[...].astype(f32)
    qh = (jnp.where(first_head, q2, 0.0).astype(bf16), jnp.where(first_head, 0.0, q2).astype(bf16))

    acc_ref[...] = jnp.zeros_like(acc_ref)
    carry_ref[...] = jnp.zeros_like(carry_ref)
    j_hi = jnp.maximum(q_off + qi * tq + tq - 2, 0) // KB

    def cond(st):
        j, alive = st
        return jnp.logical_and(j >= 0, alive > 0)

    def body(st):
        j, _ = st
        k0 = pl.multiple_of(j * KB, KB)
        kb = k_ref[pl.ds(k0, KB), :].astype(bf16)
        vb = v_ref[pl.ds(k0, KB), :].astype(bf16)
        mask = (k0 + lane) < qpos
        top = jnp.full((), -jnp.inf, f32)
        for hh in range(2):
            z = lax.dot_general(qh[hh], kb, (((1,), (1,)), ((), ())), preferred_element_type=f32)
            lneg = jnp.where(mask, _log_sigmoid(-z), 0.0)
            hi = lneg.astype(bf16)
            lo = (lneg - hi.astype(f32)).astype(bf16)
            cs = (jnp.dot(hi, cw_ref[...], preferred_element_type=f32)
                  + jnp.dot(lo, cw_ref[...], preferred_element_type=f32))
            carry = carry_ref[hh]
            between = carry + cs[:, :KB]
            w = jnp.where(mask, jnp.exp(lneg + z + between), 0.0)
            acc_ref[hh] += jnp.dot(w.astype(bf16), vb, preferred_element_type=f32)
            carry = carry + cs[:, KB:]
            carry_ref[hh] = carry
            top = jnp.maximum(top, jnp.max(carry))
        return j - 1, (top > LOG_ZERO).astype(jnp.int32)

    lax.while_loop(cond, body, (j_hi, jnp.int32(1)))
    o_ref[...] = jnp.where(first_head, acc_ref[0], acc_ref[1]).astype(bf16)


def _sb_call(q, k_all, v_all, cw, n_batch, t_q, t_k, q_off):
    tq = min(KB, t_q)
    nq = t_q // tq
    n_pair = H_SB // 2
    kv_spec = pl.BlockSpec((t_k, LANES), lambda b, p, i: (b, p))
    q_spec = pl.BlockSpec((tq, LANES), lambda b, p, i: (b * nq + i, p))
    return pl.pallas_call(
        functools.partial(_sb_kernel, tq=tq, q_off=q_off),
        grid=(n_batch, n_pair, nq),
        in_specs=[q_spec, kv_spec, kv_spec, pl.BlockSpec(cw.shape, lambda b, p, i: (0, 0))],
        out_specs=q_spec,
        out_shape=jax.ShapeDtypeStruct((n_batch * t_q, W_SB), bf16),
        scratch_shapes=[pltpu.VMEM((2, tq, LANES), f32), pltpu.VMEM((2, tq, LANES), f32)],
        compiler_params=_params(("parallel", "parallel", "arbitrary")),
        name="stick_breaking",
    )(q, k_all, v_all, cw)


def _gla_tables(chunk):
    n_lev = chunk.bit_length() - 1
    idx = np.arange(chunk)
    t, s = idx[:, None], idx[None, :]
    mats = [(s <= t), (s > t)]
    qm, km = [], []
    for lev in range(n_lev):
        h = 1 << lev
        mid = (idx // (2 * h)) * (2 * h) + h
        is_q = (idx & h) != 0
        qm.append(is_q[:, None] & (s >= mid[:, None]) & (s <= t))
        km.append((~is_q)[:, None] & (s > t) & (s <= mid[:, None] - 1))
    stack = np.concatenate(mats + qm + km, axis=0).astype(np.float32)
    x = t ^ s
    level = np.where(t > s, np.floor(np.log2(np.maximum(x, 1))).astype(np.int32), -1)
    level = np.where(t < s, -2, level).astype(np.int32)
    return jnp.asarray(stack, dtype=bf16), jnp.asarray(level), n_lev


def _gla_kernel(q_ref, k_ref, v_ref, g_ref, og_ref, s0_ref, st_ref, lev_ref, gn_ref,
                o_ref, sout_ref, s_ref, *, chunk, n_lev):
    c = pl.program_id(2)

    @pl.when(c == 0)
    def _():
        s_ref[...] = s0_ref[...]

    L = chunk
    q = q_ref[...]
    k = k_ref[...]
    g = g_ref[...]
    g_hi = g.astype(bf16)
    g_lo = (g - g_hi.astype(f32)).astype(bf16)
    g2 = jnp.concatenate([g_hi, g_lo], axis=1)
    sums = jnp.dot(st_ref[...], g2, preferred_element_type=f32)
    sums = sums[:, :LANES] + sums[:, LANES:]
    part = lambda i: sums[i * L:(i + 1) * L]
    b_incl = part(0)
    to_end = part(1)

    lane = lax.broadcasted_iota(jnp.int32, (L, LANES), 1)
    first_head = lane < DK_GLA
    heads = lambda a: (jnp.where(first_head, a, 0.0).astype(bf16), jnp.where(first_head, 0.0, a).astype(bf16))

    q_in = heads(q * jnp.exp(b_incl))
    k_end = heads(k * jnp.exp(to_end))
    q_plain = heads(q)
    kb = k.astype(bf16)
    level = lev_ref[...]
    nt = (((1,), (1,)), ((), ()))
    att = [jnp.where(level == -1, lax.dot_general(q_plain[hh], kb, nt, preferred_element_type=f32), 0.0)
           for hh in range(2)]
    for lev in range(n_lev):
        q_l = heads(q * jnp.exp(part(2 + lev)))
        k_l = (k * jnp.exp(part(2 + n_lev + lev))).astype(bf16)
        for hh in range(2):
            a = lax.dot_general(q_l[hh], k_l, nt, preferred_element_type=f32)
            att[hh] = jnp.where(level == lev, a, att[hh])

    tn = (((0,), (0,)), ((), ()))
    tot = lax.dot_general(g2, jnp.ones((L, LANES), bf16), tn, preferred_element_type=f32)
    state_decay = jnp.exp(tot[:LANES] + tot[LANES:])

    s_old = s_ref[...]
    s_b = s_old.astype(bf16)
    v = v_ref[...]
    og = og_ref[...]
    gn = gn_ref[...]
    s_new = state_decay * s_old
    outs = []
    for hh in range(2):
        vh = v[:, hh * DV_GLA:(hh + 1) * DV_GLA].astype(bf16)
        o = (jnp.dot(att[hh].astype(bf16), vh, preferred_element_type=f32)
             + jnp.dot(q_in[hh], s_b, preferred_element_type=f32))
        s_new = s_new + lax.dot_general(k_end[hh], vh, tn, preferred_element_type=f32)
        ms = jnp.mean(o * o, axis=-1, keepdims=True)
        on = o * lax.rsqrt(ms + EPS) * gn
        gate = og[:, hh * DV_GLA:(hh + 1) * DV_GLA]
        outs.append(on * (gate * jax.nn.sigmoid(gate)))
    s_ref[...] = s_new
    o_ref[...] = jnp.concatenate(outs, axis=1).astype(bf16)

    @pl.when(c == pl.num_programs(2) - 1)
    def _():
        sout_ref[...] = s_new


def _gla_call(qg, kg, vg, g, og, s0, gla_norm, n_batch, t, chunk):
    nc = t // chunk
    n_pair = H_GLA // 2
    stack, level, n_lev = _gla_tables(chunk)
    kspec = pl.BlockSpec((chunk, LANES), lambda b, p, c: (b * nc + c, p))
    vspec = pl.BlockSpec((chunk, 2 * DV_GLA), lambda b, p, c: (b * nc + c, p))
    sspec = pl.BlockSpec((None, None, 2 * DK_GLA, DV_GLA), lambda b, p, c: (b, p, 0, 0))
    const = lambda a: pl.BlockSpec(a.shape, lambda b, p, c: (0,) * a.ndim)
    s0 = s0.reshape(n_batch, n_pair, 2 * DK_GLA, DV_GLA)
    o, s_out = pl.pallas_call(
        functools.partial(_gla_kernel, chunk=chunk, n_lev=n_lev),
        grid=(n_batch, n_pair, nc),
        in_specs=[kspec, kspec, vspec, kspec, vspec, sspec, const(stack), const(level), const(gla_norm)],
        out_specs=[vspec, sspec],
        out_shape=[jax.ShapeDtypeStruct((n_batch * t, WV_GLA), bf16),
                   jax.ShapeDtypeStruct(s0.shape, f32)],
        scratch_shapes=[pltpu.VMEM((2 * DK_GLA, DV_GLA), f32)],
        compiler_params=_params(("parallel", "parallel", "arbitrary")),
        name="gla",
    )(qg, kg, vg, g, og, s0, stack, level, gla_norm)
    return o, s_out.reshape(n_batch, H_GLA, DK_GLA, DV_GLA)


def _out_kernel(a_ref, o_ref, x_ref, gate_ref, w_ref, y_ref):
    m = (jnp.dot(a_ref[...], w_ref[:W_SB, :], preferred_element_type=f32)
         + jnp.dot(o_ref[...], w_ref[W_SB:, :], preferred_element_type=f32))
    y_ref[...] = x_ref[...] + gate_ref[...] * m


def _out_call(a, o, x2d, mod, w_out, tm, blocks_per_batch):
    n = x2d.shape[0]
    r = mod.shape[1]
    row = lambda w: pl.BlockSpec((tm, w), lambda i: (i, 0))
    return pl.pallas_call(
        _out_kernel,
        grid=(n // tm,),
        in_specs=[row(W_SB), row(WV_GLA), row(D_MODEL), _mod_spec(r, blocks_per_batch, 2),
                  pl.BlockSpec(w_out.shape, lambda i: (0, 0))],
        out_specs=row(D_MODEL),
        out_shape=jax.ShapeDtypeStruct((n, D_MODEL), f32),
        compiler_params=_params(("parallel",)),
        name="out_proj",
    )(a, o, x2d, mod, w_out)


def _mlp_kernel(x_ref, sh_ref, sc_ref, gate_ref, nw_ref, wu_ref, wd_ref, y_ref, *, ff_chunk):
    x = x_ref[...]
    ms = jnp.mean(x * x, axis=-1, keepdims=True)
    y = x * lax.rsqrt(ms + EPS) * nw_ref[...]
    h = (y * (1.0 + sc_ref[...]) + sh_ref[...]).astype(bf16)
    acc = jnp.zeros(x.shape, f32)
    for c0 in range(0, D_FF, ff_chunk):
        u = jnp.dot(h, wu_ref[:, c0:c0 + ff_chunk], preferred_element_type=f32)
        u = jnp.square(jnp.maximum(u, 0.0)).astype(bf16)
        acc = acc + jnp.dot(u, wd_ref[c0:c0 + ff_chunk, :], preferred_element_type=f32)
    y_ref[...] = x + gate_ref[...] * acc


def _mlp_call(x2d, mod, norm_w, w_up, w_down, tm, blocks_per_batch):
    n = x2d.shape[0]
    r = mod.shape[1]
    row = pl.BlockSpec((tm, D_MODEL), lambda i: (i, 0))
    const = lambda a: pl.BlockSpec(a.shape, lambda i: (0,) * a.ndim)
    return pl.pallas_call(
        functools.partial(_mlp_kernel, ff_chunk=1024),
        grid=(n // tm,),
        in_specs=[row, _mod_spec(r, blocks_per_batch, 3), _mod_spec(r, blocks_per_batch, 4),
                  _mod_spec(r, blocks_per_batch, 5), const(norm_w), const(w_up), const(w_down)],
        out_specs=row,
        out_shape=jax.ShapeDtypeStruct((n, D_MODEL), f32),
        compiler_params=_params(("parallel",)),
        name="mlp",
    )(x2d, mod, mod, mod, norm_w, w_up, w_down)


def _layer_weights(l, w_in, w_out, w_up, w_down, norm_mix, norm_mlp, q_norm, k_norm, w_gate, b_gate,
                   gla_norm):
    wi = w_in[l]
    c_gate = 3 * W_SB + 2 * WK_GLA + WV_GLA
    w_in_r = jnp.concatenate(
        [wi[:, :c_gate], wi[:, c_gate + GATE_RANK:], wi[:, c_gate:c_gate + GATE_RANK],
         jnp.zeros((D_MODEL, GATE_PAD - GATE_RANK), f32)], axis=1).astype(bf16)
    wg = jnp.concatenate([w_gate[l], jnp.zeros((GATE_PAD - GATE_RANK, WK_GLA), f32)], axis=0).astype(bf16)
    head = np.arange(W_SB) // D_SB
    gmat = jnp.asarray((head[:, None] == head[None, :]).astype(np.float32) / D_SB, dtype=bf16)
    return {
        "w_in": w_in_r, "w_gate": wg, "gmat": gmat,
        "b_gate": b_gate[l].reshape(1, WK_GLA),
        "q_norm": jnp.tile(q_norm[l], H_SB).reshape(1, W_SB),
        "k_norm": jnp.tile(k_norm[l], H_SB).reshape(1, W_SB),
        "norm_mix": norm_mix[l].reshape(1, D_MODEL),
        "norm_mlp": norm_mlp[l].reshape(1, D_MODEL),
        "gla_norm": gla_norm[l].reshape(1, DV_GLA),
        "w_out": w_out[l].astype(bf16), "w_up": w_up[l].astype(bf16), "w_down": w_down[l].astype(bf16),
    }


def _group_layer(x2d, mod, lw, cw, n_batch, t, k_past, v_past, s0, tm, blocks_per_batch, chunk):
    q, k, v, qg, kg, vg, g, og = _in_call(x2d, mod, lw, tm, blocks_per_batch)
    if k_past is None:
        k_all, v_all, t_k, q_off = k, v, t, 0
    else:
        past = k_past.shape[1]
        t_k = -(-(past + t) // KB) * KB
        pad = jnp.zeros((n_batch, t_k - past - t, W_SB), f32)
        cat = lambda old, new: jnp.concatenate(
            [old.reshape(n_batch, past, W_SB), new.reshape(n_batch, t, W_SB), pad], axis=1
        ).reshape(n_batch * t_k, W_SB)
        k_all, v_all, q_off = cat(k_past, k), cat(v_past, v), past
    a = _sb_call(q, k_all, v_all, cw, n_batch, t, t_k, q_off)
    o, s_new = _gla_call(qg, kg, vg, g, og, s0, lw["gla_norm"], n_batch, t, chunk)
    x2d = _out_call(a, o, x2d, mod, lw["w_out"], tm, blocks_per_batch)
    x2d = _mlp_call(x2d, mod, lw["norm_mlp"], lw["w_up"], lw["w_down"], tm, blocks_per_batch)
    return x2d, k, v, s_new


def kernel(x_prompt, x_sample, c_prompt, c_sample, cache_k, cache_v, state_gla, w_ada, b_ada, norm_mix,
           norm_mlp, w_in, q_norm, k_norm, w_gate, b_gate, gla_norm, w_out, w_up, w_down):
    bp, tp, _ = x_prompt.shape
    bs, ts, _ = x_sample.shape
    n_c = bp + bs
    c_rows = -(-n_c // 16) * 16
    c_all = jnp.concatenate([c_prompt, c_sample, jnp.zeros((c_rows - n_c, D_MODEL), f32)], axis=0)
    mods = _ada_call(c_all, w_ada, b_ada)

    kk = np.arange(KB)
    cw = jnp.asarray(np.concatenate([(kk[:, None] > kk[None, :]), np.ones((KB, KB), bool)], axis=1)
                     .astype(np.float32), dtype=bf16)

    tm_p, tm_s = 512, bs * ts
    xp = x_prompt.reshape(bp * tp, D_MODEL)
    xs = x_sample.reshape(bs * ts, D_MODEL)
    kp_l, vp_l, sp_l, ks_l, vs_l, ss_l = [], [], [], [], [], []
    for l in range(DEPTH):
        lw = _layer_weights(l, w_in, w_out, w_up, w_down, norm_mix, norm_mlp, q_norm, k_norm, w_gate,
                            b_gate, gla_norm)
        mod_p = mods[l, :bp].reshape(bp, 1, N_MOD * D_MODEL)
        mod_s = jnp.repeat(mods[l, bp:n_c], ts, axis=0).reshape(1, bs * ts, N_MOD * D_MODEL)
        s0_p = jnp.zeros((bp, H_GLA, DK_GLA, DV_GLA), f32)
        xp, kp, vp, sp = _group_layer(xp, mod_p, lw, cw, bp, tp, None, None, s0_p, tm_p, tp // tm_p, 128)
        xs, kn, vn, sn = _group_layer(xs, mod_s, lw, cw, bs, ts, cache_k[l], cache_v[l],
                                      state_gla[l].astype(f32), tm_s, 1, ts)
        kp_l.append(kp.reshape(bp, tp, H_SB, D_SB))
        vp_l.append(vp.reshape(bp, tp, H_SB, D_SB))
        sp_l.append(sp)
        ks_l.append(kn.reshape(bs, ts, H_SB, D_SB))
        vs_l.append(vn.reshape(bs, ts, H_SB, D_SB))
        ss_l.append(sn)
    return (xp.reshape(bp, tp, D_MODEL), xs.reshape(bs, ts, D_MODEL),
            jnp.stack(kp_l), jnp.stack(vp_l), jnp.stack(sp_l),
            jnp.stack(ks_l), jnp.stack(vs_l), jnp.stack(ss_l))
```

```python
import functools

import numpy as np
import jax
import jax.numpy as jnp
from jax import lax
from jax.experimental import pallas as pl
from jax.experimental.pallas import tpu as pltpu

D_MODEL = 1024
DEPTH = 2
H_SB = 8
D_SB = 64
W_SB = H_SB * D_SB
H_GLA = 4
DK_GLA = 64
DV_GLA = 128
WK_GLA = H_GLA * DK_GLA
WV_GLA = H_GLA * DV_GLA
GATE_RANK = 16
GATE_TAU = 16.0
D_FF = 4 * D_MODEL
N_MOD = 6
EPS = 1e-6

LANES = 128
KB = 128
SB_GROUP = 3
N_PAIR = H_SB // 2
GATE_PAD = LANES
W_MAIN = W_SB + 2 * WK_GLA + 2 * WV_GLA + GATE_PAD
VMEM_LIMIT = 56 * 1024 * 1024
SURVIVAL_ZERO = 104.0
MASKED = -1e30

f32 = jnp.float32
bf16 = jnp.bfloat16
NT = (((1,), (1,)), ((), ()))
TN = (((0,), (0,)), ((), ()))


def _params(sem):
    return pltpu.CompilerParams(dimension_semantics=sem, vmem_limit_bytes=VMEM_LIMIT)


def _log_sigmoid(x):
    return jnp.minimum(x, 0.0) - jnp.log1p(jnp.exp(-jnp.abs(x)))


def _ada_kernel(c_ref, w_ref, b_ref, o_ref):
    c = c_ref[...]
    s = (c * jax.nn.sigmoid(c)).astype(bf16)
    o_ref[...] = jnp.dot(s, w_ref[...].astype(bf16), preferred_element_type=f32) + b_ref[...]


def _ada_call(c_all, w_ada, b_ada):
    rows = c_all.shape[0]
    return pl.pallas_call(
        _ada_kernel,
        grid=(DEPTH, N_MOD),
        in_specs=[
            pl.BlockSpec((rows, D_MODEL), lambda l, p: (0, 0)),
            pl.BlockSpec((None, D_MODEL, D_MODEL), lambda l, p: (l, 0, p)),
            pl.BlockSpec((None, 1, D_MODEL), lambda l, p: (l, 0, p)),
        ],
        out_specs=pl.BlockSpec((None, rows, D_MODEL), lambda l, p: (l, 0, p)),
        out_shape=jax.ShapeDtypeStruct((DEPTH, rows, N_MOD * D_MODEL), f32),
        compiler_params=_params(("parallel", "parallel")),
        name="ada_mod",
    )(c_all, w_ada, b_ada.reshape(DEPTH, 1, N_MOD * D_MODEL))


def _mod_spec(rows_per_block, blocks_per_batch, piece):
    return pl.BlockSpec((None, rows_per_block, D_MODEL),
                        lambda i: (i // blocks_per_batch, 0, piece))


def _in_kernel(x_ref, sh_ref, sc_ref, nw_ref, w_ref, wkv_ref, gm_ref, qn_ref, kn_ref, wg_ref, bg_ref,
               q_ref, kt_ref, vt_ref, ktb_ref, vtb_ref, qg_ref, kg_ref, vg_ref, g_ref, og_ref):
    x = x_ref[...]
    ms = jnp.mean(x * x, axis=-1, keepdims=True)
    y = x * lax.rsqrt(ms + EPS) * nw_ref[...]
    h = (y * (1.0 + sc_ref[...]) + sh_ref[...]).astype(bf16)

    def proj(lo, hi):
        return jnp.dot(h, w_ref[:, lo:hi], preferred_element_type=f32)

    kt = lax.dot_general(wkv_ref[:W_SB, :], h, NT, preferred_element_type=f32)
    mk = jnp.dot(gm_ref[...], (kt * kt).astype(bf16), preferred_element_type=f32)
    kt = kt * lax.rsqrt(mk + EPS) * kn_ref[...]
    kt_ref[...] = kt
    ktb_ref[...] = kt.astype(bf16)
    vt = lax.dot_general(wkv_ref[W_SB:, :], h, NT, preferred_element_type=f32)
    vt_ref[...] = vt
    vtb_ref[...] = vt.astype(bf16)

    o = 0
    q = proj(o, o + W_SB)
    mq = jnp.dot((q * q).astype(bf16), gm_ref[...], preferred_element_type=f32)
    q_ref[...] = (q * lax.rsqrt(mq + EPS) * qn_ref[...] * (D_SB ** -0.5)).astype(bf16)
    o += W_SB
    qg_ref[...] = proj(o, o + WK_GLA) * (DK_GLA ** -0.5)
    o += WK_GLA
    kg_ref[...] = proj(o, o + WK_GLA)
    o += WK_GLA
    vg_ref[...] = proj(o, o + WV_GLA)
    o += WV_GLA
    og_ref[...] = proj(o, o + WV_GLA)
    o += WV_GLA
    gr = proj(o, o + GATE_PAD).astype(bf16)
    xg = jnp.dot(gr, wg_ref[...], preferred_element_type=f32) + bg_ref[...]
    g_ref[...] = _log_sigmoid(xg) * (1.0 / GATE_TAU)


def _in_call(x2d, mod, lw, tm, blocks_per_batch):
    n = x2d.shape[0]
    r = mod.shape[1]
    n_batch = n // (tm * blocks_per_batch)
    t = tm * blocks_per_batch
    row = lambda w: pl.BlockSpec((tm, w), lambda i: (i, 0))
    col = pl.BlockSpec((None, W_SB, tm), lambda i: (i // blocks_per_batch, 0, i % blocks_per_batch))
    const = lambda a: pl.BlockSpec(a.shape, lambda i: (0,) * a.ndim)
    kn_t = jnp.broadcast_to(lw["k_norm"].reshape(W_SB, 1), (W_SB, tm))
    rows_out = [(WK_GLA, f32), (WK_GLA, f32), (WV_GLA, f32), (WK_GLA, f32), (WV_GLA, f32)]
    kv_shape = lambda dt: jax.ShapeDtypeStruct((n_batch, W_SB, t), dt)
    return pl.pallas_call(
        _in_kernel,
        grid=(n // tm,),
        in_specs=[row(D_MODEL), _mod_spec(r, blocks_per_batch, 0), _mod_spec(r, blocks_per_batch, 1),
                  const(lw["norm_mix"]), const(lw["w_main"]), const(lw["w_kvt"]), const(lw["gmat"]),
                  const(lw["q_norm"]), const(kn_t), const(lw["w_gate"]), const(lw["b_gate"])],
        out_specs=[row(W_SB), col, col, col, col] + [row(w) for w, _ in rows_out],
        out_shape=[jax.ShapeDtypeStruct((n, W_SB), bf16), kv_shape(f32), kv_shape(f32), kv_shape(bf16),
                   kv_shape(bf16)] + [jax.ShapeDtypeStruct((n, w), dt) for w, dt in rows_out],
        compiler_params=_params(("parallel",)),
        name="in_proj",
    )(x2d, mod, mod, lw["norm_mix"], lw["w_main"], lw["w_kvt"], lw["gmat"], lw["q_norm"], kn_t,
      lw["w_gate"], lw["b_gate"])


def _pair_diag(a, axis):
    z = jnp.zeros((D_SB, a.shape[1]), a.dtype)
    top = jnp.concatenate([a[:D_SB], z], axis=0)
    bot = jnp.concatenate([z, a[D_SB:]], axis=0)
    return jnp.concatenate([top, bot], axis=axis)


def _sb_pair_group(q_pair, k_tiles, v_tiles, biases, carries, acc, cw, tq):
    tiles = []
    for kt, bias in zip(k_tiles, biases):
        z2 = jnp.dot(q_pair, _pair_diag(kt, 1), preferred_element_type=f32)
        for hh in range(2):
            z = z2[:, hh * KB:(hh + 1) * KB]
            if bias is not None:
                z = z + bias
            sp = jnp.maximum(z, 0.0) + jnp.log(1.0 + jnp.exp(-jnp.abs(z)))
            hi = sp.astype(bf16)
            lo = (sp - hi.astype(f32)).astype(bf16)
            tiles.append((z - sp, jnp.concatenate([hi, lo], axis=1)))
    cs = jnp.dot(jnp.concatenate([t[1] for t in tiles], axis=0), cw, preferred_element_type=f32)
    new_carries = []
    ws = [[None, None] for _ in k_tiles]
    for hh in range(2):
        c = carries[hh]
        for bi in range(len(k_tiles)):
            ti = bi * 2 + hh
            part = cs[ti * tq:(ti + 1) * tq]
            ws[bi][hh] = jnp.exp(tiles[ti][0] - (c + part[:, :KB])).astype(bf16)
            c = c + part[:, KB:]
        new_carries.append(c)
    for bi, vt in enumerate(v_tiles):
        w_pair = jnp.concatenate(ws[bi], axis=1)
        acc = acc + lax.dot_general(w_pair, _pair_diag(vt, 1), NT, preferred_element_type=f32)
    return acc, new_carries


def _sb_kernel(q_ref, kd_ref, vd_ref, kp_ref, vp_ref, cw_ref, o_ref, acc_ref, carry_ref,
               *, tq, seg, n_past):
    b = pl.program_id(0)
    qi = pl.program_id(1)
    n_prev = qi if n_past is None else n_past
    row = lax.broadcasted_iota(jnp.int32, (tq, KB), 0)
    col = lax.broadcasted_iota(jnp.int32, (tq, KB), 1)
    seg_id = b % (KB // seg)
    seg_shift = seg.bit_length() - 1
    own = jnp.logical_and(lax.shift_right_logical(col, seg_shift) == seg_id,
                          jnp.bitwise_and(col, seg - 1) < row)
    bias0 = jnp.where(own, 0.0, MASKED)
    cw = cw_ref[...]

    def past_tiles(j):
        jc = pl.multiple_of(jnp.maximum(j, 0) * KB, KB)
        kt = kp_ref[:, pl.ds(jc, KB)].astype(bf16)
        vt = vp_ref[:, pl.ds(jc, KB)].astype(bf16)
        return kt, vt, jnp.where(j >= 0, 0.0, MASKED)

    def run_group(blocks, carries_of, acc_of):
        alive = None
        for p in range(N_PAIR):
            rows = slice(p * KB, (p + 1) * KB)
            acc, cs = _sb_pair_group(
                q_ref[:, rows], [kt[rows] for kt, _, _ in blocks], [vt[rows] for _, vt, _ in blocks],
                [bias for _, _, bias in blocks], carries_of(p), acc_of(p), cw, tq)
            acc_ref[p] = acc
            carry_ref[2 * p] = cs[0]
            carry_ref[2 * p + 1] = cs[1]
            m = jnp.minimum(cs[0], cs[1])
            alive = m if alive is None else jnp.minimum(alive, m)
        return (jnp.min(alive) < SURVIVAL_ZERO).astype(jnp.int32)

    j0 = n_prev - 1
    first = [(kd_ref[...], vd_ref[...], bias0)] + [past_tiles(j0 - d) for d in range(SB_GROUP - 1)]
    zero = jnp.zeros((tq, KB), f32)
    alive0 = run_group(first, lambda p: (zero, zero), lambda p: zero)

    def cond(st):
        j, alive = st
        return jnp.logical_and(j >= 0, alive > 0)

    def body(st):
        j, _ = st
        blocks = [past_tiles(j - d) for d in range(SB_GROUP)]
        alive = run_group(blocks, lambda p: (carry_ref[2 * p], carry_ref[2 * p + 1]), lambda p: acc_ref[p])
        return j - SB_GROUP, alive

    lax.while_loop(cond, body, (j0 - (SB_GROUP - 1), alive0))
    o_ref[...] = jnp.concatenate([acc_ref[p] for p in range(N_PAIR)], axis=1).astype(bf16)


def _sb_call(q, kd, vd, kp, vp, cw, *, n_batch, t_q, prompt, layer):
    tq = min(KB, t_q)
    nq = t_q // tq
    q_spec = pl.BlockSpec((tq, W_SB), lambda b, i: (b * nq + i, 0))
    if prompt:
        d_spec = pl.BlockSpec((None, W_SB, KB), lambda b, i: (b, 0, i))
        p_spec = pl.BlockSpec((None, W_SB, kp.shape[-1]), lambda b, i: (b, 0, 0))
        seg, n_past = KB, None
    else:
        d_spec = pl.BlockSpec((None, W_SB, KB), lambda b, i: (0, 0, (b * t_q) // KB))
        p_spec = pl.BlockSpec((None, None, W_SB, kp.shape[-1]), lambda b, i: (layer, b, 0, 0))
        seg, n_past = t_q, kp.shape[-1] // KB
    return pl.pallas_call(
        functools.partial(_sb_kernel, tq=tq, seg=seg, n_past=n_past),
        grid=(n_batch, nq),
        in_specs=[q_spec, d_spec, d_spec, p_spec, p_spec, pl.BlockSpec(cw.shape, lambda b, i: (0, 0))],
        out_specs=q_spec,
        out_shape=jax.ShapeDtypeStruct((n_batch * t_q, W_SB), bf16),
        scratch_shapes=[pltpu.VMEM((N_PAIR, tq, LANES), f32), pltpu.VMEM((H_SB, tq, LANES), f32)],
        compiler_params=_params(("parallel", "arbitrary")),
        name="stick_breaking",
    )(q, kd, vd, kp, vp, cw)


def _gla_tables(chunk):
    n_lev = chunk.bit_length() - 1
    idx = np.arange(chunk)
    t, s = idx[:, None], idx[None, :]
    mats = [(s <= t), (s > t)]
    qm, km = [], []
    for lev in range(n_lev):
        h = 1 << lev
        mid = (idx // (2 * h)) * (2 * h) + h
        is_q = (idx & h) != 0
        qm.append(is_q[:, None] & (s >= mid[:, None]) & (s <= t))
        km.append((~is_q)[:, None] & (s > t) & (s <= mid[:, None] - 1))
    stack = np.concatenate(mats + qm + km, axis=0).astype(np.float32)
    x = t ^ s
    level = np.where(t > s, np.floor(np.log2(np.maximum(x, 1))).astype(np.int32), -1)
    level = np.where(t < s, -2, level).astype(np.int32)
    return jnp.asarray(stack, dtype=bf16), jnp.asarray(level), n_lev


def _gla_kernel(q_ref, k_ref, v_ref, g_ref, og_ref, s0_ref, st_ref, lev_ref, gn_ref,
                o_ref, sout_ref, s_ref, *, chunk, n_lev):
    c = pl.program_id(2)

    @pl.when(c == 0)
    def _():
        s_ref[...] = s0_ref[...]

    L = chunk
    q = q_ref[...]
    k = k_ref[...]
    g = g_ref[...]
    g_hi = g.astype(bf16)
    g_lo = (g - g_hi.astype(f32)).astype(bf16)
    g2 = jnp.concatenate([g_hi, g_lo], axis=1)
    sums = jnp.dot(st_ref[...], g2, preferred_element_type=f32)
    sums = sums[:, :LANES] + sums[:, LANES:]
    part = lambda i: sums[i * L:(i + 1) * L]
    b_incl = part(0)
    to_end = part(1)

    lane = lax.broadcasted_iota(jnp.int32, (L, LANES), 1)
    first_head = lane < DK_GLA
    heads = lambda a: (jnp.where(first_head, a, 0.0).astype(bf16), jnp.where(first_head, 0.0, a).astype(bf16))

    q_in = heads(q * jnp.exp(b_incl))
    k_end = heads(k * jnp.exp(to_end))
    q_plain = heads(q)
    kb = k.astype(bf16)
    level = lev_ref[...]
    att = [jnp.where(level == -1, lax.dot_general(q_plain[hh], kb, NT, preferred_element_type=f32), 0.0)
           for hh in range(2)]
    for lev in range(n_lev):
        q_l = heads(q * jnp.exp(part(2 + lev)))
        k_l = (k * jnp.exp(part(2 + n_lev + lev))).astype(bf16)
        for hh in range(2):
            a = lax.dot_general(q_l[hh], k_l, NT, preferred_element_type=f32)
            att[hh] = jnp.where(level == lev, a, att[hh])

    tot = lax.dot_general(g2, jnp.ones((L, LANES), bf16), TN, preferred_element_type=f32)
    state_decay = jnp.exp(tot[:LANES] + tot[LANES:])

    s_old = s_ref[...]
    s_b = s_old.astype(bf16)
    v = v_ref[...]
    og = og_ref[...]
    gn = gn_ref[...]
    s_new = state_decay * s_old
    outs = []
    for hh in range(2):
        vh = v[:, hh * DV_GLA:(hh + 1) * DV_GLA].astype(bf16)
        o = (jnp.dot(att[hh].astype(bf16), vh, preferred_element_type=f32)
             + jnp.dot(q_in[hh], s_b, preferred_element_type=f32))
        s_new = s_new + lax.dot_general(k_end[hh], vh, TN, preferred_element_type=f32)
        ms = jnp.mean(o * o, axis=-1, keepdims=True)
        on = o * lax.rsqrt(ms + EPS) * gn
        gate = og[:, hh * DV_GLA:(hh + 1) * DV_GLA]
        outs.append(on * (gate * jax.nn.sigmoid(gate)))
    s_ref[...] = s_new
    o_ref[...] = jnp.concatenate(outs, axis=1).astype(bf16)

    @pl.when(c == pl.num_programs(2) - 1)
    def _():
        sout_ref[...] = s_new


def _gla_call(qg, kg, vg, g, og, s0, gla_norm, n_batch, t, chunk):
    nc = t // chunk
    n_pair = H_GLA // 2
    stack, level, n_lev = _gla_tables(chunk)
    kspec = pl.BlockSpec((chunk, LANES), lambda b, p, c: (b * nc + c, p))
    vspec = pl.BlockSpec((chunk, 2 * DV_GLA), lambda b, p, c: (b * nc + c, p))
    sspec = pl.BlockSpec((None, None, 2 * DK_GLA, DV_GLA), lambda b, p, c: (b, p, 0, 0))
    const = lambda a: pl.BlockSpec(a.shape, lambda b, p, c: (0,) * a.ndim)
    s0 = s0.reshape(n_batch, n_pair, 2 * DK_GLA, DV_GLA)
    o, s_out = pl.pallas_call(
        functools.partial(_gla_kernel, chunk=chunk, n_lev=n_lev),
        grid=(n_batch, n_pair, nc),
        in_specs=[kspec, kspec, vspec, kspec, vspec, sspec, const(stack), const(level), const(gla_norm)],
        out_specs=[vspec, sspec],
        out_shape=[jax.ShapeDtypeStruct((n_batch * t, WV_GLA), bf16),
                   jax.ShapeDtypeStruct(s0.shape, f32)],
        scratch_shapes=[pltpu.VMEM((2 * DK_GLA, DV_GLA), f32)],
        compiler_params=_params(("parallel", "parallel", "arbitrary")),
        name="gla",
    )(qg, kg, vg, g, og, s0, stack, level, gla_norm)
    return o, s_out.reshape(n_batch, H_GLA, DK_GLA, DV_GLA)


def _out_kernel(a_ref, o_ref, x_ref, gate_ref, w_ref, y_ref):
    m = (jnp.dot(a_ref[...], w_ref[:W_SB, :], preferred_element_type=f32)
         + jnp.dot(o_ref[...], w_ref[W_SB:, :], preferred_element_type=f32))
    y_ref[...] = x_ref[...] + gate_ref[...] * m


def _out_call(a, o, x2d, mod, w_out, tm, blocks_per_batch):
    n = x2d.shape[0]
    r = mod.shape[1]
    row = lambda w: pl.BlockSpec((tm, w), lambda i: (i, 0))
    return pl.pallas_call(
        _out_kernel,
        grid=(n // tm,),
        in_specs=[row(W_SB), row(WV_GLA), row(D_MODEL), _mod_spec(r, blocks_per_batch, 2),
                  pl.BlockSpec(w_out.shape, lambda i: (0, 0))],
        out_specs=row(D_MODEL),
        out_shape=jax.ShapeDtypeStruct((n, D_MODEL), f32),
        compiler_params=_params(("parallel",)),
        name="out_proj",
    )(a, o, x2d, mod, w_out)


def _mlp_kernel(x_ref, sh_ref, sc_ref, gate_ref, nw_ref, wu_ref, wd_ref, y_ref, *, ff_chunk):
    x = x_ref[...]
    ms = jnp.mean(x * x, axis=-1, keepdims=True)
    y = x * lax.rsqrt(ms + EPS) * nw_ref[...]
    h = (y * (1.0 + sc_ref[...]) + sh_ref[...]).astype(bf16)
    acc = jnp.zeros(x.shape, f32)
    for c0 in range(0, D_FF, ff_chunk):
        u = jnp.dot(h, wu_ref[:, c0:c0 + ff_chunk], preferred_element_type=f32)
        u = jnp.square(jnp.maximum(u, 0.0)).astype(bf16)
        acc = acc + jnp.dot(u, wd_ref[c0:c0 + ff_chunk, :], preferred_element_type=f32)
    y_ref[...] = x + gate_ref[...] * acc


def _mlp_call(x2d, mod, norm_w, w_up, w_down, tm, blocks_per_batch):
    n = x2d.shape[0]
    r = mod.shape[1]
    row = pl.BlockSpec((tm, D_MODEL), lambda i: (i, 0))
    const = lambda a: pl.BlockSpec(a.shape, lambda i: (0,) * a.ndim)
    return pl.pallas_call(
        functools.partial(_mlp_kernel, ff_chunk=1024),
        grid=(n // tm,),
        in_specs=[row, _mod_spec(r, blocks_per_batch, 3), _mod_spec(r, blocks_per_batch, 4),
                  _mod_spec(r, blocks_per_batch, 5), const(norm_w), const(w_up), const(w_down)],
        out_specs=row,
        out_shape=jax.ShapeDtypeStruct((n, D_MODEL), f32),
        compiler_params=_params(("parallel",)),
        name="mlp",
    )(x2d, mod, mod, mod, norm_w, w_up, w_down)


def _layer_weights(l, w_in, w_out, w_up, w_down, norm_mix, norm_mlp, q_norm, k_norm, w_gate, b_gate,
                   gla_norm):
    wi = w_in[l]
    c_k, c_v, c_g = W_SB, 2 * W_SB, 3 * W_SB
    c_gate = c_g + 2 * WK_GLA + WV_GLA
    w_main = jnp.concatenate(
        [wi[:, :c_k], wi[:, c_g:c_gate], wi[:, c_gate + GATE_RANK:], wi[:, c_gate:c_gate + GATE_RANK],
         jnp.zeros((D_MODEL, GATE_PAD - GATE_RANK), f32)], axis=1).astype(bf16)
    w_kvt = wi[:, c_k:c_g].T.astype(bf16)
    wg = jnp.concatenate([w_gate[l], jnp.zeros((GATE_PAD - GATE_RANK, WK_GLA), f32)], axis=0).astype(bf16)
    head = np.arange(W_SB) // D_SB
    gmat = jnp.asarray((head[:, None] == head[None, :]).astype(np.float32) / D_SB, dtype=bf16)
    return {
        "w_main": w_main, "w_kvt": w_kvt, "w_gate": wg, "gmat": gmat,
        "b_gate": b_gate[l].reshape(1, WK_GLA),
        "q_norm": jnp.tile(q_norm[l], H_SB).reshape(1, W_SB),
        "k_norm": jnp.tile(k_norm[l], H_SB),
        "norm_mix": norm_mix[l].reshape(1, D_MODEL),
        "norm_mlp": norm_mlp[l].reshape(1, D_MODEL),
        "gla_norm": gla_norm[l].reshape(1, DV_GLA),
        "w_out": w_out[l].astype(bf16), "w_up": w_up[l].astype(bf16), "w_down": w_down[l].astype(bf16),
    }


def _group_layer(x2d, mod, lw, cw, n_batch, t, past_kt, past_vt, layer, s0, tm, blocks_per_batch, chunk):
    q, kt, vt, ktb, vtb, qg, kg, vg, g, og = _in_call(x2d, mod, lw, tm, blocks_per_batch)
    if past_kt is None:
        a = _sb_call(q, ktb, vtb, ktb, vtb, cw, n_batch=n_batch, t_q=t, prompt=True, layer=layer)
    else:
        a = _sb_call(q, ktb, vtb, past_kt, past_vt, cw, n_batch=n_batch, t_q=t, prompt=False, layer=layer)
    o, s_new = _gla_call(qg, kg, vg, g, og, s0, lw["gla_norm"], n_batch, t, chunk)
    x2d = _out_call(a, o, x2d, mod, lw["w_out"], tm, blocks_per_batch)
    x2d = _mlp_call(x2d, mod, lw["norm_mlp"], lw["w_up"], lw["w_down"], tm, blocks_per_batch)
    return x2d, kt, vt, s_new


def _heads_last(kt, n_batch, t):
    return kt.reshape(kt.shape[0], H_SB, D_SB, -1).transpose(0, 3, 1, 2).reshape(n_batch, t, H_SB, D_SB)


def kernel(x_prompt, x_sample, c_prompt, c_sample, cache_k, cache_v, state_gla, w_ada, b_ada, norm_mix,
           norm_mlp, w_in, q_norm, k_norm, w_gate, b_gate, gla_norm, w_out, w_up, w_down):
    bp, tp, _ = x_prompt.shape
    bs, ts, _ = x_sample.shape
    past = cache_k.shape[2]
    n_c = bp + bs
    c_rows = -(-n_c // 16) * 16
    c_all = jnp.concatenate([c_prompt, c_sample, jnp.zeros((c_rows - n_c, D_MODEL), f32)], axis=0)
    mods = _ada_call(c_all, w_ada, b_ada)

    kk = np.arange(KB)
    half = np.concatenate([(kk[:, None] > kk[None, :]), np.ones((KB, KB), bool)], axis=1)
    cw = jnp.asarray(np.concatenate([half, half], axis=0).astype(np.float32), dtype=bf16)

    cache_kt = cache_k.transpose(0, 1, 3, 4, 2).reshape(DEPTH, bs, W_SB, past)
    cache_vt = cache_v.transpose(0, 1, 3, 4, 2).reshape(DEPTH, bs, W_SB, past)

    tm_p, tm_s = 512, bs * ts
    xp = x_prompt.reshape(bp * tp, D_MODEL)
    xs = x_sample.reshape(bs * ts, D_MODEL)
    kp_l, vp_l, sp_l, ks_l, vs_l, ss_l = [], [], [], [], [], []
    for l in range(DEPTH):
        lw = _layer_weights(l, w_in, w_out, w_up, w_down, norm_mix, norm_mlp, q_norm, k_norm, w_gate,
                            b_gate, gla_norm)
        mod_p = mods[l, :bp].reshape(bp, 1, N_MOD * D_MODEL)
        mod_s = jnp.repeat(mods[l, bp:n_c], ts, axis=0).reshape(1, bs * ts, N_MOD * D_MODEL)
        s0_p = jnp.zeros((bp, H_GLA, DK_GLA, DV_GLA), f32)
        xp, kp, vp, sp = _group_layer(xp, mod_p, lw, cw, bp, tp, None, None, l, s0_p, tm_p, tp // tm_p, 128)
        xs, kn, vn, sn = _group_layer(xs, mod_s, lw, cw, bs, ts, cache_kt, cache_vt, l,
                                      state_gla[l].astype(f32), tm_s, 1, ts)
        kp_l.append(_heads_last(kp, bp, tp))
        vp_l.append(_heads_last(vp, bp, tp))
        sp_l.append(sp)
        ks_l.append(_heads_last(kn, bs, ts))
        vs_l.append(_heads_last(vn, bs, ts))
        ss_l.append(sn)
    return (xp.reshape(bp, tp, D_MODEL), xs.reshape(bs, ts, D_MODEL),
            jnp.stack(kp_l), jnp.stack(vp_l), jnp.stack(sp_l),
            jnp.stack(ks_l), jnp.stack(vs_l), jnp.stack(ss_l))
```

```python
import functools

import numpy as np
import jax
import jax.numpy as jnp
from jax import lax
from jax.experimental import pallas as pl
from jax.experimental.pallas import tpu as pltpu

D_MODEL = 1024
DEPTH = 2
H_SB = 8
D_SB = 64
W_SB = H_SB * D_SB
H_GLA = 4
DK_GLA = 64
DV_GLA = 128
WK_GLA = H_GLA * DK_GLA
WV_GLA = H_GLA * DV_GLA
GATE_RANK = 16
GATE_TAU = 16.0
D_FF = 4 * D_MODEL
N_MOD = 6
EPS = 1e-6

LANES = 128
KB = 128
SB_GROUP = 3
N_PAIR = H_SB // 2
GATE_PAD = LANES
W_MAIN = W_SB + 2 * WK_GLA + 2 * WV_GLA + GATE_PAD
VMEM_LIMIT = 56 * 1024 * 1024
SURVIVAL_ZERO = 104.0
MASKED = -1e30

f32 = jnp.float32
bf16 = jnp.bfloat16
NT = (((1,), (1,)), ((), ()))
TN = (((0,), (0,)), ((), ()))


def _params(sem):
    return pltpu.CompilerParams(dimension_semantics=sem, vmem_limit_bytes=VMEM_LIMIT)


def _log_sigmoid(x):
    return jnp.minimum(x, 0.0) - jnp.log1p(jnp.exp(-jnp.abs(x)))


def _ada_kernel(c_ref, w_ref, b_ref, o_ref):
    c = c_ref[...]
    s = (c * jax.nn.sigmoid(c)).astype(bf16)
    o_ref[...] = jnp.dot(s, w_ref[...].astype(bf16), preferred_element_type=f32) + b_ref[...]


def _ada_call(c_all, w_ada, b_ada):
    rows = c_all.shape[0]
    return pl.pallas_call(
        _ada_kernel,
        grid=(DEPTH, N_MOD),
        in_specs=[
            pl.BlockSpec((rows, D_MODEL), lambda l, p: (0, 0)),
            pl.BlockSpec((None, D_MODEL, D_MODEL), lambda l, p: (l, 0, p)),
            pl.BlockSpec((None, 1, D_MODEL), lambda l, p: (l, 0, p)),
        ],
        out_specs=pl.BlockSpec((None, rows, D_MODEL), lambda l, p: (l, 0, p)),
        out_shape=jax.ShapeDtypeStruct((DEPTH, rows, N_MOD * D_MODEL), f32),
        compiler_params=_params(("parallel", "parallel")),
        name="ada_mod",
    )(c_all, w_ada, b_ada.reshape(DEPTH, 1, N_MOD * D_MODEL))


def _mod_spec(rows_per_block, blocks_per_batch, piece):
    return pl.BlockSpec((None, rows_per_block, D_MODEL),
                        lambda i: (i // blocks_per_batch, 0, piece))


def _in_kernel(x_ref, sh_ref, sc_ref, nw_ref, w_ref, wkv_ref, gm_ref, qn_ref, kn_ref, wg_ref, bg_ref,
               q_ref, kt_ref, vt_ref, ktb_ref, vtb_ref, qg_ref, kg_ref, vg_ref, g_ref, og_ref):
    x = x_ref[...]
    ms = jnp.mean(x * x, axis=-1, keepdims=True)
    y = x * lax.rsqrt(ms + EPS) * nw_ref[...]
    h = (y * (1.0 + sc_ref[...]) + sh_ref[...]).astype(bf16)

    def proj(lo, hi):
        return jnp.dot(h, w_ref[:, lo:hi], preferred_element_type=f32)

    kt = lax.dot_general(wkv_ref[:W_SB, :], h, NT, preferred_element_type=f32)
    mk = jnp.dot(gm_ref[...], (kt * kt).astype(bf16), preferred_element_type=f32)
    kt = kt * lax.rsqrt(mk + EPS) * kn_ref[...]
    kt_ref[...] = kt
    ktb_ref[...] = kt.astype(bf16)
    vt = lax.dot_general(wkv_ref[W_SB:, :], h, NT, preferred_element_type=f32)
    vt_ref[...] = vt
    vtb_ref[...] = vt.astype(bf16)

    o = 0
    q = proj(o, o + W_SB)
    mq = jnp.dot((q * q).astype(bf16), gm_ref[...], preferred_element_type=f32)
    q_ref[...] = (q * lax.rsqrt(mq + EPS) * qn_ref[...] * (D_SB ** -0.5)).astype(bf16)
    o += W_SB
    qg_ref[...] = proj(o, o + WK_GLA) * (DK_GLA ** -0.5)
    o += WK_GLA
    kg_ref[...] = proj(o, o + WK_GLA)
    o += WK_GLA
    vg_ref[...] = proj(o, o + WV_GLA)
    o += WV_GLA
    og_ref[...] = proj(o, o + WV_GLA)
    o += WV_GLA
    gr = proj(o, o + GATE_PAD).astype(bf16)
    xg = jnp.dot(gr, wg_ref[...], preferred_element_type=f32) + bg_ref[...]
    g_ref[...] = _log_sigmoid(xg) * (1.0 / GATE_TAU)


def _in_call(x2d, mod, lw, tm, blocks_per_batch):
    n = x2d.shape[0]
    r = mod.shape[1]
    n_batch = n // (tm * blocks_per_batch)
    t = tm * blocks_per_batch
    row = lambda w: pl.BlockSpec((tm, w), lambda i: (i, 0))
    col = pl.BlockSpec((None, W_SB, tm), lambda i: (i // blocks_per_batch, 0, i % blocks_per_batch))
    const = lambda a: pl.BlockSpec(a.shape, lambda i: (0,) * a.ndim)
    kn_t = jnp.broadcast_to(lw["k_norm"].reshape(W_SB, 1), (W_SB, tm))
    rows_out = [(WK_GLA, f32), (WK_GLA, f32), (WV_GLA, f32), (WK_GLA, f32), (WV_GLA, f32)]
    kv_shape = lambda dt: jax.ShapeDtypeStruct((n_batch, W_SB, t), dt)
    return pl.pallas_call(
        _in_kernel,
        grid=(n // tm,),
        in_specs=[row(D_MODEL), _mod_spec(r, blocks_per_batch, 0), _mod_spec(r, blocks_per_batch, 1),
                  const(lw["norm_mix"]), const(lw["w_main"]), const(lw["w_kvt"]), const(lw["gmat"]),
                  const(lw["q_norm"]), const(kn_t), const(lw["w_gate"]), const(lw["b_gate"])],
        out_specs=[row(W_SB), col, col, col, col] + [row(w) for w, _ in rows_out],
        out_shape=[jax.ShapeDtypeStruct((n, W_SB), bf16), kv_shape(f32), kv_shape(f32), kv_shape(bf16),
                   kv_shape(bf16)] + [jax.ShapeDtypeStruct((n, w), dt) for w, dt in rows_out],
        compiler_params=_params(("parallel",)),
        name="in_proj",
    )(x2d, mod, mod, lw["norm_mix"], lw["w_main"], lw["w_kvt"], lw["gmat"], lw["q_norm"], kn_t,
      lw["w_gate"], lw["b_gate"])


def _pair_diag(a, axis):
    z = jnp.zeros((D_SB, a.shape[1]), a.dtype)
    top = jnp.concatenate([a[:D_SB], z], axis=0)
    bot = jnp.concatenate([z, a[D_SB:]], axis=0)
    return jnp.concatenate([top, bot], axis=axis)


def _sb_pair_group(q_pair, k_tiles, v_tiles, biases, carries, acc, cw, tq):
    tiles = []
    for kt, bias in zip(k_tiles, biases):
        z2 = jnp.dot(q_pair, _pair_diag(kt, 1), preferred_element_type=f32)
        for hh in range(2):
            z = z2[:, hh * KB:(hh + 1) * KB]
            if bias is not None:
                z = z + bias
            sp = jnp.maximum(z, 0.0) + jnp.log(1.0 + jnp.exp(-jnp.abs(z)))
            hi = sp.astype(bf16)
            lo = (sp - hi.astype(f32)).astype(bf16)
            tiles.append((z - sp, jnp.concatenate([hi, lo], axis=1)))
    cs = jnp.dot(jnp.concatenate([t[1] for t in tiles], axis=0), cw, preferred_element_type=f32)
    new_carries = []
    ws = [[None, None] for _ in k_tiles]
    for hh in range(2):
        c = carries[hh]
        for bi in range(len(k_tiles)):
            ti = bi * 2 + hh
            part = cs[ti * tq:(ti + 1) * tq]
            ws[bi][hh] = jnp.exp(tiles[ti][0] - (c + part[:, :KB])).astype(bf16)
            c = c + part[:, KB:]
        new_carries.append(c)
    for bi, vt in enumerate(v_tiles):
        w_pair = jnp.concatenate(ws[bi], axis=1)
        acc = acc + lax.dot_general(w_pair, _pair_diag(vt, 1), NT, preferred_element_type=f32)
    return acc, new_carries


def _sb_kernel(q_ref, kd_ref, vd_ref, kp_ref, vp_ref, cw_ref, o_ref, acc_ref, carry_ref,
               *, tq, seg, n_past):
    b = pl.program_id(0)
    qi = pl.program_id(1)
    n_prev = qi if n_past is None else n_past
    row = lax.broadcasted_iota(jnp.int32, (tq, KB), 0)
    col = lax.broadcasted_iota(jnp.int32, (tq, KB), 1)
    seg_id = b % (KB // seg)
    seg_shift = seg.bit_length() - 1
    own = jnp.logical_and(lax.shift_right_logical(col, seg_shift) == seg_id,
                          jnp.bitwise_and(col, seg - 1) < row)
    bias0 = jnp.where(own, 0.0, MASKED)
    cw = cw_ref[...]

    def past_tiles(j):
        jc = pl.multiple_of(jnp.maximum(j, 0) * KB, KB)
        kt = kp_ref[:, pl.ds(jc, KB)].astype(bf16)
        vt = vp_ref[:, pl.ds(jc, KB)].astype(bf16)
        return kt, vt, jnp.where(j >= 0, 0.0, MASKED)

    def run_group(blocks, carries_of, acc_of):
        alive = None
        for p in range(N_PAIR):
            rows = slice(p * KB, (p + 1) * KB)
            acc, cs = _sb_pair_group(
                q_ref[:, rows], [kt[rows] for kt, _, _ in blocks], [vt[rows] for _, vt, _ in blocks],
                [bias for _, _, bias in blocks], carries_of(p), acc_of(p), cw, tq)
            acc_ref[p] = acc
            carry_ref[2 * p] = cs[0]
            carry_ref[2 * p + 1] = cs[1]
            m = jnp.minimum(cs[0], cs[1])
            alive = m if alive is None else jnp.minimum(alive, m)
        return (jnp.min(alive) < SURVIVAL_ZERO).astype(jnp.int32)

    j0 = n_prev - 1
    first = [(kd_ref[...], vd_ref[...], bias0)] + [past_tiles(j0 - d) for d in range(SB_GROUP - 1)]
    zero = jnp.zeros((tq, KB), f32)
    alive0 = run_group(first, lambda p: (zero, zero), lambda p: zero)

    def cond(st):
        j, alive = st
        return jnp.logical_and(j >= 0, alive > 0)

    def body(st):
        j, _ = st
        blocks = [past_tiles(j - d) for d in range(SB_GROUP)]
        alive = run_group(blocks, lambda p: (carry_ref[2 * p], carry_ref[2 * p + 1]), lambda p: acc_ref[p])
        return j - SB_GROUP, alive

    lax.while_loop(cond, body, (j0 - (SB_GROUP - 1), alive0))
    o_ref[...] = jnp.concatenate([acc_ref[p] for p in range(N_PAIR)], axis=1).astype(bf16)


def _sb_call(q, kd, vd, kp, vp, cw, *, n_batch, t_q, prompt, layer):
    tq = min(KB, t_q)
    nq = t_q // tq
    q_spec = pl.BlockSpec((tq, W_SB), lambda b, i: (b * nq + i, 0))
    if prompt:
        d_spec = pl.BlockSpec((None, W_SB, KB), lambda b, i: (b, 0, i))
        p_spec = pl.BlockSpec((None, W_SB, kp.shape[-1]), lambda b, i: (b, 0, 0))
        seg, n_past = KB, None
    else:
        d_spec = pl.BlockSpec((None, W_SB, KB), lambda b, i: (0, 0, (b * t_q) // KB))
        p_spec = pl.BlockSpec((None, None, W_SB, kp.shape[-1]), lambda b, i: (layer, b, 0, 0))
        seg, n_past = t_q, kp.shape[-1] // KB
    return pl.pallas_call(
        functools.partial(_sb_kernel, tq=tq, seg=seg, n_past=n_past),
        grid=(n_batch, nq),
        in_specs=[q_spec, d_spec, d_spec, p_spec, p_spec, pl.BlockSpec(cw.shape, lambda b, i: (0, 0))],
        out_specs=q_spec,
        out_shape=jax.ShapeDtypeStruct((n_batch * t_q, W_SB), bf16),
        scratch_shapes=[pltpu.VMEM((N_PAIR, tq, LANES), f32), pltpu.VMEM((H_SB, tq, LANES), f32)],
        compiler_params=_params(("parallel", "arbitrary")),
        name="stick_breaking",
    )(q, kd, vd, kp, vp, cw)


GLA_PAIRS = H_GLA // 2
GLA_SAFE_DECAY = 60.0


def _gla_tables(chunk):
    idx = np.arange(chunk)
    t, s = idx[:, None], idx[None, :]
    tri = jnp.asarray((s <= t).astype(np.float32), dtype=bf16)
    level = np.where(t > s, np.floor(np.log2(np.maximum(t ^ s, 1))).astype(np.int32), -1)
    level = np.where(t < s, -2, level).astype(np.int32)
    return tri, jnp.asarray(np.concatenate([level, level], axis=1))


def _level_reference(b, h):
    L = b.shape[0]
    if h >= 4:
        return jnp.concatenate(
            [jnp.broadcast_to(b[c0 + h - 1:c0 + h, :], (2 * h, LANES)) for c0 in range(0, L, 2 * h)], axis=0)
    b3 = b.reshape(L // 8, 8, LANES)
    pos = jnp.bitwise_and(lax.broadcasted_iota(jnp.int32, b3.shape, 1), 2 * h - 1)
    up = lambda n: pltpu.roll(b3, n, 1)
    if h == 1:
        ref = jnp.where(pos == 0, b3, up(1))
    else:
        ref = jnp.where(pos == 0, up(7), jnp.where(pos == 1, b3, jnp.where(pos == 2, up(1), up(2))))
    return ref.reshape(L, LANES)


def _two_heads(a, first_head):
    return jnp.concatenate([jnp.where(first_head, a, 0.0).astype(bf16),
                            jnp.where(first_head, 0.0, a).astype(bf16)], axis=0)


def _gla_att_single(q_in, k, b, lev2, first_head):
    a = lax.dot_general(q_in, _two_heads(k * jnp.exp(-b), first_head), NT, preferred_element_type=f32)
    return jnp.where(lev2 >= -1, a, 0.0)


def _gla_att_levels(q, k, b, lev2, first_head, n_lev):
    a = lax.dot_general(q.astype(bf16), _two_heads(k, first_head), NT, preferred_element_type=f32)
    att = jnp.where(lev2 == -1, a, 0.0)
    for lev in range(n_lev):
        d = b - _level_reference(b, 1 << lev)
        e = jnp.exp(-jnp.abs(d))
        is_query = d < 0.0
        q_l = (q * jnp.where(is_query, e, 1.0)).astype(bf16)
        k_l = k * jnp.where(is_query, 1.0, e)
        a = lax.dot_general(q_l, _two_heads(k_l, first_head), NT, preferred_element_type=f32)
        att = jnp.where(lev2 == lev, a, att)
    return att


def _gla_kernel(q_ref, k_ref, v_ref, g_ref, og_ref, s0_ref, tri_ref, lev_ref, gn_ref,
                o_ref, sout_ref, s_ref, att_ref, *, chunk, bb):
    c = pl.program_id(1)
    L = chunk
    n_lev = L.bit_length() - 1
    chains = [(bi, p) for bi in range(bb) for p in range(GLA_PAIRS)]
    zeros_s = jnp.zeros((DK_GLA, DV_GLA), f32)

    @pl.when(c == 0)
    def _():
        for bi, p in chains:
            s0 = s0_ref[bi, p]
            s_ref[bi, p] = jnp.concatenate(
                [jnp.concatenate([s0[:DK_GLA], zeros_s], axis=1),
                 jnp.concatenate([zeros_s, s0[DK_GLA:]], axis=1)], axis=0)

    lane = lax.broadcasted_iota(jnp.int32, (L, LANES), 1)
    first_head = lane < DK_GLA
    lev2 = lev_ref[...]
    tri = tri_ref[...]

    pre = []
    floor = None
    for bi, p in chains:
        lanes = slice(p * LANES, (p + 1) * LANES)
        q = q_ref[bi, :, lanes]
        k = k_ref[bi, :, lanes]
        g = g_ref[bi, :, lanes]
        g_hi = g.astype(bf16)
        g_lo = (g - g_hi.astype(f32)).astype(bf16)
        g2 = jnp.concatenate([g_hi, g_lo], axis=1)
        bs = jnp.dot(tri, g2, preferred_element_type=f32)
        b = bs[:, :LANES] + bs[:, LANES:]
        b_last = b[L - 1:L, :]
        q_in = (q * jnp.exp(b)).astype(bf16)
        k_end = (k * jnp.exp(b_last - b)).astype(bf16)
        pre.append((q, k, g2, b, q_in, k_end))
        floor = b_last if floor is None else jnp.minimum(floor, b_last)
    single = jnp.min(floor) >= -GLA_SAFE_DECAY

    @pl.when(single)
    def _():
        for ci, (q, k, g2, b, q_in, k_end) in enumerate(pre):
            att_ref[ci] = _gla_att_single(q_in, k, b, lev2, first_head)

    @pl.when(jnp.logical_not(single))
    def _():
        for ci, (q, k, g2, b, q_in, k_end) in enumerate(pre):
            att_ref[ci] = _gla_att_levels(q, k, b, lev2, first_head, n_lev)

    rowi = lax.broadcasted_iota(jnp.int32, (2 * DK_GLA, 2 * DV_GLA), 0)
    coli = lax.broadcasted_iota(jnp.int32, (2 * DK_GLA, 2 * DV_GLA), 1)
    own_block = (rowi < DK_GLA) == (coli < DV_GLA)
    ones = jnp.ones((L, LANES), bf16)
    zeros_v = jnp.zeros((L, DV_GLA), bf16)
    gn = gn_ref[...]
    for ci, (bi, p) in enumerate(chains):
        q, k, g2, b, q_in, k_end = pre[ci]
        cols = slice(p * 2 * DV_GLA, (p + 1) * 2 * DV_GLA)
        v = v_ref[bi, :, cols].astype(bf16)
        v_diag = jnp.concatenate([jnp.concatenate([v[:, :DV_GLA], zeros_v], axis=1),
                                  jnp.concatenate([zeros_v, v[:, DV_GLA:]], axis=1)], axis=0)
        s_old = s_ref[bi, p]
        o2 = (jnp.dot(att_ref[ci].astype(bf16), v_diag, preferred_element_type=f32)
              + jnp.dot(q_in, s_old.astype(bf16), preferred_element_type=f32))
        tot = lax.dot_general(g2, ones, TN, preferred_element_type=f32)
        decay = jnp.exp(tot[:LANES] + tot[LANES:])
        upd = lax.dot_general(k_end, v, TN, preferred_element_type=f32)
        s_ref[bi, p] = jnp.concatenate([decay, decay], axis=1) * s_old + jnp.where(own_block, upd, 0.0)
        og = og_ref[bi, :, cols]
        outs = []
        for hh in range(2):
            o = o2[:, hh * DV_GLA:(hh + 1) * DV_GLA]
            ms = jnp.mean(o * o, axis=-1, keepdims=True)
            gate = og[:, hh * DV_GLA:(hh + 1) * DV_GLA]
            outs.append(o * lax.rsqrt(ms + EPS) * gn * (gate * jax.nn.sigmoid(gate)))
        o_ref[bi, :, cols] = jnp.concatenate(outs, axis=1).astype(bf16)

    @pl.when(c == pl.num_programs(1) - 1)
    def _():
        for bi, p in chains:
            s = s_ref[bi, p]
            sout_ref[bi, p] = jnp.concatenate([s[:DK_GLA, :DV_GLA], s[DK_GLA:, DV_GLA:]], axis=0)


def _gla_call(qg, kg, vg, g, og, s0, gla_norm, n_batch, t, chunk, bb):
    nc = t // chunk
    tri, lev2 = _gla_tables(chunk)
    kspec = pl.BlockSpec((bb, chunk, WK_GLA), lambda i, c: (i, c, 0))
    vspec = pl.BlockSpec((bb, chunk, WV_GLA), lambda i, c: (i, c, 0))
    sspec = pl.BlockSpec((bb, GLA_PAIRS, 2 * DK_GLA, DV_GLA), lambda i, c: (i, 0, 0, 0))
    const = lambda a: pl.BlockSpec(a.shape, lambda i, c: (0,) * a.ndim)
    s0 = s0.reshape(n_batch, GLA_PAIRS, 2 * DK_GLA, DV_GLA)
    k3 = lambda a: a.reshape(n_batch, t, WK_GLA)
    v3 = lambda a: a.reshape(n_batch, t, WV_GLA)
    o, s_out = pl.pallas_call(
        functools.partial(_gla_kernel, chunk=chunk, bb=bb),
        grid=(n_batch // bb, nc),
        in_specs=[kspec, kspec, vspec, kspec, vspec, sspec, const(tri), const(lev2), const(gla_norm)],
        out_specs=[vspec, sspec],
        out_shape=[jax.ShapeDtypeStruct((n_batch, t, WV_GLA), bf16),
                   jax.ShapeDtypeStruct(s0.shape, f32)],
        scratch_shapes=[pltpu.VMEM((bb, GLA_PAIRS, 2 * DK_GLA, 2 * DV_GLA), f32),
                        pltpu.VMEM((bb * GLA_PAIRS, chunk, 2 * chunk), f32)],
        compiler_params=_params(("parallel", "arbitrary")),
        name="gla",
    )(k3(qg), k3(kg), v3(vg), k3(g), v3(og), s0, tri, lev2, gla_norm)
    return o.reshape(n_batch * t, WV_GLA), s_out.reshape(n_batch, H_GLA, DK_GLA, DV_GLA)


def _out_kernel(a_ref, o_ref, x_ref, gate_ref, w_ref, y_ref):
    m = (jnp.dot(a_ref[...], w_ref[:W_SB, :], preferred_element_type=f32)
         + jnp.dot(o_ref[...], w_ref[W_SB:, :], preferred_element_type=f32))
    y_ref[...] = x_ref[...] + gate_ref[...] * m


def _out_call(a, o, x2d, mod, w_out, tm, blocks_per_batch):
    n = x2d.shape[0]
    r = mod.shape[1]
    row = lambda w: pl.BlockSpec((tm, w), lambda i: (i, 0))
    return pl.pallas_call(
        _out_kernel,
        grid=(n // tm,),
        in_specs=[row(W_SB), row(WV_GLA), row(D_MODEL), _mod_spec(r, blocks_per_batch, 2),
                  pl.BlockSpec(w_out.shape, lambda i: (0, 0))],
        out_specs=row(D_MODEL),
        out_shape=jax.ShapeDtypeStruct((n, D_MODEL), f32),
        compiler_params=_params(("parallel",)),
        name="out_proj",
    )(a, o, x2d, mod, w_out)


def _mlp_kernel(x_ref, sh_ref, sc_ref, gate_ref, nw_ref, wu_ref, wd_ref, y_ref, *, ff_chunk):
    x = x_ref[...]
    ms = jnp.mean(x * x, axis=-1, keepdims=True)
    y = x * lax.rsqrt(ms + EPS) * nw_ref[...]
    h = (y * (1.0 + sc_ref[...]) + sh_ref[...]).astype(bf16)
    acc = jnp.zeros(x.shape, f32)
    for c0 in range(0, D_FF, ff_chunk):
        u = jnp.dot(h, wu_ref[:, c0:c0 + ff_chunk], preferred_element_type=f32)
        u = jnp.square(jnp.maximum(u, 0.0)).astype(bf16)
        acc = acc + jnp.dot(u, wd_ref[c0:c0 + ff_chunk, :], preferred_element_type=f32)
    y_ref[...] = x + gate_ref[...] * acc


def _mlp_call(x2d, mod, norm_w, w_up, w_down, tm, blocks_per_batch):
    n = x2d.shape[0]
    r = mod.shape[1]
    row = pl.BlockSpec((tm, D_MODEL), lambda i: (i, 0))
    const = lambda a: pl.BlockSpec(a.shape, lambda i: (0,) * a.ndim)
    return pl.pallas_call(
        functools.partial(_mlp_kernel, ff_chunk=1024),
        grid=(n // tm,),
        in_specs=[row, _mod_spec(r, blocks_per_batch, 3), _mod_spec(r, blocks_per_batch, 4),
                  _mod_spec(r, blocks_per_batch, 5), const(norm_w), const(w_up), const(w_down)],
        out_specs=row,
        out_shape=jax.ShapeDtypeStruct((n, D_MODEL), f32),
        compiler_params=_params(("parallel",)),
        name="mlp",
    )(x2d, mod, mod, mod, norm_w, w_up, w_down)


def _layer_weights(l, w_in, w_out, w_up, w_down, norm_mix, norm_mlp, q_norm, k_norm, w_gate, b_gate,
                   gla_norm):
    wi = w_in[l]
    c_k, c_v, c_g = W_SB, 2 * W_SB, 3 * W_SB
    c_gate = c_g + 2 * WK_GLA + WV_GLA
    w_main = jnp.concatenate(
        [wi[:, :c_k], wi[:, c_g:c_gate], wi[:, c_gate + GATE_RANK:], wi[:, c_gate:c_gate + GATE_RANK],
         jnp.zeros((D_MODEL, GATE_PAD - GATE_RANK), f32)], axis=1).astype(bf16)
    w_kvt = wi[:, c_k:c_g].T.astype(bf16)
    wg = jnp.concatenate([w_gate[l], jnp.zeros((GATE_PAD - GATE_RANK, WK_GLA), f32)], axis=0).astype(bf16)
    head = np.arange(W_SB) // D_SB
    gmat = jnp.asarray((head[:, None] == head[None, :]).astype(np.float32) / D_SB, dtype=bf16)
    return {
        "w_main": w_main, "w_kvt": w_kvt, "w_gate": wg, "gmat": gmat,
        "b_gate": b_gate[l].reshape(1, WK_GLA),
        "q_norm": jnp.tile(q_norm[l], H_SB).reshape(1, W_SB),
        "k_norm": jnp.tile(k_norm[l], H_SB),
        "norm_mix": norm_mix[l].reshape(1, D_MODEL),
        "norm_mlp": norm_mlp[l].reshape(1, D_MODEL),
        "gla_norm": gla_norm[l].reshape(1, DV_GLA),
        "w_out": w_out[l].astype(bf16), "w_up": w_up[l].astype(bf16), "w_down": w_down[l].astype(bf16),
    }


def _group_layer(x2d, mod, lw, cw, n_batch, t, past_kt, past_vt, layer, s0, tm, blocks_per_batch, chunk):
    q, kt, vt, ktb, vtb, qg, kg, vg, g, og = _in_call(x2d, mod, lw, tm, blocks_per_batch)
    if past_kt is None:
        a = _sb_call(q, ktb, vtb, ktb, vtb, cw, n_batch=n_batch, t_q=t, prompt=True, layer=layer)
    else:
        a = _sb_call(q, ktb, vtb, past_kt, past_vt, cw, n_batch=n_batch, t_q=t, prompt=False, layer=layer)
    o, s_new = _gla_call(qg, kg, vg, g, og, s0, lw["gla_norm"], n_batch, t, chunk, 2)
    x2d = _out_call(a, o, x2d, mod, lw["w_out"], tm, blocks_per_batch)
    x2d = _mlp_call(x2d, mod, lw["norm_mlp"], lw["w_up"], lw["w_down"], tm, blocks_per_batch)
    return x2d, kt, vt, s_new


def _heads_last(kt, n_batch, t):
    return kt.reshape(kt.shape[0], H_SB, D_SB, -1).transpose(0, 3, 1, 2).reshape(n_batch, t, H_SB, D_SB)


def kernel(x_prompt, x_sample, c_prompt, c_sample, cache_k, cache_v, state_gla, w_ada, b_ada, norm_mix,
           norm_mlp, w_in, q_norm, k_norm, w_gate, b_gate, gla_norm, w_out, w_up, w_down):
    bp, tp, _ = x_prompt.shape
    bs, ts, _ = x_sample.shape
    past = cache_k.shape[2]
    n_c = bp + bs
    c_rows = -(-n_c // 16) * 16
    c_all = jnp.concatenate([c_prompt, c_sample, jnp.zeros((c_rows - n_c, D_MODEL), f32)], axis=0)
    mods = _ada_call(c_all, w_ada, b_ada)

    kk = np.arange(KB)
    half = np.concatenate([(kk[:, None] > kk[None, :]), np.ones((KB, KB), bool)], axis=1)
    cw = jnp.asarray(np.concatenate([half, half], axis=0).astype(np.float32), dtype=bf16)

    cache_kt = cache_k.transpose(0, 1, 3, 4, 2).reshape(DEPTH, bs, W_SB, past)
    cache_vt = cache_v.transpose(0, 1, 3, 4, 2).reshape(DEPTH, bs, W_SB, past)

    tm_p, tm_s = 512, bs * ts
    xp = x_prompt.reshape(bp * tp, D_MODEL)
    xs = x_sample.reshape(bs * ts, D_MODEL)
    kp_l, vp_l, sp_l, ks_l, vs_l, ss_l = [], [], [], [], [], []
    for l in range(DEPTH):
        lw = _layer_weights(l, w_in, w_out, w_up, w_down, norm_mix, norm_mlp, q_norm, k_norm, w_gate,
                            b_gate, gla_norm)
        mod_p = mods[l, :bp].reshape(bp, 1, N_MOD * D_MODEL)
        mod_s = jnp.repeat(mods[l, bp:n_c], ts, axis=0).reshape(1, bs * ts, N_MOD * D_MODEL)
        s0_p = jnp.zeros((bp, H_GLA, DK_GLA, DV_GLA), f32)
        xp, kp, vp, sp = _group_layer(xp, mod_p, lw, cw, bp, tp, None, None, l, s0_p, tm_p, tp // tm_p, 128)
        xs, kn, vn, sn = _group_layer(xs, mod_s, lw, cw, bs, ts, cache_kt, cache_vt, l,
                                      state_gla[l].astype(f32), tm_s, 1, ts)
        kp_l.append(_heads_last(kp, bp, tp))
        vp_l.append(_heads_last(vp, bp, tp))
        sp_l.append(sp)
        ks_l.append(_heads_last(kn, bs, ts))
        vs_l.append(_heads_last(vn, bs, ts))
        ss_l.append(sn)
    return (xp.reshape(bp, tp, D_MODEL), xs.reshape(bs, ts, D_MODEL),
            jnp.stack(kp_l), jnp.stack(vp_l), jnp.stack(sp_l),
            jnp.stack(ks_l), jnp.stack(vs_l), jnp.stack(ss_l))
```

```python
import functools

import numpy as np
import jax
import jax.numpy as jnp
from jax import lax
from jax.experimental import pallas as pl
from jax.experimental.pallas import tpu as pltpu

D_MODEL = 1024
DEPTH = 2
H_SB = 8
D_SB = 64
W_SB = H_SB * D_SB
H_GLA = 4
DK_GLA = 64
DV_GLA = 128
WK_GLA = H_GLA * DK_GLA
WV_GLA = H_GLA * DV_GLA
GATE_RANK = 16
GATE_TAU = 16.0
D_FF = 4 * D_MODEL
N_MOD = 6
EPS = 1e-6

LANES = 128
KB = 128
SB_HEAD_ROWS = 32
SB_LOOP_GROUP = 2
N_PAIR = H_SB // 2
GATE_PAD = LANES
W_MAIN = W_SB + 2 * WK_GLA + 2 * WV_GLA + GATE_PAD
VMEM_LIMIT = 56 * 1024 * 1024
SURVIVAL_ZERO = 104.0
MASKED = -1e30

f32 = jnp.float32
bf16 = jnp.bfloat16
NT = (((1,), (1,)), ((), ()))
TN = (((0,), (0,)), ((), ()))


def _params(sem):
    return pltpu.CompilerParams(dimension_semantics=sem, vmem_limit_bytes=VMEM_LIMIT)


def _log_sigmoid(x):
    return jnp.minimum(x, 0.0) - jnp.log1p(jnp.exp(-jnp.abs(x)))


def _ada_kernel(c_ref, w_ref, b_ref, o_ref):
    c = c_ref[...]
    s = (c * jax.nn.sigmoid(c)).astype(bf16)
    o_ref[...] = jnp.dot(s, w_ref[...].astype(bf16), preferred_element_type=f32) + b_ref[...]


def _ada_call(c_all, w_ada, b_ada):
    rows = c_all.shape[0]
    return pl.pallas_call(
        _ada_kernel,
        grid=(DEPTH, N_MOD),
        in_specs=[
            pl.BlockSpec((rows, D_MODEL), lambda l, p: (0, 0)),
            pl.BlockSpec((None, D_MODEL, D_MODEL), lambda l, p: (l, 0, p)),
            pl.BlockSpec((None, 1, D_MODEL), lambda l, p: (l, 0, p)),
        ],
        out_specs=pl.BlockSpec((None, rows, D_MODEL), lambda l, p: (l, 0, p)),
        out_shape=jax.ShapeDtypeStruct((DEPTH, rows, N_MOD * D_MODEL), f32),
        compiler_params=_params(("parallel", "parallel")),
        name="ada_mod",
    )(c_all, w_ada, b_ada.reshape(DEPTH, 1, N_MOD * D_MODEL))


def _mod_spec(rows_per_block, blocks_per_batch, piece):
    return pl.BlockSpec((None, rows_per_block, D_MODEL),
                        lambda i: (i // blocks_per_batch, 0, piece))


def _in_kernel(x_ref, sh_ref, sc_ref, nw_ref, w_ref, wkv_ref, gm_ref, qn_ref, kn_ref, wg_ref, bg_ref,
               q_ref, kt_ref, vt_ref, ktb_ref, vtb_ref, qg_ref, kg_ref, vg_ref, g_ref, og_ref):
    x = x_ref[...]
    ms = jnp.mean(x * x, axis=-1, keepdims=True)
    y = x * lax.rsqrt(ms + EPS) * nw_ref[...]
    h = (y * (1.0 + sc_ref[...]) + sh_ref[...]).astype(bf16)

    def proj(lo, hi):
        return jnp.dot(h, w_ref[:, lo:hi], preferred_element_type=f32)

    kt = lax.dot_general(wkv_ref[:W_SB, :], h, NT, preferred_element_type=f32)
    mk = jnp.dot(gm_ref[...], (kt * kt).astype(bf16), preferred_element_type=f32)
    kt = kt * lax.rsqrt(mk + EPS) * kn_ref[...]
    kt_ref[...] = kt
    ktb_ref[...] = kt.astype(bf16)
    vt = lax.dot_general(wkv_ref[W_SB:, :], h, NT, preferred_element_type=f32)
    vt_ref[...] = vt
    vtb_ref[...] = vt.astype(bf16)

    o = 0
    q = proj(o, o + W_SB)
    mq = jnp.dot((q * q).astype(bf16), gm_ref[...], preferred_element_type=f32)
    q_ref[...] = (q * lax.rsqrt(mq + EPS) * qn_ref[...] * (D_SB ** -0.5)).astype(bf16)
    o += W_SB
    qg_ref[...] = proj(o, o + WK_GLA) * (DK_GLA ** -0.5)
    o += WK_GLA
    kg_ref[...] = proj(o, o + WK_GLA)
    o += WK_GLA
    vg_ref[...] = proj(o, o + WV_GLA)
    o += WV_GLA
    og_ref[...] = proj(o, o + WV_GLA)
    o += WV_GLA
    gr = proj(o, o + GATE_PAD).astype(bf16)
    xg = jnp.dot(gr, wg_ref[...], preferred_element_type=f32) + bg_ref[...]
    g_ref[...] = _log_sigmoid(xg) * (1.0 / GATE_TAU)


def _in_call(x2d, mod, lw, tm, blocks_per_batch):
    n = x2d.shape[0]
    r = mod.shape[1]
    n_batch = n // (tm * blocks_per_batch)
    t = tm * blocks_per_batch
    row = lambda w: pl.BlockSpec((tm, w), lambda i: (i, 0))
    col = pl.BlockSpec((None, W_SB, tm), lambda i: (i // blocks_per_batch, 0, i % blocks_per_batch))
    const = lambda a: pl.BlockSpec(a.shape, lambda i: (0,) * a.ndim)
    kn_t = jnp.broadcast_to(lw["k_norm"].reshape(W_SB, 1), (W_SB, tm))
    rows_out = [(WK_GLA, f32), (WK_GLA, f32), (WV_GLA, f32), (WK_GLA, f32), (WV_GLA, f32)]
    kv_shape = lambda dt: jax.ShapeDtypeStruct((n_batch, W_SB, t), dt)
    return pl.pallas_call(
        _in_kernel,
        grid=(n // tm,),
        in_specs=[row(D_MODEL), _mod_spec(r, blocks_per_batch, 0), _mod_spec(r, blocks_per_batch, 1),
                  const(lw["norm_mix"]), const(lw["w_main"]), const(lw["w_kvt"]), const(lw["gmat"]),
                  const(lw["q_norm"]), const(kn_t), const(lw["w_gate"]), const(lw["b_gate"])],
        out_specs=[row(W_SB), col, col, col, col] + [row(w) for w, _ in rows_out],
        out_shape=[jax.ShapeDtypeStruct((n, W_SB), bf16), kv_shape(f32), kv_shape(f32), kv_shape(bf16),
                   kv_shape(bf16)] + [jax.ShapeDtypeStruct((n, w), dt) for w, dt in rows_out],
        compiler_params=_params(("parallel",)),
        name="in_proj",
    )(x2d, mod, mod, lw["norm_mix"], lw["w_main"], lw["w_kvt"], lw["gmat"], lw["q_norm"], kn_t,
      lw["w_gate"], lw["b_gate"])


def _pair_diag(a):
    z = jnp.zeros((D_SB, a.shape[1]), a.dtype)
    top = jnp.concatenate([a[:D_SB], z], axis=0)
    bot = jnp.concatenate([z, a[D_SB:]], axis=0)
    return jnp.concatenate([top, bot], axis=1)


def _sb_scores(q_pair, k_tiles, biases, n_rows):
    tiles = []
    for kt, bias, r in zip(k_tiles, biases, n_rows):
        z2 = jnp.dot(q_pair[:r], _pair_diag(kt), preferred_element_type=f32)
        for hh in range(2):
            z = z2[:, hh * KB:(hh + 1) * KB] + (bias if bias.ndim == 0 else bias[:r])
            sp = jnp.maximum(z, 0.0) + jnp.log(1.0 + jnp.exp(-jnp.abs(z)))
            hi = sp.astype(bf16)
            lo = (sp - hi.astype(f32)).astype(bf16)
            tiles.append((z - sp, jnp.concatenate([hi, lo], axis=1)))
    return tiles


def _add_rows(a, r, upd):
    if a is None:
        return upd
    return a + upd if r == a.shape[0] else jnp.concatenate([a[:r] + upd, a[r:]], axis=0)


def _sb_apply(tiles, cs, v_tiles, carries, acc, n_rows):
    new_carries = []
    ws = [[None, None] for _ in v_tiles]
    for hh in range(2):
        c = carries[hh]
        for bi, r in enumerate(n_rows):
            ti = bi * 2 + hh
            between = cs[ti][:, :KB] if c is None else c[:r] + cs[ti][:, :KB]
            ws[bi][hh] = jnp.exp(tiles[ti][0] - between).astype(bf16)
            c = _add_rows(c, r, cs[ti][:, KB:])
        new_carries.append(c)
    for bi, (vt, r) in enumerate(zip(v_tiles, n_rows)):
        w_pair = jnp.concatenate(ws[bi], axis=1)
        acc = _add_rows(acc, r, lax.dot_general(w_pair, _pair_diag(vt), NT, preferred_element_type=f32))
    return acc, new_carries


def _sb_kernel(q_ref, kd_ref, vd_ref, kp_ref, vp_ref, cw_ref, o_ref, acc_ref, carry_ref,
               *, tq, seg, n_past):
    b = pl.program_id(0)
    qi = pl.program_id(1)
    n_prev = qi if n_past is None else n_past
    head_rows = min(SB_HEAD_ROWS, tq)
    row = lax.broadcasted_iota(jnp.int32, (tq, KB), 0)
    col = lax.broadcasted_iota(jnp.int32, (tq, KB), 1)
    seg_id = b % (KB // seg)
    seg_shift = seg.bit_length() - 1
    own = jnp.logical_and(lax.shift_right_logical(col, seg_shift) == seg_id,
                          jnp.bitwise_and(col, seg - 1) < row)
    bias0 = jnp.where(own, 0.0, MASKED)
    cw = cw_ref[...]
    pair_rows = [slice(p * KB, (p + 1) * KB) for p in range(N_PAIR)]

    def past_block(j, skip=None):
        jc = pl.multiple_of(jnp.maximum(j, 0) * KB, KB)
        kt = kp_ref[:, pl.ds(jc, KB)].astype(bf16)
        vt = vp_ref[:, pl.ds(jc, KB)].astype(bf16)
        absent = j < 0
        if skip is None:
            return kt, vt, jnp.where(absent, MASKED, 0.0)
        return kt, vt, jnp.where(jnp.logical_or(absent, skip), MASKED, 0.0)

    def run_group(blocks, n_rows, carries_of, acc_of):
        tiles = [_sb_scores(q_ref[:, rows], [kt[rows] for kt, _, _ in blocks], [bias for _, _, bias in blocks],
                            n_rows) for rows in pair_rows]
        cs_all = jnp.dot(jnp.concatenate([t[1] for pt in tiles for t in pt], axis=0), cw,
                         preferred_element_type=f32)
        alive, start = None, 0
        for p, rows in enumerate(pair_rows):
            cs = []
            for r in n_rows:
                for _ in range(2):
                    cs.append(cs_all[start:start + r])
                    start += r
            acc, cr = _sb_apply(tiles[p], cs, [vt[rows] for _, vt, _ in blocks], carries_of(p), acc_of(p), n_rows)
            acc_ref[p] = acc
            carry_ref[2 * p] = cr[0]
            carry_ref[2 * p + 1] = cr[1]
            m = jnp.minimum(cr[0], cr[1])
            alive = m if alive is None else jnp.minimum(alive, m)
        return (jnp.min(alive) < SURVIVAL_ZERO).astype(jnp.int32)

    j0 = n_prev - 1
    first = [(kd_ref[...], vd_ref[...], bias0), past_block(j0), past_block(j0 - 1)]
    alive0 = run_group(first, [tq, tq, head_rows], lambda p: (None, None), lambda p: None)

    def cond(st):
        j, alive = st
        return jnp.logical_and(j >= 0, alive > 0)

    def body(st):
        j, _ = st
        blocks = []
        for d in range(SB_LOOP_GROUP):
            seen = None if head_rows == tq else jnp.logical_and(j - d == j0 - 1, row < head_rows)
            blocks.append(past_block(j - d, seen))
        alive = run_group(blocks, [tq] * SB_LOOP_GROUP,
                          lambda p: (carry_ref[2 * p], carry_ref[2 * p + 1]), lambda p: acc_ref[p])
        return j - SB_LOOP_GROUP, alive

    lax.while_loop(cond, body, (j0 - (2 if head_rows == tq else 1), alive0))
    o_ref[...] = jnp.concatenate([acc_ref[p] for p in range(N_PAIR)], axis=1).astype(bf16)


def _sb_call(q, kd, vd, kp, vp, cw, *, n_batch, t_q, prompt, layer):
    tq = min(KB, t_q)
    nq = t_q // tq
    q_spec = pl.BlockSpec((tq, W_SB), lambda b, i: (b * nq + i, 0))
    if prompt:
        d_spec = pl.BlockSpec((None, W_SB, KB), lambda b, i: (b, 0, i))
        p_spec = pl.BlockSpec((None, W_SB, kp.shape[-1]), lambda b, i: (b, 0, 0))
        seg, n_past = KB, None
    else:
        d_spec = pl.BlockSpec((None, W_SB, KB), lambda b, i: (0, 0, (b * t_q) // KB))
        p_spec = pl.BlockSpec((None, None, W_SB, kp.shape[-1]), lambda b, i: (layer, b, 0, 0))
        seg, n_past = t_q, kp.shape[-1] // KB
    return pl.pallas_call(
        functools.partial(_sb_kernel, tq=tq, seg=seg, n_past=n_past),
        grid=(n_batch, nq),
        in_specs=[q_spec, d_spec, d_spec, p_spec, p_spec, pl.BlockSpec(cw.shape, lambda b, i: (0, 0))],
        out_specs=q_spec,
        out_shape=jax.ShapeDtypeStruct((n_batch * t_q, W_SB), bf16),
        scratch_shapes=[pltpu.VMEM((N_PAIR, tq, LANES), f32), pltpu.VMEM((H_SB, tq, LANES), f32)],
        compiler_params=_params(("parallel", "arbitrary")),
        name="stick_breaking",
    )(q, kd, vd, kp, vp, cw)


GLA_PAIRS = H_GLA // 2
GLA_SAFE_DECAY = 60.0


def _gla_tables(chunk):
    idx = np.arange(chunk)
    t, s = idx[:, None], idx[None, :]
    tri = jnp.asarray((s <= t).astype(np.float32), dtype=bf16)
    level = np.where(t > s, np.floor(np.log2(np.maximum(t ^ s, 1))).astype(np.int32), -1)
    level = np.where(t < s, -2, level).astype(np.int32)
    return tri, jnp.asarray(np.concatenate([level, level], axis=1))


def _level_reference(b, h):
    L = b.shape[0]
    if h >= 4:
        return jnp.concatenate(
            [jnp.broadcast_to(b[c0 + h - 1:c0 + h, :], (2 * h, LANES)) for c0 in range(0, L, 2 * h)], axis=0)
    b3 = b.reshape(L // 8, 8, LANES)
    pos = jnp.bitwise_and(lax.broadcasted_iota(jnp.int32, b3.shape, 1), 2 * h - 1)
    up = lambda n: pltpu.roll(b3, n, 1)
    if h == 1:
        ref = jnp.where(pos == 0, b3, up(1))
    else:
        ref = jnp.where(pos == 0, up(7), jnp.where(pos == 1, b3, jnp.where(pos == 2, up(1), up(2))))
    return ref.reshape(L, LANES)


def _two_heads(a, first_head):
    return jnp.concatenate([jnp.where(first_head, a, 0.0).astype(bf16),
                            jnp.where(first_head, 0.0, a).astype(bf16)], axis=0)


def _gla_att_single(q_in, k, b, lev2, first_head):
    a = lax.dot_general(q_in, _two_heads(k * jnp.exp(-b), first_head), NT, preferred_element_type=f32)
    return jnp.where(lev2 >= -1, a, 0.0)


def _gla_att_levels(q, k, b, lev2, first_head, n_lev):
    a = lax.dot_general(q.astype(bf16), _two_heads(k, first_head), NT, preferred_element_type=f32)
    att = jnp.where(lev2 == -1, a, 0.0)
    for lev in range(n_lev):
        d = b - _level_reference(b, 1 << lev)
        e = jnp.exp(-jnp.abs(d))
        is_query = d < 0.0
        q_l = (q * jnp.where(is_query, e, 1.0)).astype(bf16)
        k_l = k * jnp.where(is_query, 1.0, e)
        a = lax.dot_general(q_l, _two_heads(k_l, first_head), NT, preferred_element_type=f32)
        att = jnp.where(lev2 == lev, a, att)
    return att


def _gla_kernel(q_ref, k_ref, v_ref, g_ref, og_ref, s0_ref, tri_ref, lev_ref, gn_ref,
                o_ref, sout_ref, s_ref, att_ref, *, chunk, bb):
    c = pl.program_id(1)
    L = chunk
    n_lev = L.bit_length() - 1
    chains = [(bi, p) for bi in range(bb) for p in range(GLA_PAIRS)]
    zeros_s = jnp.zeros((DK_GLA, DV_GLA), f32)

    @pl.when(c == 0)
    def _():
        for bi, p in chains:
            s0 = s0_ref[bi, p]
            s_ref[bi, p] = jnp.concatenate(
                [jnp.concatenate([s0[:DK_GLA], zeros_s], axis=1),
                 jnp.concatenate([zeros_s, s0[DK_GLA:]], axis=1)], axis=0)

    lane = lax.broadcasted_iota(jnp.int32, (L, LANES), 1)
    first_head = lane < DK_GLA
    lev2 = lev_ref[...]
    tri = tri_ref[...]

    pre = []
    floor = None
    for bi, p in chains:
        lanes = slice(p * LANES, (p + 1) * LANES)
        q = q_ref[bi, :, lanes]
        k = k_ref[bi, :, lanes]
        g = g_ref[bi, :, lanes]
        g_hi = g.astype(bf16)
        g_lo = (g - g_hi.astype(f32)).astype(bf16)
        g2 = jnp.concatenate([g_hi, g_lo], axis=1)
        bs = jnp.dot(tri, g2, preferred_element_type=f32)
        b = bs[:, :LANES] + bs[:, LANES:]
        b_last = b[L - 1:L, :]
        q_in = (q * jnp.exp(b)).astype(bf16)
        k_end = (k * jnp.exp(b_last - b)).astype(bf16)
        pre.append((q, k, g2, b, q_in, k_end))
        floor = b_last if floor is None else jnp.minimum(floor, b_last)
    single = jnp.min(floor) >= -GLA_SAFE_DECAY

    @pl.when(single)
    def _():
        for ci, (q, k, g2, b, q_in, k_end) in enumerate(pre):
            att_ref[ci] = _gla_att_single(q_in, k, b, lev2, first_head)

    @pl.when(jnp.logical_not(single))
    def _():
        for ci, (q, k, g2, b, q_in, k_end) in enumerate(pre):
            att_ref[ci] = _gla_att_levels(q, k, b, lev2, first_head, n_lev)

    rowi = lax.broadcasted_iota(jnp.int32, (2 * DK_GLA, 2 * DV_GLA), 0)
    coli = lax.broadcasted_iota(jnp.int32, (2 * DK_GLA, 2 * DV_GLA), 1)
    own_block = (rowi < DK_GLA) == (coli < DV_GLA)
    ones = jnp.ones((L, LANES), bf16)
    zeros_v = jnp.zeros((L, DV_GLA), bf16)
    gn = gn_ref[...]
    for ci, (bi, p) in enumerate(chains):
        q, k, g2, b, q_in, k_end = pre[ci]
        cols = slice(p * 2 * DV_GLA, (p + 1) * 2 * DV_GLA)
        v = v_ref[bi, :, cols].astype(bf16)
        v_diag = jnp.concatenate([jnp.concatenate([v[:, :DV_GLA], zeros_v], axis=1),
                                  jnp.concatenate([zeros_v, v[:, DV_GLA:]], axis=1)], axis=0)
        s_old = s_ref[bi, p]
        o2 = (jnp.dot(att_ref[ci].astype(bf16), v_diag, preferred_element_type=f32)
              + jnp.dot(q_in, s_old.astype(bf16), preferred_element_type=f32))
        tot = lax.dot_general(g2, ones, TN, preferred_element_type=f32)
        decay = jnp.exp(tot[:LANES] + tot[LANES:])
        upd = lax.dot_general(k_end, v, TN, preferred_element_type=f32)
        s_ref[bi, p] = jnp.concatenate([decay, decay], axis=1) * s_old + jnp.where(own_block, upd, 0.0)
        og = og_ref[bi, :, cols]
        outs = []
        for hh in range(2):
            o = o2[:, hh * DV_GLA:(hh + 1) * DV_GLA]
            ms = jnp.mean(o * o, axis=-1, keepdims=True)
            gate = og[:, hh * DV_GLA:(hh + 1) * DV_GLA]
            outs.append(o * lax.rsqrt(ms + EPS) * gn * (gate * jax.nn.sigmoid(gate)))
        o_ref[bi, :, cols] = jnp.concatenate(outs, axis=1).astype(bf16)

    @pl.when(c == pl.num_programs(1) - 1)
    def _():
        for bi, p in chains:
            s = s_ref[bi, p]
            sout_ref[bi, p] = jnp.concatenate([s[:DK_GLA, :DV_GLA], s[DK_GLA:, DV_GLA:]], axis=0)


def _gla_call(qg, kg, vg, g, og, s0, gla_norm, n_batch, t, chunk, bb):
    nc = t // chunk
    tri, lev2 = _gla_tables(chunk)
    kspec = pl.BlockSpec((bb, chunk, WK_GLA), lambda i, c: (i, c, 0))
    vspec = pl.BlockSpec((bb, chunk, WV_GLA), lambda i, c: (i, c, 0))
    sspec = pl.BlockSpec((bb, GLA_PAIRS, 2 * DK_GLA, DV_GLA), lambda i, c: (i, 0, 0, 0))
    const = lambda a: pl.BlockSpec(a.shape, lambda i, c: (0,) * a.ndim)
    s0 = s0.reshape(n_batch, GLA_PAIRS, 2 * DK_GLA, DV_GLA)
    k3 = lambda a: a.reshape(n_batch, t, WK_GLA)
    v3 = lambda a: a.reshape(n_batch, t, WV_GLA)
    o, s_out = pl.pallas_call(
        functools.partial(_gla_kernel, chunk=chunk, bb=bb),
        grid=(n_batch // bb, nc),
        in_specs=[kspec, kspec, vspec, kspec, vspec, sspec, const(tri), const(lev2), const(gla_norm)],
        out_specs=[vspec, sspec],
        out_shape=[jax.ShapeDtypeStruct((n_batch, t, WV_GLA), bf16),
                   jax.ShapeDtypeStruct(s0.shape, f32)],
        scratch_shapes=[pltpu.VMEM((bb, GLA_PAIRS, 2 * DK_GLA, 2 * DV_GLA), f32),
                        pltpu.VMEM((bb * GLA_PAIRS, chunk, 2 * chunk), f32)],
        compiler_params=_params(("parallel", "arbitrary")),
        name="gla",
    )(k3(qg), k3(kg), v3(vg), k3(g), v3(og), s0, tri, lev2, gla_norm)
    return o.reshape(n_batch * t, WV_GLA), s_out.reshape(n_batch, H_GLA, DK_GLA, DV_GLA)


def _mlp_kernel(a_ref, o_ref, x_ref, mix_gate_ref, sh_ref, sc_ref, gate_ref, nw_ref, wo_ref, wu_ref, wd_ref,
                y_ref, *, ff_chunk):
    m = (jnp.dot(a_ref[...], wo_ref[:W_SB, :], preferred_element_type=f32)
         + jnp.dot(o_ref[...], wo_ref[W_SB:, :], preferred_element_type=f32))
    x = x_ref[...] + mix_gate_ref[...] * m
    ms = jnp.mean(x * x, axis=-1, keepdims=True)
    y = x * lax.rsqrt(ms + EPS) * nw_ref[...]
    h = (y * (1.0 + sc_ref[...]) + sh_ref[...]).astype(bf16)
    acc = jnp.zeros(x.shape, f32)
    for c0 in range(0, D_FF, ff_chunk):
        u = jnp.dot(h, wu_ref[:, c0:c0 + ff_chunk], preferred_element_type=f32)
        u = jnp.square(jnp.maximum(u, 0.0)).astype(bf16)
        acc = acc + jnp.dot(u, wd_ref[c0:c0 + ff_chunk, :], preferred_element_type=f32)
    y_ref[...] = x + gate_ref[...] * acc


def _mlp_call(a, o, x2d, mod, lw, tm, blocks_per_batch):
    n = x2d.shape[0]
    r = mod.shape[1]
    row = lambda w: pl.BlockSpec((tm, w), lambda i: (i, 0))
    const = lambda a: pl.BlockSpec(a.shape, lambda i: (0,) * a.ndim)
    piece = lambda k: _mod_spec(r, blocks_per_batch, k)
    return pl.pallas_call(
        functools.partial(_mlp_kernel, ff_chunk=1024),
        grid=(n // tm,),
        in_specs=[row(W_SB), row(WV_GLA), row(D_MODEL), piece(2), piece(3), piece(4), piece(5),
                  const(lw["norm_mlp"]), const(lw["w_out"]), const(lw["w_up"]), const(lw["w_down"])],
        out_specs=row(D_MODEL),
        out_shape=jax.ShapeDtypeStruct((n, D_MODEL), f32),
        compiler_params=_params(("parallel",)),
        name="out_mlp",
    )(a, o, x2d, mod, mod, mod, mod, lw["norm_mlp"], lw["w_out"], lw["w_up"], lw["w_down"])


def _layer_weights(l, w_in, w_out, w_up, w_down, norm_mix, norm_mlp, q_norm, k_norm, w_gate, b_gate,
                   gla_norm):
    wi = w_in[l]
    c_k, c_v, c_g = W_SB, 2 * W_SB, 3 * W_SB
    c_gate = c_g + 2 * WK_GLA + WV_GLA
    w_main = jnp.concatenate(
        [wi[:, :c_k], wi[:, c_g:c_gate], wi[:, c_gate + GATE_RANK:], wi[:, c_gate:c_gate + GATE_RANK],
         jnp.zeros((D_MODEL, GATE_PAD - GATE_RANK), f32)], axis=1).astype(bf16)
    w_kvt = wi[:, c_k:c_g].T.astype(bf16)
    wg = jnp.concatenate([w_gate[l], jnp.zeros((GATE_PAD - GATE_RANK, WK_GLA), f32)], axis=0).astype(bf16)
    head = np.arange(W_SB) // D_SB
    gmat = jnp.asarray((head[:, None] == head[None, :]).astype(np.float32) / D_SB, dtype=bf16)
    return {
        "w_main": w_main, "w_kvt": w_kvt, "w_gate": wg, "gmat": gmat,
        "b_gate": b_gate[l].reshape(1, WK_GLA),
        "q_norm": jnp.tile(q_norm[l], H_SB).reshape(1, W_SB),
        "k_norm": jnp.tile(k_norm[l], H_SB),
        "norm_mix": norm_mix[l].reshape(1, D_MODEL),
        "norm_mlp": norm_mlp[l].reshape(1, D_MODEL),
        "gla_norm": gla_norm[l].reshape(1, DV_GLA),
        "w_out": w_out[l].astype(bf16), "w_up": w_up[l].astype(bf16), "w_down": w_down[l].astype(bf16),
    }


def _group_layer(x2d, mod, lw, cw, n_batch, t, past_kt, past_vt, layer, s0, tm, blocks_per_batch, chunk,
                 gla_bb):
    q, kt, vt, ktb, vtb, qg, kg, vg, g, og = _in_call(x2d, mod, lw, tm, blocks_per_batch)
    if past_kt is None:
        a = _sb_call(q, ktb, vtb, ktb, vtb, cw, n_batch=n_batch, t_q=t, prompt=True, layer=layer)
    else:
        a = _sb_call(q, ktb, vtb, past_kt, past_vt, cw, n_batch=n_batch, t_q=t, prompt=False, layer=layer)
    o, s_new = _gla_call(qg, kg, vg, g, og, s0, lw["gla_norm"], n_batch, t, chunk, gla_bb)
    x2d = _mlp_call(a, o, x2d, mod, lw, tm, blocks_per_batch)
    return x2d, kt, vt, s_new


def _heads_last(kt, n_batch, t):
    return kt.reshape(kt.shape[0], H_SB, D_SB, -1).transpose(0, 3, 1, 2).reshape(n_batch, t, H_SB, D_SB)


def kernel(x_prompt, x_sample, c_prompt, c_sample, cache_k, cache_v, state_gla, w_ada, b_ada, norm_mix,
           norm_mlp, w_in, q_norm, k_norm, w_gate, b_gate, gla_norm, w_out, w_up, w_down):
    bp, tp, _ = x_prompt.shape
    bs, ts, _ = x_sample.shape
    past = cache_k.shape[2]
    n_c = bp + bs
    c_rows = -(-n_c // 16) * 16
    c_all = jnp.concatenate([c_prompt, c_sample, jnp.zeros((c_rows - n_c, D_MODEL), f32)], axis=0)
    mods = _ada_call(c_all, w_ada, b_ada)

    kk = np.arange(KB)
    half = np.concatenate([(kk[:, None] > kk[None, :]), np.ones((KB, KB), bool)], axis=1)
    cw = jnp.asarray(np.concatenate([half, half], axis=0).astype(np.float32), dtype=bf16)

    cache_kt = cache_k.transpose(0, 1, 3, 4, 2).reshape(DEPTH, bs, W_SB, past)
    cache_vt = cache_v.transpose(0, 1, 3, 4, 2).reshape(DEPTH, bs, W_SB, past)

    tm_p, tm_s = 512, bs * ts
    xp = x_prompt.reshape(bp * tp, D_MODEL)
    xs = x_sample.reshape(bs * ts, D_MODEL)
    kp_l, vp_l, sp_l, ks_l, vs_l, ss_l = [], [], [], [], [], []
    for l in range(DEPTH):
        lw = _layer_weights(l, w_in, w_out, w_up, w_down, norm_mix, norm_mlp, q_norm, k_norm, w_gate,
                            b_gate, gla_norm)
        mod_p = mods[l, :bp].reshape(bp, 1, N_MOD * D_MODEL)
        mod_s = jnp.repeat(mods[l, bp:n_c], ts, axis=0).reshape(1, bs * ts, N_MOD * D_MODEL)
        s0_p = jnp.zeros((bp, H_GLA, DK_GLA, DV_GLA), f32)
        xp, kp, vp, sp = _group_layer(xp, mod_p, lw, cw, bp, tp, None, None, l, s0_p, tm_p, tp // tm_p, 128, bp)
        xs, kn, vn, sn = _group_layer(xs, mod_s, lw, cw, bs, ts, cache_kt, cache_vt, l,
                                      state_gla[l].astype(f32), tm_s, 1, ts, 4)
        kp_l.append(_heads_last(kp, bp, tp))
        vp_l.append(_heads_last(vp, bp, tp))
        sp_l.append(sp)
        ks_l.append(_heads_last(kn, bs, ts))
        vs_l.append(_heads_last(vn, bs, ts))
        ss_l.append(sn)
    return (xp.reshape(bp, tp, D_MODEL), xs.reshape(bs, ts, D_MODEL),
            jnp.stack(kp_l), jnp.stack(vp_l), jnp.stack(sp_l),
            jnp.stack(ks_l), jnp.stack(vs_l), jnp.stack(ss_l))
```

```python
import functools

import numpy as np
import jax
import jax.numpy as jnp
from jax import lax
from jax.experimental import pallas as pl
from jax.experimental.pallas import tpu as pltpu

D_MODEL = 1024
DEPTH = 2
H_SB = 8
D_SB = 64
W_SB = H_SB * D_SB
H_GLA = 4
DK_GLA = 64
DV_GLA = 128
WK_GLA = H_GLA * DK_GLA
WV_GLA = H_GLA * DV_GLA
GATE_RANK = 16
GATE_TAU = 16.0
D_FF = 4 * D_MODEL
N_MOD = 6
EPS = 1e-6

LANES = 128
KB = 128
SB_FIRST_PASS = 3
SB_LOOP_GROUP = 1
N_PAIR = H_SB // 2
GATE_PAD = LANES
W_MAIN = W_SB + 2 * WK_GLA + 2 * WV_GLA + GATE_PAD
VMEM_LIMIT = 56 * 1024 * 1024
SURVIVAL_ZERO = 104.0
MASKED = -1e30

f32 = jnp.float32
bf16 = jnp.bfloat16
NT = (((1,), (1,)), ((), ()))
TN = (((0,), (0,)), ((), ()))


def _params(sem):
    return pltpu.CompilerParams(dimension_semantics=sem, vmem_limit_bytes=VMEM_LIMIT)


def _log_sigmoid(x):
    return jnp.minimum(x, 0.0) - jnp.log1p(jnp.exp(-jnp.abs(x)))


def _ada_kernel(c_ref, w_ref, b_ref, o_ref):
    c = c_ref[...]
    s = (c * jax.nn.sigmoid(c)).astype(bf16)
    o_ref[...] = jnp.dot(s, w_ref[...].astype(bf16), preferred_element_type=f32) + b_ref[...]


def _ada_call(c_all, w_ada, b_ada):
    rows = c_all.shape[0]
    return pl.pallas_call(
        _ada_kernel,
        grid=(DEPTH, N_MOD),
        in_specs=[
            pl.BlockSpec((rows, D_MODEL), lambda l, p: (0, 0)),
            pl.BlockSpec((None, D_MODEL, D_MODEL), lambda l, p: (l, 0, p)),
            pl.BlockSpec((None, 1, D_MODEL), lambda l, p: (l, 0, p)),
        ],
        out_specs=pl.BlockSpec((None, rows, D_MODEL), lambda l, p: (l, 0, p)),
        out_shape=jax.ShapeDtypeStruct((DEPTH, rows, N_MOD * D_MODEL), f32),
        compiler_params=_params(("parallel", "parallel")),
        name="ada_mod",
    )(c_all, w_ada, b_ada.reshape(DEPTH, 1, N_MOD * D_MODEL))


def _mod_spec(rows_per_block, blocks_per_batch, piece):
    return pl.BlockSpec((None, rows_per_block, D_MODEL),
                        lambda i: (i // blocks_per_batch, 0, piece))


def _in_kernel(*refs, n_prev):
    (x_ref, sh_ref, sc_ref, nw_ref, w_ref, wkv_ref, gm_ref, qn_ref, kn_ref, wg_ref, bg_ref) = refs[:11]
    prev_refs = refs[11:11 + 2 * min(n_prev, 1)]
    (q_ref, kt_ref, vt_ref, ktb_ref, vtb_ref, qg_ref, kg_ref, vg_ref, g_ref, og_ref) = refs[len(refs) - 10:]
    for prev_ref, out_ref in zip(prev_refs, (kt_ref, vt_ref)):
        out_ref[:n_prev] = prev_ref[...]
    x = x_ref[...]
    ms = jnp.mean(x * x, axis=-1, keepdims=True)
    y = x * lax.rsqrt(ms + EPS) * nw_ref[...]
    h = (y * (1.0 + sc_ref[...]) + sh_ref[...]).astype(bf16)

    def proj(lo, hi):
        return jnp.dot(h, w_ref[:, lo:hi], preferred_element_type=f32)

    kt = lax.dot_general(wkv_ref[:W_SB, :], h, NT, preferred_element_type=f32)
    mk = jnp.dot(gm_ref[...], (kt * kt).astype(bf16), preferred_element_type=f32)
    kt = kt * lax.rsqrt(mk + EPS) * kn_ref[...]
    kt_ref[n_prev] = kt
    ktb_ref[...] = kt.astype(bf16)
    vt = lax.dot_general(wkv_ref[W_SB:, :], h, NT, preferred_element_type=f32)
    vt_ref[n_prev] = vt
    vtb_ref[...] = vt.astype(bf16)

    o = 0
    q = proj(o, o + W_SB)
    mq = jnp.dot((q * q).astype(bf16), gm_ref[...], preferred_element_type=f32)
    q_ref[...] = (q * lax.rsqrt(mq + EPS) * qn_ref[...] * (D_SB ** -0.5)).astype(bf16)
    o += W_SB
    qg_ref[...] = proj(o, o + WK_GLA) * (DK_GLA ** -0.5)
    o += WK_GLA
    kg_ref[...] = proj(o, o + WK_GLA)
    o += WK_GLA
    vg_ref[...] = proj(o, o + WV_GLA)
    o += WV_GLA
    og_ref[...] = proj(o, o + WV_GLA)
    o += WV_GLA
    gr = proj(o, o + GATE_PAD).astype(bf16)
    xg = jnp.dot(gr, wg_ref[...], preferred_element_type=f32) + bg_ref[...]
    g_ref[...] = _log_sigmoid(xg) * (1.0 / GATE_TAU)


def _in_call(x2d, mod, lw, tm, blocks_per_batch, prev_kv):
    n = x2d.shape[0]
    r = mod.shape[1]
    n_batch = n // (tm * blocks_per_batch)
    t = tm * blocks_per_batch
    n_prev = prev_kv[0].shape[0] if prev_kv else 0
    row = lambda w: pl.BlockSpec((tm, w), lambda i: (i, 0))
    col_map = lambda i: (i // blocks_per_batch, 0, i % blocks_per_batch)
    col = pl.BlockSpec((None, W_SB, tm), col_map)
    stack = lambda depth: pl.BlockSpec((depth, None, W_SB, tm), lambda i: (0,) + col_map(i))
    const = lambda a: pl.BlockSpec(a.shape, lambda i: (0,) * a.ndim)
    kn_t = jnp.broadcast_to(lw["k_norm"].reshape(W_SB, 1), (W_SB, tm))
    rows_out = [(WK_GLA, f32), (WK_GLA, f32), (WV_GLA, f32), (WK_GLA, f32), (WV_GLA, f32)]
    kv_stack = jax.ShapeDtypeStruct((n_prev + 1, n_batch, W_SB, t), f32)
    kv_b = jax.ShapeDtypeStruct((n_batch, W_SB, t), bf16)
    return pl.pallas_call(
        functools.partial(_in_kernel, n_prev=n_prev),
        grid=(n // tm,),
        in_specs=[row(D_MODEL), _mod_spec(r, blocks_per_batch, 0), _mod_spec(r, blocks_per_batch, 1),
                  const(lw["norm_mix"]), const(lw["w_main"]), const(lw["w_kvt"]), const(lw["gmat"]),
                  const(lw["q_norm"]), const(kn_t), const(lw["w_gate"]), const(lw["b_gate"])]
        + [stack(n_prev)] * len(prev_kv),
        out_specs=[row(W_SB), stack(n_prev + 1), stack(n_prev + 1), col, col] + [row(w) for w, _ in rows_out],
        out_shape=[jax.ShapeDtypeStruct((n, W_SB), bf16), kv_stack, kv_stack, kv_b, kv_b]
        + [jax.ShapeDtypeStruct((n, w), dt) for w, dt in rows_out],
        compiler_params=_params(("parallel",)),
        name="in_proj",
    )(x2d, mod, mod, lw["norm_mix"], lw["w_main"], lw["w_kvt"], lw["gmat"], lw["q_norm"], kn_t,
      lw["w_gate"], lw["b_gate"], *prev_kv)


def _pair_diag(a):
    z = jnp.zeros((D_SB, a.shape[1]), a.dtype)
    top = jnp.concatenate([a[:D_SB], z], axis=0)
    bot = jnp.concatenate([z, a[D_SB:]], axis=0)
    return jnp.concatenate([top, bot], axis=1)


def _sb_scores(q_pair, k_tiles, biases):
    tiles = []
    for kt, bias in zip(k_tiles, biases):
        z2 = jnp.dot(q_pair, _pair_diag(kt), preferred_element_type=f32)
        for hh in range(2):
            z = z2[:, hh * KB:(hh + 1) * KB] + bias
            sp = jnp.maximum(z, 0.0) + jnp.log(1.0 + jnp.exp(-jnp.abs(z)))
            hi = sp.astype(bf16)
            lo = (sp - hi.astype(f32)).astype(bf16)
            tiles.append((z - sp, jnp.concatenate([hi, lo], axis=1)))
    return tiles


def _sb_apply(tiles, cs, v_tiles, carries, acc, tq):
    new_carries = []
    ws = [[None, None] for _ in v_tiles]
    for hh in range(2):
        c = carries[hh]
        for bi in range(len(v_tiles)):
            ti = bi * 2 + hh
            part = cs[ti * tq:(ti + 1) * tq]
            between = part[:, :KB] if c is None else c + part[:, :KB]
            ws[bi][hh] = jnp.exp(tiles[ti][0] - between).astype(bf16)
            c = part[:, KB:] if c is None else c + part[:, KB:]
        new_carries.append(c)
    for bi, vt in enumerate(v_tiles):
        w_pair = jnp.concatenate(ws[bi], axis=1)
        pv = lax.dot_general(w_pair, _pair_diag(vt), NT, preferred_element_type=f32)
        acc = pv if acc is None else acc + pv
    return acc, new_carries


def _sb_kernel(q_ref, kd_ref, vd_ref, kp_ref, vp_ref, cw_ref, o_ref, acc_ref, carry_ref,
               *, tq, seg, n_past):
    b = pl.program_id(0)
    qi = pl.program_id(1)
    n_prev = qi if n_past is None else n_past
    row = lax.broadcasted_iota(jnp.int32, (tq, KB), 0)
    col = lax.broadcasted_iota(jnp.int32, (tq, KB), 1)
    seg_id = b % (KB // seg)
    seg_shift = seg.bit_length() - 1
    own = jnp.logical_and(lax.shift_right_logical(col, seg_shift) == seg_id,
                          jnp.bitwise_and(col, seg - 1) < row)
    bias0 = jnp.where(own, 0.0, MASKED)
    cw = cw_ref[...]
    pair_rows = [slice(p * KB, (p + 1) * KB) for p in range(N_PAIR)]

    def past_block(j):
        jc = pl.multiple_of(jnp.maximum(j, 0) * KB, KB)
        kt = kp_ref[:, pl.ds(jc, KB)].astype(bf16)
        vt = vp_ref[:, pl.ds(jc, KB)].astype(bf16)
        return kt, vt, jnp.where(j < 0, MASKED, 0.0)

    def run_group(blocks, carries_of, acc_of):
        tiles = [_sb_scores(q_ref[:, rows], [kt[rows] for kt, _, _ in blocks], [bias for _, _, bias in blocks])
                 for rows in pair_rows]
        n_rows = 2 * len(blocks) * tq
        cs = jnp.dot(jnp.concatenate([t[1] for pt in tiles for t in pt], axis=0), cw,
                     preferred_element_type=f32)
        alive = None
        for p, rows in enumerate(pair_rows):
            acc, cr = _sb_apply(tiles[p], cs[p * n_rows:(p + 1) * n_rows], [vt[rows] for _, vt, _ in blocks],
                                carries_of(p), acc_of(p), tq)
            acc_ref[p] = acc
            carry_ref[2 * p] = cr[0]
            carry_ref[2 * p + 1] = cr[1]
            m = jnp.minimum(cr[0], cr[1])
            alive = m if alive is None else jnp.minimum(alive, m)
        return (jnp.min(alive) < SURVIVAL_ZERO).astype(jnp.int32)

    j0 = n_prev - 1
    first = [(kd_ref[...], vd_ref[...], bias0)] + [past_block(j0 - d) for d in range(SB_FIRST_PASS - 1)]
    alive0 = run_group(first, lambda p: (None, None), lambda p: None)

    def cond(st):
        j, alive = st
        return jnp.logical_and(j >= 0, alive > 0)

    def body(st):
        j, _ = st
        blocks = [past_block(j - d) for d in range(SB_LOOP_GROUP)]
        alive = run_group(blocks, lambda p: (carry_ref[2 * p], carry_ref[2 * p + 1]), lambda p: acc_ref[p])
        return j - SB_LOOP_GROUP, alive

    lax.while_loop(cond, body, (j0 - (SB_FIRST_PASS - 1), alive0))
    o_ref[...] = jnp.concatenate([acc_ref[p] for p in range(N_PAIR)], axis=1).astype(bf16)


def _sb_call(q, kd, vd, kp, vp, cw, *, n_batch, t_q, prompt, layer):
    tq = min(KB, t_q)
    nq = t_q // tq
    q_spec = pl.BlockSpec((tq, W_SB), lambda b, i: (b * nq + i, 0))
    if prompt:
        d_spec = pl.BlockSpec((None, W_SB, KB), lambda b, i: (b, 0, i))
        p_spec = pl.BlockSpec((None, W_SB, kp.shape[-1]), lambda b, i: (b, 0, 0))
        seg, n_past = KB, None
    else:
        d_spec = pl.BlockSpec((None, W_SB, KB), lambda b, i: (0, 0, (b * t_q) // KB))
        p_spec = pl.BlockSpec((None, None, W_SB, kp.shape[-1]), lambda b, i: (layer, b, 0, 0))
        seg, n_past = t_q, kp.shape[-1] // KB
    return pl.pallas_call(
        functools.partial(_sb_kernel, tq=tq, seg=seg, n_past=n_past),
        grid=(n_batch, nq),
        in_specs=[q_spec, d_spec, d_spec, p_spec, p_spec, pl.BlockSpec(cw.shape, lambda b, i: (0, 0))],
        out_specs=q_spec,
        out_shape=jax.ShapeDtypeStruct((n_batch * t_q, W_SB), bf16),
        scratch_shapes=[pltpu.VMEM((N_PAIR, tq, LANES), f32), pltpu.VMEM((H_SB, tq, LANES), f32)],
        compiler_params=_params(("parallel", "arbitrary")),
        name="stick_breaking",
    )(q, kd, vd, kp, vp, cw)


GLA_PAIRS = H_GLA // 2
GLA_SAFE_DECAY = 60.0


def _gla_tables(chunk):
    idx = np.arange(chunk)
    t, s = idx[:, None], idx[None, :]
    tri = jnp.asarray((s <= t).astype(np.float32), dtype=bf16)
    level = np.where(t > s, np.floor(np.log2(np.maximum(t ^ s, 1))).astype(np.int32), -1)
    level = np.where(t < s, -2, level).astype(np.int32)
    return tri, jnp.asarray(np.concatenate([level, level], axis=1))


def _level_reference(b, h):
    L = b.shape[0]
    if h >= 4:
        return jnp.concatenate(
            [jnp.broadcast_to(b[c0 + h - 1:c0 + h, :], (2 * h, LANES)) for c0 in range(0, L, 2 * h)], axis=0)
    b3 = b.reshape(L // 8, 8, LANES)
    pos = jnp.bitwise_and(lax.broadcasted_iota(jnp.int32, b3.shape, 1), 2 * h - 1)
    up = lambda n: pltpu.roll(b3, n, 1)
    if h == 1:
        ref = jnp.where(pos == 0, b3, up(1))
    else:
        ref = jnp.where(pos == 0, up(7), jnp.where(pos == 1, b3, jnp.where(pos == 2, up(1), up(2))))
    return ref.reshape(L, LANES)


def _two_heads(a, first_head):
    return jnp.concatenate([jnp.where(first_head, a, 0.0).astype(bf16),
                            jnp.where(first_head, 0.0, a).astype(bf16)], axis=0)


def _gla_att_single(q_in, k, b, lev2, first_head):
    a = lax.dot_general(q_in, _two_heads(k * jnp.exp(-b), first_head), NT, preferred_element_type=f32)
    return jnp.where(lev2 >= -1, a, 0.0)


def _gla_att_levels(q, k, b, lev2, first_head, n_lev):
    a = lax.dot_general(q.astype(bf16), _two_heads(k, first_head), NT, preferred_element_type=f32)
    att = jnp.where(lev2 == -1, a, 0.0)
    for lev in range(n_lev):
        d = b - _level_reference(b, 1 << lev)
        e = jnp.exp(-jnp.abs(d))
        is_query = d < 0.0
        q_l = (q * jnp.where(is_query, e, 1.0)).astype(bf16)
        k_l = k * jnp.where(is_query, 1.0, e)
        a = lax.dot_general(q_l, _two_heads(k_l, first_head), NT, preferred_element_type=f32)
        att = jnp.where(lev2 == lev, a, att)
    return att


def _gla_kernel(q_ref, k_ref, v_ref, g_ref, og_ref, s0_ref, tri_ref, lev_ref, gn_ref,
                o_ref, sout_ref, s_ref, att_ref, *, chunk, bb):
    c = pl.program_id(1)
    L = chunk
    n_lev = L.bit_length() - 1
    chains = [(bi, p) for bi in range(bb) for p in range(GLA_PAIRS)]
    zeros_s = jnp.zeros((DK_GLA, DV_GLA), f32)

    @pl.when(c == 0)
    def _():
        for bi, p in chains:
            s0 = s0_ref[bi, p]
            s_ref[bi, p] = jnp.concatenate(
                [jnp.concatenate([s0[:DK_GLA], zeros_s], axis=1),
                 jnp.concatenate([zeros_s, s0[DK_GLA:]], axis=1)], axis=0)

    lane = lax.broadcasted_iota(jnp.int32, (L, LANES), 1)
    first_head = lane < DK_GLA
    lev2 = lev_ref[...]
    tri = tri_ref[...]

    pre = []
    floor = None
    for bi, p in chains:
        lanes = slice(p * LANES, (p + 1) * LANES)
        q = q_ref[bi, :, lanes]
        k = k_ref[bi, :, lanes]
        g = g_ref[bi, :, lanes]
        g_hi = g.astype(bf16)
        g_lo = (g - g_hi.astype(f32)).astype(bf16)
        g2 = jnp.concatenate([g_hi, g_lo], axis=1)
        bs = jnp.dot(tri, g2, preferred_element_type=f32)
        b = bs[:, :LANES] + bs[:, LANES:]
        b_last = b[L - 1:L, :]
        q_in = (q * jnp.exp(b)).astype(bf16)
        k_end = (k * jnp.exp(b_last - b)).astype(bf16)
        pre.append((q, k, g2, b, q_in, k_end))
        floor = b_last if floor is None else jnp.minimum(floor, b_last)
    single = jnp.min(floor) >= -GLA_SAFE_DECAY

    @pl.when(single)
    def _():
        for ci, (q, k, g2, b, q_in, k_end) in enumerate(pre):
            att_ref[ci] = _gla_att_single(q_in, k, b, lev2, first_head)

    @pl.when(jnp.logical_not(single))
    def _():
        for ci, (q, k, g2, b, q_in, k_end) in enumerate(pre):
            att_ref[ci] = _gla_att_levels(q, k, b, lev2, first_head, n_lev)

    rowi = lax.broadcasted_iota(jnp.int32, (2 * DK_GLA, 2 * DV_GLA), 0)
    coli = lax.broadcasted_iota(jnp.int32, (2 * DK_GLA, 2 * DV_GLA), 1)
    own_block = (rowi < DK_GLA) == (coli < DV_GLA)
    ones = jnp.ones((L, LANES), bf16)
    zeros_v = jnp.zeros((L, DV_GLA), bf16)
    gn = gn_ref[...]
    for ci, (bi, p) in enumerate(chains):
        q, k, g2, b, q_in, k_end = pre[ci]
        cols = slice(p * 2 * DV_GLA, (p + 1) * 2 * DV_GLA)
        v = v_ref[bi, :, cols].astype(bf16)
        v_diag = jnp.concatenate([jnp.concatenate([v[:, :DV_GLA], zeros_v], axis=1),
                                  jnp.concatenate([zeros_v, v[:, DV_GLA:]], axis=1)], axis=0)
        s_old = s_ref[bi, p]
        o2 = (jnp.dot(att_ref[ci].astype(bf16), v_diag, preferred_element_type=f32)
              + jnp.dot(q_in, s_old.astype(bf16), preferred_element_type=f32))
        tot = lax.dot_general(g2, ones, TN, preferred_element_type=f32)
        decay = jnp.exp(tot[:LANES] + tot[LANES:])
        upd = lax.dot_general(k_end, v, TN, preferred_element_type=f32)
        s_ref[bi, p] = jnp.concatenate([decay, decay], axis=1) * s_old + jnp.where(own_block, upd, 0.0)
        og = og_ref[bi, :, cols]
        outs = []
        for hh in range(2):
            o = o2[:, hh * DV_GLA:(hh + 1) * DV_GLA]
            ms = jnp.mean(o * o, axis=-1, keepdims=True)
            gate = og[:, hh * DV_GLA:(hh + 1) * DV_GLA]
            outs.append(o * lax.rsqrt(ms + EPS) * gn * (gate * jax.nn.sigmoid(gate)))
        o_ref[bi, :, cols] = jnp.concatenate(outs, axis=1).astype(bf16)

    @pl.when(c == pl.num_programs(1) - 1)
    def _():
        for bi, p in chains:
            s = s_ref[bi, p]
            sout_ref[bi, p] = jnp.concatenate([s[:DK_GLA, :DV_GLA], s[DK_GLA:, DV_GLA:]], axis=0)


def _gla_call(qg, kg, vg, g, og, s0, gla_norm, n_batch, t, chunk, bb):
    nc = t // chunk
    tri, lev2 = _gla_tables(chunk)
    kspec = pl.BlockSpec((bb, chunk, WK_GLA), lambda i, c: (i, c, 0))
    vspec = pl.BlockSpec((bb, chunk, WV_GLA), lambda i, c: (i, c, 0))
    sspec = pl.BlockSpec((bb, GLA_PAIRS, 2 * DK_GLA, DV_GLA), lambda i, c: (i, 0, 0, 0))
    const = lambda a: pl.BlockSpec(a.shape, lambda i, c: (0,) * a.ndim)
    s0 = s0.reshape(n_batch, GLA_PAIRS, 2 * DK_GLA, DV_GLA)
    k3 = lambda a: a.reshape(n_batch, t, WK_GLA)
    v3 = lambda a: a.reshape(n_batch, t, WV_GLA)
    o, s_out = pl.pallas_call(
        functools.partial(_gla_kernel, chunk=chunk, bb=bb),
        grid=(n_batch // bb, nc),
        in_specs=[kspec, kspec, vspec, kspec, vspec, sspec, const(tri), const(lev2), const(gla_norm)],
        out_specs=[vspec, sspec],
        out_shape=[jax.ShapeDtypeStruct((n_batch, t, WV_GLA), bf16),
                   jax.ShapeDtypeStruct(s0.shape, f32)],
        scratch_shapes=[pltpu.VMEM((bb, GLA_PAIRS, 2 * DK_GLA, 2 * DV_GLA), f32),
                        pltpu.VMEM((bb * GLA_PAIRS, chunk, 2 * chunk), f32)],
        compiler_params=_params(("parallel", "arbitrary")),
        name="gla",
    )(k3(qg), k3(kg), v3(vg), k3(g), v3(og), s0, tri, lev2, gla_norm)
    return o.reshape(n_batch * t, WV_GLA), s_out.reshape(n_batch, H_GLA, DK_GLA, DV_GLA)


def _mlp_kernel(a_ref, o_ref, x_ref, mix_gate_ref, sh_ref, sc_ref, gate_ref, nw_ref, wo_ref, wu_ref, wd_ref,
                y_ref, *, ff_chunk):
    m = (jnp.dot(a_ref[...], wo_ref[:W_SB, :], preferred_element_type=f32)
         + jnp.dot(o_ref[...], wo_ref[W_SB:, :], preferred_element_type=f32))
    x = x_ref[...] + mix_gate_ref[...] * m
    ms = jnp.mean(x * x, axis=-1, keepdims=True)
    y = x * lax.rsqrt(ms + EPS) * nw_ref[...]
    h = (y * (1.0 + sc_ref[...]) + sh_ref[...]).astype(bf16)
    acc = jnp.zeros(x.shape, f32)
    for c0 in range(0, D_FF, ff_chunk):
        u = jnp.dot(h, wu_ref[:, c0:c0 + ff_chunk], preferred_element_type=f32)
        u = jnp.square(jnp.maximum(u, 0.0)).astype(bf16)
        acc = acc + jnp.dot(u, wd_ref[c0:c0 + ff_chunk, :], preferred_element_type=f32)
    y_ref[...] = x + gate_ref[...] * acc


def _mlp_call(a, o, x2d, mod, lw, tm, blocks_per_batch):
    n = x2d.shape[0]
    r = mod.shape[1]
    row = lambda w: pl.BlockSpec((tm, w), lambda i: (i, 0))
    const = lambda a: pl.BlockSpec(a.shape, lambda i: (0,) * a.ndim)
    piece = lambda k: _mod_spec(r, blocks_per_batch, k)
    return pl.pallas_call(
        functools.partial(_mlp_kernel, ff_chunk=1024),
        grid=(n // tm,),
        in_specs=[row(W_SB), row(WV_GLA), row(D_MODEL), piece(2), piece(3), piece(4), piece(5),
                  const(lw["norm_mlp"]), const(lw["w_out"]), const(lw["w_up"]), const(lw["w_down"])],
        out_specs=row(D_MODEL),
        out_shape=jax.ShapeDtypeStruct((n, D_MODEL), f32),
        compiler_params=_params(("parallel",)),
        name="out_mlp",
    )(a, o, x2d, mod, mod, mod, mod, lw["norm_mlp"], lw["w_out"], lw["w_up"], lw["w_down"])


def _layer_weights(l, w_in, w_out, w_up, w_down, norm_mix, norm_mlp, q_norm, k_norm, w_gate, b_gate,
                   gla_norm):
    wi = w_in[l]
    c_k, c_v, c_g = W_SB, 2 * W_SB, 3 * W_SB
    c_gate = c_g + 2 * WK_GLA + WV_GLA
    w_main = jnp.concatenate(
        [wi[:, :c_k], wi[:, c_g:c_gate], wi[:, c_gate + GATE_RANK:], wi[:, c_gate:c_gate + GATE_RANK],
         jnp.zeros((D_MODEL, GATE_PAD - GATE_RANK), f32)], axis=1).astype(bf16)
    w_kvt = wi[:, c_k:c_g].T.astype(bf16)
    wg = jnp.concatenate([w_gate[l], jnp.zeros((GATE_PAD - GATE_RANK, WK_GLA), f32)], axis=0).astype(bf16)
    head = np.arange(W_SB) // D_SB
    gmat = jnp.asarray((head[:, None] == head[None, :]).astype(np.float32) / D_SB, dtype=bf16)
    return {
        "w_main": w_main, "w_kvt": w_kvt, "w_gate": wg, "gmat": gmat,
        "b_gate": b_gate[l].reshape(1, WK_GLA),
        "q_norm": jnp.tile(q_norm[l], H_SB).reshape(1, W_SB),
        "k_norm": jnp.tile(k_norm[l], H_SB),
        "norm_mix": norm_mix[l].reshape(1, D_MODEL),
        "norm_mlp": norm_mlp[l].reshape(1, D_MODEL),
        "gla_norm": gla_norm[l].reshape(1, DV_GLA),
        "w_out": w_out[l].astype(bf16), "w_up": w_up[l].astype(bf16), "w_down": w_down[l].astype(bf16),
    }


def _group_layer(x2d, mod, lw, cw, n_batch, t, past_kt, past_vt, layer, s0, tm, blocks_per_batch, chunk,
                 gla_bb, prev_kv=()):
    q, kt, vt, ktb, vtb, qg, kg, vg, g, og = _in_call(x2d, mod, lw, tm, blocks_per_batch, prev_kv)
    if past_kt is None:
        a = _sb_call(q, ktb, vtb, ktb, vtb, cw, n_batch=n_batch, t_q=t, prompt=True, layer=layer)
    else:
        a = _sb_call(q, ktb, vtb, past_kt, past_vt, cw, n_batch=n_batch, t_q=t, prompt=False, layer=layer)
    o, s_new = _gla_call(qg, kg, vg, g, og, s0, lw["gla_norm"], n_batch, t, chunk, gla_bb)
    x2d = _mlp_call(a, o, x2d, mod, lw, tm, blocks_per_batch)
    return x2d, kt, vt, s_new


def _heads_last(kt, n_batch, t):
    layers, nb = kt.shape[:2]
    return kt.reshape(layers, nb, H_SB, D_SB, -1).transpose(0, 1, 4, 2, 3).reshape(layers, n_batch, t, H_SB, D_SB)


def kernel(x_prompt, x_sample, c_prompt, c_sample, cache_k, cache_v, state_gla, w_ada, b_ada, norm_mix,
           norm_mlp, w_in, q_norm, k_norm, w_gate, b_gate, gla_norm, w_out, w_up, w_down):
    bp, tp, _ = x_prompt.shape
    bs, ts, _ = x_sample.shape
    past = cache_k.shape[2]
    n_c = bp + bs
    c_rows = -(-n_c // 16) * 16
    c_all = jnp.concatenate([c_prompt, c_sample, jnp.zeros((c_rows - n_c, D_MODEL), f32)], axis=0)
    mods = _ada_call(c_all, w_ada, b_ada)

    kk = np.arange(KB)
    half = np.concatenate([(kk[:, None] > kk[None, :]), np.ones((KB, KB), bool)], axis=1)
    cw = jnp.asarray(np.concatenate([half, half], axis=0).astype(np.float32), dtype=bf16)

    cache_kt = cache_k.transpose(0, 1, 3, 4, 2).reshape(DEPTH, bs, W_SB, past)
    cache_vt = cache_v.transpose(0, 1, 3, 4, 2).reshape(DEPTH, bs, W_SB, past)

    tm_p, tm_s = 512, bs * ts
    xp = x_prompt.reshape(bp * tp, D_MODEL)
    xs = x_sample.reshape(bs * ts, D_MODEL)
    sp_l, ks_l, vs_l, ss_l = [], [], [], []
    prompt_kv = ()
    for l in range(DEPTH):
        lw = _layer_weights(l, w_in, w_out, w_up, w_down, norm_mix, norm_mlp, q_norm, k_norm, w_gate,
                            b_gate, gla_norm)
        mod_p = mods[l, :bp].reshape(bp, 1, N_MOD * D_MODEL)
        mod_s = jnp.repeat(mods[l, bp:n_c], ts, axis=0).reshape(1, bs * ts, N_MOD * D_MODEL)
        s0_p = jnp.zeros((bp, H_GLA, DK_GLA, DV_GLA), f32)
        xp, kp, vp, sp = _group_layer(xp, mod_p, lw, cw, bp, tp, None, None, l, s0_p, tm_p, tp // tm_p, 128, bp,
                                      prompt_kv)
        prompt_kv = (kp, vp)
        xs, kn, vn, sn = _group_layer(xs, mod_s, lw, cw, bs, ts, cache_kt, cache_vt, l,
                                      state_gla[l].astype(f32), tm_s, 1, ts, 4)
        sp_l.append(sp)
        ks_l.append(_heads_last(kn, bs, ts)[0])
        vs_l.append(_heads_last(vn, bs, ts)[0])
        ss_l.append(sn)
    return (xp.reshape(bp, tp, D_MODEL), xs.reshape(bs, ts, D_MODEL),
            _heads_last(kp, bp, tp), _heads_last(vp, bp, tp), jnp.stack(sp_l),
            jnp.stack(ks_l), jnp.stack(vs_l), jnp.stack(ss_l))
```

```python
import functools

import numpy as np
import jax
import jax.numpy as jnp
from jax import lax
from jax.experimental import pallas as pl
from jax.experimental.pallas import tpu as pltpu

D_MODEL = 1024
DEPTH = 2
H_SB = 8
D_SB = 64
W_SB = H_SB * D_SB
H_GLA = 4
DK_GLA = 64
DV_GLA = 128
WK_GLA = H_GLA * DK_GLA
WV_GLA = H_GLA * DV_GLA
GATE_RANK = 16
GATE_TAU = 16.0
D_FF = 4 * D_MODEL
N_MOD = 6
EPS = 1e-6

LANES = 128
KB = 128
SB_FIRST_PASS = 3
SB_LOOP_GROUP = 1
N_PAIR = H_SB // 2
GATE_PAD = LANES
VMEM_LIMIT = 56 * 1024 * 1024
SURVIVAL_ZERO = 104.0
MASKED = -1e30

f32 = jnp.float32
bf16 = jnp.bfloat16
NT = (((1,), (1,)), ((), ()))
TN = (((0,), (0,)), ((), ()))


def _params(sem):
    return pltpu.CompilerParams(dimension_semantics=sem, vmem_limit_bytes=VMEM_LIMIT)


def _log_sigmoid(x):
    return jnp.minimum(x, 0.0) - jnp.log1p(jnp.exp(-jnp.abs(x)))


def _ada_kernel(c_ref, w_ref, b_ref, o_ref):
    c = c_ref[...]
    s = (c * jax.nn.sigmoid(c)).astype(bf16)
    o_ref[...] = jnp.dot(s, w_ref[...].astype(bf16), preferred_element_type=f32) + b_ref[...]


def _ada_call(c_all, w_ada, b_ada):
    rows = c_all.shape[0]
    return pl.pallas_call(
        _ada_kernel,
        grid=(DEPTH, N_MOD),
        in_specs=[
            pl.BlockSpec((rows, D_MODEL), lambda l, p: (0, 0)),
            pl.BlockSpec((None, D_MODEL, D_MODEL), lambda l, p: (l, 0, p)),
            pl.BlockSpec((None, 1, D_MODEL), lambda l, p: (l, 0, p)),
        ],
        out_specs=pl.BlockSpec((None, rows, D_MODEL), lambda l, p: (l, 0, p)),
        out_shape=jax.ShapeDtypeStruct((DEPTH, rows, N_MOD * D_MODEL), f32),
        compiler_params=_params(("parallel", "parallel")),
        name="ada_mod",
    )(c_all, w_ada, b_ada.reshape(DEPTH, 1, N_MOD * D_MODEL))


def _mod_spec(rows_per_block, blocks_per_batch, piece):
    return pl.BlockSpec((None, rows_per_block, D_MODEL),
                        lambda i: (i // blocks_per_batch, 0, piece))


def _head_rms(t):
    t3 = t.reshape(H_SB, D_SB, t.shape[1])
    ms = jnp.mean(t3 * t3, axis=1, keepdims=True)
    return (t3 * lax.rsqrt(ms + EPS)).reshape(t.shape)


def _in_kernel(*refs, n_prev):
    (x_ref, sh_ref, sc_ref, nw_ref, w_ref, wqkv_ref, qn_ref, kn_ref, wg_ref, bg_ref) = refs[:10]
    prev_refs = refs[10:10 + 2 * min(n_prev, 1)]
    (q_ref, kt_ref, vt_ref, ktb_ref, vtb_ref, qg_ref, kg_ref, vg_ref, g_ref, og_ref) = refs[len(refs) - 10:]
    for prev_ref, out_ref in zip(prev_refs, (kt_ref, vt_ref)):
        out_ref[:n_prev] = prev_ref[...]
    x = x_ref[...]
    ms = jnp.mean(x * x, axis=-1, keepdims=True)
    y = x * lax.rsqrt(ms + EPS) * nw_ref[...]
    h = (y * (1.0 + sc_ref[...]) + sh_ref[...]).astype(bf16)

    def proj(lo, hi):
        return jnp.dot(h, w_ref[:, lo:hi], preferred_element_type=f32)

    qt = lax.dot_general(wqkv_ref[:W_SB, :], h, NT, preferred_element_type=f32)
    qt = _head_rms(qt) * qn_ref[...] * (D_SB ** -0.5)
    q_ref[...] = qt.T.astype(bf16)
    kt = lax.dot_general(wqkv_ref[W_SB:2 * W_SB, :], h, NT, preferred_element_type=f32)
    kt = _head_rms(kt) * kn_ref[...]
    kt_ref[n_prev] = kt
    ktb_ref[...] = kt.astype(bf16)
    vt = lax.dot_general(wqkv_ref[2 * W_SB:, :], h, NT, preferred_element_type=f32)
    vt_ref[n_prev] = vt
    vtb_ref[...] = vt.astype(bf16)

    o = 0
    qg_ref[...] = proj(o, o + WK_GLA) * (DK_GLA ** -0.5)
    o += WK_GLA
    kg_ref[...] = proj(o, o + WK_GLA)
    o += WK_GLA
    vg_ref[...] = proj(o, o + WV_GLA)
    o += WV_GLA
    og_ref[...] = proj(o, o + WV_GLA)
    o += WV_GLA
    gr = proj(o, o + GATE_PAD).astype(bf16)
    xg = jnp.dot(gr, wg_ref[...], preferred_element_type=f32) + bg_ref[...]
    g_ref[...] = _log_sigmoid(xg) * (1.0 / GATE_TAU)


def _in_call(x2d, mod, lw, tm, blocks_per_batch, prev_kv):
    n = x2d.shape[0]
    r = mod.shape[1]
    n_batch = n // (tm * blocks_per_batch)
    t = tm * blocks_per_batch
    n_prev = prev_kv[0].shape[0] if prev_kv else 0
    row = lambda w: pl.BlockSpec((tm, w), lambda i: (i, 0))
    col_map = lambda i: (i // blocks_per_batch, 0, i % blocks_per_batch)
    col = pl.BlockSpec((None, W_SB, tm), col_map)
    stack = lambda depth: pl.BlockSpec((depth, None, W_SB, tm), lambda i: (0,) + col_map(i))
    const = lambda a: pl.BlockSpec(a.shape, lambda i: (0,) * a.ndim)
    qn_t = jnp.broadcast_to(lw["q_norm"].reshape(W_SB, 1), (W_SB, tm))
    kn_t = jnp.broadcast_to(lw["k_norm"].reshape(W_SB, 1), (W_SB, tm))
    rows_out = [(WK_GLA, f32), (WK_GLA, f32), (WV_GLA, f32), (WK_GLA, f32), (WV_GLA, f32)]
    kv_stack = jax.ShapeDtypeStruct((n_prev + 1, n_batch, W_SB, t), f32)
    kv_b = jax.ShapeDtypeStruct((n_batch, W_SB, t), bf16)
    return pl.pallas_call(
        functools.partial(_in_kernel, n_prev=n_prev),
        grid=(n // tm,),
        in_specs=[row(D_MODEL), _mod_spec(r, blocks_per_batch, 0), _mod_spec(r, blocks_per_batch, 1),
                  const(lw["norm_mix"]), const(lw["w_main"]), const(lw["w_qkvt"]),
                  const(qn_t), const(kn_t), const(lw["w_gate"]), const(lw["b_gate"])]
        + [stack(n_prev)] * len(prev_kv),
        out_specs=[row(W_SB), stack(n_prev + 1), stack(n_prev + 1), col, col] + [row(w) for w, _ in rows_out],
        out_shape=[jax.ShapeDtypeStruct((n, W_SB), bf16), kv_stack, kv_stack, kv_b, kv_b]
        + [jax.ShapeDtypeStruct((n, w), dt) for w, dt in rows_out],
        compiler_params=_params(("parallel",)),
        name="in_proj",
    )(x2d, mod, mod, lw["norm_mix"], lw["w_main"], lw["w_qkvt"], qn_t, kn_t,
      lw["w_gate"], lw["b_gate"], *prev_kv)


def _pair_diag(a):
    z = jnp.zeros((D_SB, a.shape[1]), a.dtype)
    top = jnp.concatenate([a[:D_SB], z], axis=0)
    bot = jnp.concatenate([z, a[D_SB:]], axis=0)
    return jnp.concatenate([top, bot], axis=1)


def _sb_scores(q_pair, k_tiles, biases):
    tiles = []
    for kt, bias in zip(k_tiles, biases):
        z2 = jnp.dot(q_pair, _pair_diag(kt), preferred_element_type=f32)
        for hh in range(2):
            z = z2[:, hh * KB:(hh + 1) * KB] + bias
            sp = jnp.maximum(z, 0.0) + jnp.log(1.0 + jnp.exp(-jnp.abs(z)))
            hi = sp.astype(bf16)
            lo = (sp - hi.astype(f32)).astype(bf16)
            tiles.append((z - sp, jnp.concatenate([hi, lo], axis=1)))
    return tiles


def _sb_apply(tiles, cs, v_tiles, carries, acc, tq):
    new_carries = []
    ws = [[None, None] for _ in v_tiles]
    for hh in range(2):
        c = carries[hh]
        for bi in range(len(v_tiles)):
            ti = bi * 2 + hh
            part = cs[ti * tq:(ti + 1) * tq]
            between = part[:, :KB] if c is None else c + part[:, :KB]
            ws[bi][hh] = jnp.exp(tiles[ti][0] - between).astype(bf16)
            c = part[:, KB:] if c is None else c + part[:, KB:]
        new_carries.append(c)
    for bi, vt in enumerate(v_tiles):
        w_pair = jnp.concatenate(ws[bi], axis=1)
        pv = lax.dot_general(w_pair, _pair_diag(vt), NT, preferred_element_type=f32)
        acc = pv if acc is None else acc + pv
    return acc, new_carries


def _sb_kernel(q_ref, kd_ref, vd_ref, kp_ref, vp_ref, cw_ref, o_ref, acc_ref, carry_ref,
               *, tq, seg, n_past):
    b = pl.program_id(0)
    qi = pl.program_id(1)
    n_prev = qi if n_past is None else n_past
    row = lax.broadcasted_iota(jnp.int32, (tq, KB), 0)
    col = lax.broadcasted_iota(jnp.int32, (tq, KB), 1)
    seg_id = b % (KB // seg)
    seg_shift = seg.bit_length() - 1
    own = jnp.logical_and(lax.shift_right_logical(col, seg_shift) == seg_id,
                          jnp.bitwise_and(col, seg - 1) < row)
    bias0 = jnp.where(own, 0.0, MASKED)
    cw = cw_ref[...]
    pair_rows = [slice(p * KB, (p + 1) * KB) for p in range(N_PAIR)]

    def past_block(j):
        jc = pl.multiple_of(jnp.maximum(j, 0) * KB, KB)
        kt = kp_ref[:, pl.ds(jc, KB)].astype(bf16)
        vt = vp_ref[:, pl.ds(jc, KB)].astype(bf16)
        return kt, vt, jnp.where(j < 0, MASKED, 0.0)

    def run_group(blocks, carries_of, acc_of):
        tiles = [_sb_scores(q_ref[:, rows], [kt[rows] for kt, _, _ in blocks], [bias for _, _, bias in blocks])
                 for rows in pair_rows]
        n_rows = 2 * len(blocks) * tq
        cs = jnp.dot(jnp.concatenate([t[1] for pt in tiles for t in pt], axis=0), cw,
                     preferred_element_type=f32)
        alive = None
        for p, rows in enumerate(pair_rows):
            acc, cr = _sb_apply(tiles[p], cs[p * n_rows:(p + 1) * n_rows], [vt[rows] for _, vt, _ in blocks],
                                carries_of(p), acc_of(p), tq)
            acc_ref[p] = acc
            carry_ref[2 * p] = cr[0]
            carry_ref[2 * p + 1] = cr[1]
            m = jnp.minimum(cr[0], cr[1])
            alive = m if alive is None else jnp.minimum(alive, m)
        return (jnp.min(alive) < SURVIVAL_ZERO).astype(jnp.int32)

    j0 = n_prev - 1
    first = [(kd_ref[...], vd_ref[...], bias0)] + [past_block(j0 - d) for d in range(SB_FIRST_PASS - 1)]
    alive0 = run_group(first, lambda p: (None, None), lambda p: None)

    def cond(st):
        j, alive = st
        return jnp.logical_and(j >= 0, alive > 0)

    def body(st):
        j, _ = st
        blocks = [past_block(j - d) for d in range(SB_LOOP_GROUP)]
        alive = run_group(blocks, lambda p: (carry_ref[2 * p], carry_ref[2 * p + 1]), lambda p: acc_ref[p])
        return j - SB_LOOP_GROUP, alive

    lax.while_loop(cond, body, (j0 - (SB_FIRST_PASS - 1), alive0))
    o_ref[...] = jnp.concatenate([acc_ref[p] for p in range(N_PAIR)], axis=1).astype(bf16)


def _sb_call(q, kd, vd, kp, vp, cw, *, n_batch, t_q, prompt, layer):
    tq = min(KB, t_q)
    nq = t_q // tq
    q_spec = pl.BlockSpec((tq, W_SB), lambda b, i: (b * nq + i, 0))
    if prompt:
        d_spec = pl.BlockSpec((None, W_SB, KB), lambda b, i: (b, 0, i))
        p_spec = pl.BlockSpec((None, W_SB, kp.shape[-1]), lambda b, i: (b, 0, 0))
        seg, n_past = KB, None
    else:
        d_spec = pl.BlockSpec((None, W_SB, KB), lambda b, i: (0, 0, (b * t_q) // KB))
        p_spec = pl.BlockSpec((None, None, W_SB, kp.shape[-1]), lambda b, i: (layer, b, 0, 0))
        seg, n_past = t_q, kp.shape[-1] // KB
    return pl.pallas_call(
        functools.partial(_sb_kernel, tq=tq, seg=seg, n_past=n_past),
        grid=(n_batch, nq),
        in_specs=[q_spec, d_spec, d_spec, p_spec, p_spec, pl.BlockSpec(cw.shape, lambda b, i: (0, 0))],
        out_specs=q_spec,
        out_shape=jax.ShapeDtypeStruct((n_batch * t_q, W_SB), bf16),
        scratch_shapes=[pltpu.VMEM((N_PAIR, tq, LANES), f32), pltpu.VMEM((H_SB, tq, LANES), f32)],
        compiler_params=_params(("parallel", "arbitrary")),
        name="stick_breaking",
    )(q, kd, vd, kp, vp, cw)


GLA_PAIRS = H_GLA // 2
GLA_SAFE_DECAY = 60.0


def _gla_tables(chunk):
    idx = np.arange(chunk)
    t, s = idx[:, None], idx[None, :]
    tri = jnp.asarray((s <= t).astype(np.float32), dtype=bf16)
    level = np.where(t > s, np.floor(np.log2(np.maximum(t ^ s, 1))).astype(np.int32), -1)
    level = np.where(t < s, -2, level).astype(np.int32)
    return tri, jnp.asarray(np.concatenate([level, level], axis=1))


def _level_reference(b, h):
    L = b.shape[0]
    if h >= 4:
        return jnp.concatenate(
            [jnp.broadcast_to(b[c0 + h - 1:c0 + h, :], (2 * h, LANES)) for c0 in range(0, L, 2 * h)], axis=0)
    b3 = b.reshape(L // 8, 8, LANES)
    pos = jnp.bitwise_and(lax.broadcasted_iota(jnp.int32, b3.shape, 1), 2 * h - 1)
    up = lambda n: pltpu.roll(b3, n, 1)
    if h == 1:
        ref = jnp.where(pos == 0, b3, up(1))
    else:
        ref = jnp.where(pos == 0, up(7), jnp.where(pos == 1, b3, jnp.where(pos == 2, up(1), up(2))))
    return ref.reshape(L, LANES)


def _two_heads(a, first_head):
    return jnp.concatenate([jnp.where(first_head, a, 0.0).astype(bf16),
                            jnp.where(first_head, 0.0, a).astype(bf16)], axis=0)


def _gla_att_single(q_in, k, b, lev2, first_head):
    a = lax.dot_general(q_in, _two_heads(k * jnp.exp(-b), first_head), NT, preferred_element_type=f32)
    return jnp.where(lev2 >= -1, a, 0.0)


def _gla_att_levels(q, k, b, lev2, first_head, n_lev):
    a = lax.dot_general(q.astype(bf16), _two_heads(k, first_head), NT, preferred_element_type=f32)
    att = jnp.where(lev2 == -1, a, 0.0)
    for lev in range(n_lev):
        d = b - _level_reference(b, 1 << lev)
        e = jnp.exp(-jnp.abs(d))
        is_query = d < 0.0
        q_l = (q * jnp.where(is_query, e, 1.0)).astype(bf16)
        k_l = k * jnp.where(is_query, 1.0, e)
        a = lax.dot_general(q_l, _two_heads(k_l, first_head), NT, preferred_element_type=f32)
        att = jnp.where(lev2 == lev, a, att)
    return att


def _gla_kernel(q_ref, k_ref, v_ref, g_ref, og_ref, s0_ref, tri_ref, lev_ref, gn_ref,
                o_ref, sout_ref, s_ref, att_ref, *, chunk, bb):
    c = pl.program_id(1)
    L = chunk
    n_lev = L.bit_length() - 1
    chains = [(bi, p) for bi in range(bb) for p in range(GLA_PAIRS)]
    zeros_s = jnp.zeros((DK_GLA, DV_GLA), f32)

    @pl.when(c == 0)
    def _():
        for bi, p in chains:
            s0 = s0_ref[bi, p]
            s_ref[bi, p] = jnp.concatenate(
                [jnp.concatenate([s0[:DK_GLA], zeros_s], axis=1),
                 jnp.concatenate([zeros_s, s0[DK_GLA:]], axis=1)], axis=0)

    lane = lax.broadcasted_iota(jnp.int32, (L, LANES), 1)
    first_head = lane < DK_GLA
    lev2 = lev_ref[...]
    tri = tri_ref[...]

    pre = []
    floor = None
    for bi, p in chains:
        lanes = slice(p * LANES, (p + 1) * LANES)
        q = q_ref[bi, :, lanes]
        k = k_ref[bi, :, lanes]
        g = g_ref[bi, :, lanes]
        g_hi = g.astype(bf16)
        g_lo = (g - g_hi.astype(f32)).astype(bf16)
        g2 = jnp.concatenate([g_hi, g_lo], axis=1)
        bs = jnp.dot(tri, g2, preferred_element_type=f32)
        b = bs[:, :LANES] + bs[:, LANES:]
        b_last = b[L - 1:L, :]
        q_in = (q * jnp.exp(b)).astype(bf16)
        k_end = (k * jnp.exp(b_last - b)).astype(bf16)
        pre.append((q, k, g2, b, q_in, k_end))
        floor = b_last if floor is None else jnp.minimum(floor, b_last)
    single = jnp.min(floor) >= -GLA_SAFE_DECAY

    @pl.when(single)
    def _():
        for ci, (q, k, g2, b, q_in, k_end) in enumerate(pre):
            att_ref[ci] = _gla_att_single(q_in, k, b, lev2, first_head)

    @pl.when(jnp.logical_not(single))
    def _():
        for ci, (q, k, g2, b, q_in, k_end) in enumerate(pre):
            att_ref[ci] = _gla_att_levels(q, k, b, lev2, first_head, n_lev)

    rowi = lax.broadcasted_iota(jnp.int32, (2 * DK_GLA, 2 * DV_GLA), 0)
    coli = lax.broadcasted_iota(jnp.int32, (2 * DK_GLA, 2 * DV_GLA), 1)
    own_block = (rowi < DK_GLA) == (coli < DV_GLA)
    ones = jnp.ones((L, LANES), bf16)
    zeros_v = jnp.zeros((L, DV_GLA), bf16)
    gn = gn_ref[...]
    for ci, (bi, p) in enumerate(chains):
        q, k, g2, b, q_in, k_end = pre[ci]
        cols = slice(p * 2 * DV_GLA, (p + 1) * 2 * DV_GLA)
        v = v_ref[bi, :, cols].astype(bf16)
        v_diag = jnp.concatenate([jnp.concatenate([v[:, :DV_GLA], zeros_v], axis=1),
                                  jnp.concatenate([zeros_v, v[:, DV_GLA:]], axis=1)], axis=0)
        s_old = s_ref[bi, p]
        o2 = (jnp.dot(att_ref[ci].astype(bf16), v_diag, preferred_element_type=f32)
              + jnp.dot(q_in, s_old.astype(bf16), preferred_element_type=f32))
        tot = lax.dot_general(g2, ones, TN, preferred_element_type=f32)
        decay = jnp.exp(tot[:LANES] + tot[LANES:])
        upd = lax.dot_general(k_end, v, TN, preferred_element_type=f32)
        s_ref[bi, p] = jnp.concatenate([decay, decay], axis=1) * s_old + jnp.where(own_block, upd, 0.0)
        og = og_ref[bi, :, cols]
        outs = []
        for hh in range(2):
            o = o2[:, hh * DV_GLA:(hh + 1) * DV_GLA]
            ms = jnp.mean(o * o, axis=-1, keepdims=True)
            gate = og[:, hh * DV_GLA:(hh + 1) * DV_GLA]
            outs.append(o * lax.rsqrt(ms + EPS) * gn * (gate * jax.nn.sigmoid(gate)))
        o_ref[bi, :, cols] = jnp.concatenate(outs, axis=1).astype(bf16)

    @pl.when(c == pl.num_programs(1) - 1)
    def _():
        for bi, p in chains:
            s = s_ref[bi, p]
            sout_ref[bi, p] = jnp.concatenate([s[:DK_GLA, :DV_GLA], s[DK_GLA:, DV_GLA:]], axis=0)


def _gla_call(qg, kg, vg, g, og, s0, gla_norm, n_batch, t, chunk, bb):
    nc = t // chunk
    tri, lev2 = _gla_tables(chunk)
    kspec = pl.BlockSpec((bb, chunk, WK_GLA), lambda i, c: (i, c, 0))
    vspec = pl.BlockSpec((bb, chunk, WV_GLA), lambda i, c: (i, c, 0))
    sspec = pl.BlockSpec((bb, GLA_PAIRS, 2 * DK_GLA, DV_GLA), lambda i, c: (i, 0, 0, 0))
    const = lambda a: pl.BlockSpec(a.shape, lambda i, c: (0,) * a.ndim)
    s0 = s0.reshape(n_batch, GLA_PAIRS, 2 * DK_GLA, DV_GLA)
    k3 = lambda a: a.reshape(n_batch, t, WK_GLA)
    v3 = lambda a: a.reshape(n_batch, t, WV_GLA)
    o, s_out = pl.pallas_call(
        functools.partial(_gla_kernel, chunk=chunk, bb=bb),
        grid=(n_batch // bb, nc),
        in_specs=[kspec, kspec, vspec, kspec, vspec, sspec, const(tri), const(lev2), const(gla_norm)],
        out_specs=[vspec, sspec],
        out_shape=[jax.ShapeDtypeStruct((n_batch, t, WV_GLA), bf16),
                   jax.ShapeDtypeStruct(s0.shape, f32)],
        scratch_shapes=[pltpu.VMEM((bb, GLA_PAIRS, 2 * DK_GLA, 2 * DV_GLA), f32),
                        pltpu.VMEM((bb * GLA_PAIRS, chunk, 2 * chunk), f32)],
        compiler_params=_params(("parallel", "arbitrary")),
        name="gla",
    )(k3(qg), k3(kg), v3(vg), k3(g), v3(og), s0, tri, lev2, gla_norm)
    return o.reshape(n_batch * t, WV_GLA), s_out.reshape(n_batch, H_GLA, DK_GLA, DV_GLA)


def _mlp_kernel(a_ref, o_ref, x_ref, mix_gate_ref, sh_ref, sc_ref, gate_ref, nw_ref, wo_ref, wu_ref, wd_ref,
                y_ref, *, ff_chunk):
    m = (jnp.dot(a_ref[...], wo_ref[:W_SB, :], preferred_element_type=f32)
         + jnp.dot(o_ref[...], wo_ref[W_SB:, :], preferred_element_type=f32))
    x = x_ref[...] + mix_gate_ref[...] * m
    ms = jnp.mean(x * x, axis=-1, keepdims=True)
    y = x * lax.rsqrt(ms + EPS) * nw_ref[...]
    h = (y * (1.0 + sc_ref[...]) + sh_ref[...]).astype(bf16)
    acc = jnp.zeros(x.shape, f32)
    for c0 in range(0, D_FF, ff_chunk):
        u = jnp.dot(h, wu_ref[:, c0:c0 + ff_chunk], preferred_element_type=f32)
        u = jnp.square(jnp.maximum(u, 0.0)).astype(bf16)
        acc = acc + jnp.dot(u, wd_ref[c0:c0 + ff_chunk, :], preferred_element_type=f32)
    y_ref[...] = x + gate_ref[...] * acc


def _mlp_call(a, o, x2d, mod, lw, tm, blocks_per_batch):
    n = x2d.shape[0]
    r = mod.shape[1]
    row = lambda w: pl.BlockSpec((tm, w), lambda i: (i, 0))
    const = lambda a: pl.BlockSpec(a.shape, lambda i: (0,) * a.ndim)
    piece = lambda k: _mod_spec(r, blocks_per_batch, k)
    return pl.pallas_call(
        functools.partial(_mlp_kernel, ff_chunk=1024),
        grid=(n // tm,),
        in_specs=[row(W_SB), row(WV_GLA), row(D_MODEL), piece(2), piece(3), piece(4), piece(5),
                  const(lw["norm_mlp"]), const(lw["w_out"]), const(lw["w_up"]), const(lw["w_down"])],
        out_specs=row(D_MODEL),
        out_shape=jax.ShapeDtypeStruct((n, D_MODEL), f32),
        compiler_params=_params(("parallel",)),
        name="out_mlp",
    )(a, o, x2d, mod, mod, mod, mod, lw["norm_mlp"], lw["w_out"], lw["w_up"], lw["w_down"])


def _layer_weights(l, w_in, w_out, w_up, w_down, norm_mix, norm_mlp, q_norm, k_norm, w_gate, b_gate,
                   gla_norm):
    wi = w_in[l]
    c_g = 3 * W_SB
    c_gate = c_g + 2 * WK_GLA + WV_GLA
    w_main = jnp.concatenate(
        [wi[:, c_g:c_gate], wi[:, c_gate + GATE_RANK:], wi[:, c_gate:c_gate + GATE_RANK],
         jnp.zeros((D_MODEL, GATE_PAD - GATE_RANK), f32)], axis=1).astype(bf16)
    w_qkvt = wi[:, :c_g].T.astype(bf16)
    wg = jnp.concatenate([w_gate[l], jnp.zeros((GATE_PAD - GATE_RANK, WK_GLA), f32)], axis=0).astype(bf16)
    return {
        "w_main": w_main, "w_qkvt": w_qkvt, "w_gate": wg,
        "b_gate": b_gate[l].reshape(1, WK_GLA),
        "q_norm": jnp.tile(q_norm[l], H_SB),
        "k_norm": jnp.tile(k_norm[l], H_SB),
        "norm_mix": norm_mix[l].reshape(1, D_MODEL),
        "norm_mlp": norm_mlp[l].reshape(1, D_MODEL),
        "gla_norm": gla_norm[l].reshape(1, DV_GLA),
        "w_out": w_out[l].astype(bf16), "w_up": w_up[l].astype(bf16), "w_down": w_down[l].astype(bf16),
    }


def _group_layer(x2d, mod, lw, cw, n_batch, t, past_kt, past_vt, layer, s0, tm, blocks_per_batch, chunk,
                 gla_bb, prev_kv=()):
    q, kt, vt, ktb, vtb, qg, kg, vg, g, og = _in_call(x2d, mod, lw, tm, blocks_per_batch, prev_kv)
    if past_kt is None:
        a = _sb_call(q, ktb, vtb, ktb, vtb, cw, n_batch=n_batch, t_q=t, prompt=True, layer=layer)
    else:
        a = _sb_call(q, ktb, vtb, past_kt, past_vt, cw, n_batch=n_batch, t_q=t, prompt=False, layer=layer)
    o, s_new = _gla_call(qg, kg, vg, g, og, s0, lw["gla_norm"], n_batch, t, chunk, gla_bb)
    x2d = _mlp_call(a, o, x2d, mod, lw, tm, blocks_per_batch)
    return x2d, kt, vt, s_new


def _heads_last(kt, n_batch, t):
    layers, nb = kt.shape[:2]
    return kt.reshape(layers, nb, H_SB, D_SB, -1).transpose(0, 1, 4, 2, 3).reshape(layers, n_batch, t, H_SB, D_SB)


def kernel(x_prompt, x_sample, c_prompt, c_sample, cache_k, cache_v, state_gla, w_ada, b_ada, norm_mix,
           norm_mlp, w_in, q_norm, k_norm, w_gate, b_gate, gla_norm, w_out, w_up, w_down):
    bp, tp, _ = x_prompt.shape
    bs, ts, _ = x_sample.shape
    past = cache_k.shape[2]
    n_c = bp + bs
    c_rows = -(-n_c // 16) * 16
    c_all = jnp.concatenate([c_prompt, c_sample, jnp.zeros((c_rows - n_c, D_MODEL), f32)], axis=0)
    mods = _ada_call(c_all, w_ada, b_ada)

    kk = np.arange(KB)
    half = np.concatenate([(kk[:, None] > kk[None, :]), np.ones((KB, KB), bool)], axis=1)
    cw = jnp.asarray(np.concatenate([half, half], axis=0).astype(np.float32), dtype=bf16)

    cache_kt = cache_k.transpose(0, 1, 3, 4, 2).reshape(DEPTH, bs, W_SB, past)
    cache_vt = cache_v.transpose(0, 1, 3, 4, 2).reshape(DEPTH, bs, W_SB, past)

    tm_p, tm_s = 512, bs * ts
    xp = x_prompt.reshape(bp * tp, D_MODEL)
    xs = x_sample.reshape(bs * ts, D_MODEL)
    sp_l, ks_l, vs_l, ss_l = [], [], [], []
    prompt_kv = ()
    for l in range(DEPTH):
        lw = _layer_weights(l, w_in, w_out, w_up, w_down, norm_mix, norm_mlp, q_norm, k_norm, w_gate,
                            b_gate, gla_norm)
        mod_p = mods[l, :bp].reshape(bp, 1, N_MOD * D_MODEL)
        mod_s = jnp.repeat(mods[l, bp:n_c], ts, axis=0).reshape(1, bs * ts, N_MOD * D_MODEL)
        s0_p = jnp.zeros((bp, H_GLA, DK_GLA, DV_GLA), f32)
        xp, kp, vp, sp = _group_layer(xp, mod_p, lw, cw, bp, tp, None, None, l, s0_p, tm_p, tp // tm_p, 256, bp,
                                      prompt_kv)
        prompt_kv = (kp, vp)
        xs, kn, vn, sn = _group_layer(xs, mod_s, lw, cw, bs, ts, cache_kt, cache_vt, l,
                                      state_gla[l].astype(f32), tm_s, 1, ts, 4)
        sp_l.append(sp)
        ks_l.append(_heads_last(kn, bs, ts)[0])
        vs_l.append(_heads_last(vn, bs, ts)[0])
        ss_l.append(sn)
    return (xp.reshape(bp, tp, D_MODEL), xs.reshape(bs, ts, D_MODEL),
            _heads_last(kp, bp, tp), _heads_last(vp, bp, tp), jnp.stack(sp_l),
            jnp.stack(ks_l), jnp.stack(vs_l), jnp.stack(ss_l))
```

```python
import functools

import numpy as np
import jax
import jax.numpy as jnp
from jax import lax
from jax.experimental import pallas as pl
from jax.experimental.pallas import tpu as pltpu

D_MODEL = 1024
DEPTH = 2
H_SB = 8
D_SB = 64
W_SB = H_SB * D_SB
H_GLA = 4
DK_GLA = 64
DV_GLA = 128
WK_GLA = H_GLA * DK_GLA
WV_GLA = H_GLA * DV_GLA
GATE_RANK = 16
GATE_TAU = 16.0
D_FF = 4 * D_MODEL
N_MOD = 6
EPS = 1e-6

LANES = 128
KB = 128
SB_FIRST_PASS = 3
SB_LOOP_GROUP = 1
N_PAIR = H_SB // 2
GATE_PAD = LANES
VMEM_LIMIT = 56 * 1024 * 1024
SURVIVAL_ZERO = 104.0
MASKED = -1e30

f32 = jnp.float32
bf16 = jnp.bfloat16
NT = (((1,), (1,)), ((), ()))
TN = (((0,), (0,)), ((), ()))


def _params(sem):
    return pltpu.CompilerParams(dimension_semantics=sem, vmem_limit_bytes=VMEM_LIMIT)


def _log_sigmoid(x):
    return jnp.minimum(x, 0.0) - jnp.log1p(jnp.exp(-jnp.abs(x)))


def _ada_kernel(c_ref, w_ref, b_ref, o_ref):
    c = c_ref[...]
    s = (c * jax.nn.sigmoid(c)).astype(bf16)
    o_ref[...] = jnp.dot(s, w_ref[...].astype(bf16), preferred_element_type=f32) + b_ref[...]


def _ada_call(c_all, w_ada, b_ada):
    rows = c_all.shape[0]
    return pl.pallas_call(
        _ada_kernel,
        grid=(DEPTH, N_MOD),
        in_specs=[
            pl.BlockSpec((rows, D_MODEL), lambda l, p: (0, 0)),
            pl.BlockSpec((None, D_MODEL, D_MODEL), lambda l, p: (l, 0, p)),
            pl.BlockSpec((None, 1, D_MODEL), lambda l, p: (l, 0, p)),
        ],
        out_specs=pl.BlockSpec((None, rows, D_MODEL), lambda l, p: (l, 0, p)),
        out_shape=jax.ShapeDtypeStruct((DEPTH, rows, N_MOD * D_MODEL), f32),
        compiler_params=_params(("parallel", "parallel")),
        name="ada_mod",
    )(c_all, w_ada, b_ada.reshape(DEPTH, 1, N_MOD * D_MODEL))


def _mod_spec(rows_per_block, blocks_per_batch, piece):
    return pl.BlockSpec((None, rows_per_block, D_MODEL),
                        lambda i: (i // blocks_per_batch, 0, piece))


def _head_rms(t):
    t3 = t.reshape(H_SB, D_SB, t.shape[1])
    ms = jnp.mean(t3 * t3, axis=1, keepdims=True)
    return (t3 * lax.rsqrt(ms + EPS)).reshape(t.shape)


def _in_kernel(*refs, n_prev):
    (x_ref, sh_ref, sc_ref, nw_ref, w_ref, wqkv_ref, qn_ref, kn_ref, wg_ref, bg_ref) = refs[:10]
    prev_refs = refs[10:10 + 2 * min(n_prev, 1)]
    (q_ref, kt_ref, vt_ref, ktb_ref, vtb_ref, qg_ref, kg_ref, vg_ref, g_ref, og_ref) = refs[len(refs) - 10:]
    for prev_ref, out_ref in zip(prev_refs, (kt_ref, vt_ref)):
        out_ref[:n_prev] = prev_ref[...]
    x = x_ref[...]
    ms = jnp.mean(x * x, axis=-1, keepdims=True)
    y = x * lax.rsqrt(ms + EPS) * nw_ref[...]
    h = (y * (1.0 + sc_ref[...]) + sh_ref[...]).astype(bf16)

    def proj(lo, hi):
        return jnp.dot(h, w_ref[:, lo:hi], preferred_element_type=f32)

    qkvt = lax.dot_general(wqkv_ref[...], h, NT, preferred_element_type=f32)
    qt = _head_rms(qkvt[:W_SB]) * qn_ref[...] * (D_SB ** -0.5)
    q_ref[...] = qt.T.astype(bf16)
    kt = _head_rms(qkvt[W_SB:2 * W_SB]) * kn_ref[...]
    kt_ref[n_prev] = kt
    ktb_ref[...] = kt.astype(bf16)
    vt = qkvt[2 * W_SB:]
    vt_ref[n_prev] = vt
    vtb_ref[...] = vt.astype(bf16)

    o = 0
    qg_ref[...] = proj(o, o + WK_GLA) * (DK_GLA ** -0.5)
    o += WK_GLA
    kg_ref[...] = proj(o, o + WK_GLA)
    o += WK_GLA
    vg_ref[...] = proj(o, o + WV_GLA).astype(bf16)
    o += WV_GLA
    og_ref[...] = proj(o, o + WV_GLA)
    o += WV_GLA
    gr = proj(o, o + GATE_PAD).astype(bf16)
    xg = jnp.dot(gr, wg_ref[...], preferred_element_type=f32) + bg_ref[...]
    g_ref[...] = _log_sigmoid(xg) * (1.0 / GATE_TAU)


def _in_call(x2d, mod, lw, tm, blocks_per_batch, prev_kv):
    n = x2d.shape[0]
    r = mod.shape[1]
    n_batch = n // (tm * blocks_per_batch)
    t = tm * blocks_per_batch
    n_prev = prev_kv[0].shape[0] if prev_kv else 0
    row = lambda w: pl.BlockSpec((tm, w), lambda i: (i, 0))
    col_map = lambda i: (i // blocks_per_batch, 0, i % blocks_per_batch)
    col = pl.BlockSpec((None, W_SB, tm), col_map)
    stack = lambda depth: pl.BlockSpec((depth, None, W_SB, tm), lambda i: (0,) + col_map(i))
    const = lambda a: pl.BlockSpec(a.shape, lambda i: (0,) * a.ndim)
    qn_t = jnp.broadcast_to(lw["q_norm"].reshape(W_SB, 1), (W_SB, tm))
    kn_t = jnp.broadcast_to(lw["k_norm"].reshape(W_SB, 1), (W_SB, tm))
    rows_out = [(WK_GLA, f32), (WK_GLA, f32), (WV_GLA, bf16), (WK_GLA, f32), (WV_GLA, f32)]
    kv_stack = jax.ShapeDtypeStruct((n_prev + 1, n_batch, W_SB, t), f32)
    kv_b = jax.ShapeDtypeStruct((n_batch, W_SB, t), bf16)
    return pl.pallas_call(
        functools.partial(_in_kernel, n_prev=n_prev),
        grid=(n // tm,),
        in_specs=[row(D_MODEL), _mod_spec(r, blocks_per_batch, 0), _mod_spec(r, blocks_per_batch, 1),
                  const(lw["norm_mix"]), const(lw["w_main"]), const(lw["w_qkvt"]),
                  const(qn_t), const(kn_t), const(lw["w_gate"]), const(lw["b_gate"])]
        + [stack(n_prev)] * len(prev_kv),
        out_specs=[row(W_SB), stack(n_prev + 1), stack(n_prev + 1), col, col] + [row(w) for w, _ in rows_out],
        out_shape=[jax.ShapeDtypeStruct((n, W_SB), bf16), kv_stack, kv_stack, kv_b, kv_b]
        + [jax.ShapeDtypeStruct((n, w), dt) for w, dt in rows_out],
        compiler_params=_params(("parallel",)),
        name="in_proj",
    )(x2d, mod, mod, lw["norm_mix"], lw["w_main"], lw["w_qkvt"], qn_t, kn_t,
      lw["w_gate"], lw["b_gate"], *prev_kv)


def _pair_diag(a):
    z = jnp.zeros((D_SB, a.shape[1]), a.dtype)
    top = jnp.concatenate([a[:D_SB], z], axis=0)
    bot = jnp.concatenate([z, a[D_SB:]], axis=0)
    return jnp.concatenate([top, bot], axis=1)


def _sb_scores(q_pair, k_tiles, biases):
    tiles = []
    for kt, bias in zip(k_tiles, biases):
        z2 = jnp.dot(q_pair, _pair_diag(kt), preferred_element_type=f32)
        for hh in range(2):
            z = z2[:, hh * KB:(hh + 1) * KB] + bias
            sp = jnp.maximum(z, 0.0) + jnp.log(1.0 + jnp.exp(-jnp.abs(z)))
            hi = sp.astype(bf16)
            lo = (sp - hi.astype(f32)).astype(bf16)
            tiles.append((z - sp, jnp.concatenate([hi, lo], axis=1)))
    return tiles


def _sb_apply(tiles, cs, v_tiles, carries, acc, tq):
    new_carries = []
    ws = [[None, None] for _ in v_tiles]
    for hh in range(2):
        c = carries[hh]
        for bi in range(len(v_tiles)):
            ti = bi * 2 + hh
            part = cs[ti * tq:(ti + 1) * tq]
            between = part[:, :KB] if c is None else c + part[:, :KB]
            ws[bi][hh] = jnp.exp(tiles[ti][0] - between).astype(bf16)
            c = part[:, KB:] if c is None else c + part[:, KB:]
        new_carries.append(c)
    for bi, vt in enumerate(v_tiles):
        w_pair = jnp.concatenate(ws[bi], axis=1)
        pv = lax.dot_general(w_pair, _pair_diag(vt), NT, preferred_element_type=f32)
        acc = pv if acc is None else acc + pv
    return acc, new_carries


def _sb_kernel(q_ref, kd_ref, vd_ref, kp_ref, vp_ref, cw_ref, o_ref, acc_ref, carry_ref,
               *, tq, seg, n_past):
    b = pl.program_id(0)
    qi = pl.program_id(1)
    n_prev = qi if n_past is None else n_past
    row = lax.broadcasted_iota(jnp.int32, (tq, KB), 0)
    col = lax.broadcasted_iota(jnp.int32, (tq, KB), 1)
    seg_id = b % (KB // seg)
    seg_shift = seg.bit_length() - 1
    own = jnp.logical_and(lax.shift_right_logical(col, seg_shift) == seg_id,
                          jnp.bitwise_and(col, seg - 1) < row)
    bias0 = jnp.where(own, 0.0, MASKED)
    cw = cw_ref[...]
    pair_rows = [slice(p * KB, (p + 1) * KB) for p in range(N_PAIR)]

    def past_block(j):
        jc = pl.multiple_of(jnp.maximum(j, 0) * KB, KB)
        if n_past is None:
            kt, vt = kp_ref[b, :, pl.ds(jc, KB)], vp_ref[b, :, pl.ds(jc, KB)]
        else:
            kt, vt = kp_ref[:, pl.ds(jc, KB)], vp_ref[:, pl.ds(jc, KB)]
        return kt.astype(bf16), vt.astype(bf16), jnp.where(j < 0, MASKED, 0.0)

    def run_group(blocks, carries_of, acc_of):
        tiles = [_sb_scores(q_ref[:, rows], [kt[rows] for kt, _, _ in blocks], [bias for _, _, bias in blocks])
                 for rows in pair_rows]
        n_rows = 2 * len(blocks) * tq
        cs = jnp.dot(jnp.concatenate([t[1] for pt in tiles for t in pt], axis=0), cw,
                     preferred_element_type=f32)
        alive = None
        for p, rows in enumerate(pair_rows):
            acc, cr = _sb_apply(tiles[p], cs[p * n_rows:(p + 1) * n_rows], [vt[rows] for _, vt, _ in blocks],
                                carries_of(p), acc_of(p), tq)
            acc_ref[p] = acc
            carry_ref[2 * p] = cr[0]
            carry_ref[2 * p + 1] = cr[1]
            m = jnp.minimum(cr[0], cr[1])
            alive = m if alive is None else jnp.minimum(alive, m)
        return (jnp.min(alive) < SURVIVAL_ZERO).astype(jnp.int32)

    j0 = n_prev - 1
    first = [(kd_ref[...], vd_ref[...], bias0)] + [past_block(j0 - d) for d in range(SB_FIRST_PASS - 1)]
    alive0 = run_group(first, lambda p: (None, None), lambda p: None)

    def cond(st):
        j, alive = st
        return jnp.logical_and(j >= 0, alive > 0)

    def body(st):
        j, _ = st
        blocks = [past_block(j - d) for d in range(SB_LOOP_GROUP)]
        alive = run_group(blocks, lambda p: (carry_ref[2 * p], carry_ref[2 * p + 1]), lambda p: acc_ref[p])
        return j - SB_LOOP_GROUP, alive

    lax.while_loop(cond, body, (j0 - (SB_FIRST_PASS - 1), alive0))
    o_ref[...] = jnp.concatenate([acc_ref[p] for p in range(N_PAIR)], axis=1).astype(bf16)


def _sb_call(q, kd, vd, kp, vp, cw, *, n_batch, t_q, prompt, layer):
    tq = min(KB, t_q)
    nq = t_q // tq
    q_spec = pl.BlockSpec((tq, W_SB), lambda b, i: (b * nq + i, 0))
    if prompt:
        d_spec = pl.BlockSpec((None, W_SB, KB), lambda b, i: (b, 0, i))
        p_spec = pl.BlockSpec(kp.shape, lambda b, i: (0, 0, 0))
        seg, n_past = KB, None
    else:
        d_spec = pl.BlockSpec((None, W_SB, KB), lambda b, i: (0, 0, (b * t_q) // KB))
        p_spec = pl.BlockSpec((None, None, W_SB, kp.shape[-1]), lambda b, i: (layer, b, 0, 0))
        seg, n_past = t_q, kp.shape[-1] // KB
    return pl.pallas_call(
        functools.partial(_sb_kernel, tq=tq, seg=seg, n_past=n_past),
        grid=(n_batch, nq),
        in_specs=[q_spec, d_spec, d_spec, p_spec, p_spec, pl.BlockSpec(cw.shape, lambda b, i: (0, 0))],
        out_specs=q_spec,
        out_shape=jax.ShapeDtypeStruct((n_batch * t_q, W_SB), bf16),
        scratch_shapes=[pltpu.VMEM((N_PAIR, tq, LANES), f32), pltpu.VMEM((H_SB, tq, LANES), f32)],
        compiler_params=_params(("parallel", "arbitrary")),
        name="stick_breaking",
    )(q, kd, vd, kp, vp, cw)


GLA_PAIRS = H_GLA // 2
GLA_SAFE_DECAY = 60.0


def _gla_tables(chunk):
    idx = np.arange(chunk)
    t, s = idx[:, None], idx[None, :]
    tri = jnp.asarray((s <= t).astype(np.float32), dtype=bf16)
    level = np.where(t > s, np.floor(np.log2(np.maximum(t ^ s, 1))).astype(np.int32), -1)
    level = np.where(t < s, -2, level).astype(np.int32)
    return tri, jnp.asarray(np.concatenate([level, level], axis=1))


def _level_reference(b, h):
    L = b.shape[0]
    if h >= 4:
        return jnp.concatenate(
            [jnp.broadcast_to(b[c0 + h - 1:c0 + h, :], (2 * h, LANES)) for c0 in range(0, L, 2 * h)], axis=0)
    b3 = b.reshape(L // 8, 8, LANES)
    pos = jnp.bitwise_and(lax.broadcasted_iota(jnp.int32, b3.shape, 1), 2 * h - 1)
    up = lambda n: pltpu.roll(b3, n, 1)
    if h == 1:
        ref = jnp.where(pos == 0, b3, up(1))
    else:
        ref = jnp.where(pos == 0, up(7), jnp.where(pos == 1, b3, jnp.where(pos == 2, up(1), up(2))))
    return ref.reshape(L, LANES)


def _two_heads(a, first_head):
    return jnp.concatenate([jnp.where(first_head, a, 0.0).astype(bf16),
                            jnp.where(first_head, 0.0, a).astype(bf16)], axis=0)


def _gla_att_single(q_in, k, b, lev2, first_head):
    a = lax.dot_general(q_in, _two_heads(k * jnp.exp(-b), first_head), NT, preferred_element_type=f32)
    return jnp.where(lev2 >= -1, a, 0.0)


def _gla_att_levels(q, k, b, lev2, first_head, n_lev):
    a = lax.dot_general(q.astype(bf16), _two_heads(k, first_head), NT, preferred_element_type=f32)
    att = jnp.where(lev2 == -1, a, 0.0)
    for lev in range(n_lev):
        d = b - _level_reference(b, 1 << lev)
        e = jnp.exp(-jnp.abs(d))
        is_query = d < 0.0
        q_l = (q * jnp.where(is_query, e, 1.0)).astype(bf16)
        k_l = k * jnp.where(is_query, 1.0, e)
        a = lax.dot_general(q_l, _two_heads(k_l, first_head), NT, preferred_element_type=f32)
        att = jnp.where(lev2 == lev, a, att)
    return att


def _gla_kernel(q_ref, k_ref, v_ref, g_ref, og_ref, s0_ref, tri_ref, lev_ref, gn_ref,
                o_ref, sout_ref, s_ref, att_ref, *, chunk, bb):
    c = pl.program_id(1)
    L = chunk
    n_lev = L.bit_length() - 1
    chains = [(bi, p) for bi in range(bb) for p in range(GLA_PAIRS)]
    zeros_s = jnp.zeros((DK_GLA, DV_GLA), f32)

    @pl.when(c == 0)
    def _():
        for bi, p in chains:
            s0 = s0_ref[bi, p]
            s_ref[bi, p] = jnp.concatenate(
                [jnp.concatenate([s0[:DK_GLA], zeros_s], axis=1),
                 jnp.concatenate([zeros_s, s0[DK_GLA:]], axis=1)], axis=0)

    lane = lax.broadcasted_iota(jnp.int32, (L, LANES), 1)
    first_head = lane < DK_GLA
    lev2 = lev_ref[...]
    tri = tri_ref[...]

    pre = []
    floor = None
    for bi, p in chains:
        lanes = slice(p * LANES, (p + 1) * LANES)
        q = q_ref[bi, :, lanes]
        k = k_ref[bi, :, lanes]
        g = g_ref[bi, :, lanes]
        g_hi = g.astype(bf16)
        g_lo = (g - g_hi.astype(f32)).astype(bf16)
        g2 = jnp.concatenate([g_hi, g_lo], axis=1)
        bs = jnp.dot(tri, g2, preferred_element_type=f32)
        b = bs[:, :LANES] + bs[:, LANES:]
        b_last = b[L - 1:L, :]
        q_in = (q * jnp.exp(b)).astype(bf16)
        k_end = (k * jnp.exp(b_last - b)).astype(bf16)
        pre.append((q, k, g2, b, q_in, k_end))
        floor = b_last if floor is None else jnp.minimum(floor, b_last)
    single = jnp.min(floor) >= -GLA_SAFE_DECAY

    @pl.when(single)
    def _():
        for ci, (q, k, g2, b, q_in, k_end) in enumerate(pre):
            att_ref[ci] = _gla_att_single(q_in, k, b, lev2, first_head)

    @pl.when(jnp.logical_not(single))
    def _():
        for ci, (q, k, g2, b, q_in, k_end) in enumerate(pre):
            att_ref[ci] = _gla_att_levels(q, k, b, lev2, first_head, n_lev)

    rowi = lax.broadcasted_iota(jnp.int32, (2 * DK_GLA, 2 * DV_GLA), 0)
    coli = lax.broadcasted_iota(jnp.int32, (2 * DK_GLA, 2 * DV_GLA), 1)
    own_block = (rowi < DK_GLA) == (coli < DV_GLA)
    ones = jnp.ones((L, LANES), bf16)
    zeros_v = jnp.zeros((L, DV_GLA), bf16)
    gn = gn_ref[...]
    for ci, (bi, p) in enumerate(chains):
        q, k, g2, b, q_in, k_end = pre[ci]
        cols = slice(p * 2 * DV_GLA, (p + 1) * 2 * DV_GLA)
        v = v_ref[bi, :, cols].astype(bf16)
        v_diag = jnp.concatenate([jnp.concatenate([v[:, :DV_GLA], zeros_v], axis=1),
                                  jnp.concatenate([zeros_v, v[:, DV_GLA:]], axis=1)], axis=0)
        s_old = s_ref[bi, p]
        o2 = (jnp.dot(att_ref[ci].astype(bf16), v_diag, preferred_element_type=f32)
              + jnp.dot(q_in, s_old.astype(bf16), preferred_element_type=f32))
        tot = lax.dot_general(g2, ones, TN, preferred_element_type=f32)
        decay = jnp.exp(tot[:LANES] + tot[LANES:])
        upd = lax.dot_general(k_end, v, TN, preferred_element_type=f32)
        s_ref[bi, p] = jnp.concatenate([decay, decay], axis=1) * s_old + jnp.where(own_block, upd, 0.0)
        og = og_ref[bi, :, cols]
        outs = []
        for hh in range(2):
            o = o2[:, hh * DV_GLA:(hh + 1) * DV_GLA]
            ms = jnp.mean(o * o, axis=-1, keepdims=True)
            gate = og[:, hh * DV_GLA:(hh + 1) * DV_GLA]
            outs.append(o * lax.rsqrt(ms + EPS) * gn * (gate * jax.nn.sigmoid(gate)))
        o_ref[bi, :, cols] = jnp.concatenate(outs, axis=1).astype(bf16)

    @pl.when(c == pl.num_programs(1) - 1)
    def _():
        for bi, p in chains:
            s = s_ref[bi, p]
            sout_ref[bi, p] = jnp.concatenate([s[:DK_GLA, :DV_GLA], s[DK_GLA:, DV_GLA:]], axis=0)


def _gla_call(qg, kg, vg, g, og, s0, gla_norm, n_batch, t, chunk, bb):
    nc = t // chunk
    tri, lev2 = _gla_tables(chunk)
    kspec = pl.BlockSpec((bb, chunk, WK_GLA), lambda i, c: (i, c, 0))
    vspec = pl.BlockSpec((bb, chunk, WV_GLA), lambda i, c: (i, c, 0))
    sspec = pl.BlockSpec((bb, GLA_PAIRS, 2 * DK_GLA, DV_GLA), lambda i, c: (i, 0, 0, 0))
    const = lambda a: pl.BlockSpec(a.shape, lambda i, c: (0,) * a.ndim)
    s0 = s0.reshape(n_batch, GLA_PAIRS, 2 * DK_GLA, DV_GLA)
    k3 = lambda a: a.reshape(n_batch, t, WK_GLA)
    v3 = lambda a: a.reshape(n_batch, t, WV_GLA)
    o, s_out = pl.pallas_call(
        functools.partial(_gla_kernel, chunk=chunk, bb=bb),
        grid=(n_batch // bb, nc),
        in_specs=[kspec, kspec, vspec, kspec, vspec, sspec, const(tri), const(lev2), const(gla_norm)],
        out_specs=[vspec, sspec],
        out_shape=[jax.ShapeDtypeStruct((n_batch, t, WV_GLA), bf16),
                   jax.ShapeDtypeStruct(s0.shape, f32)],
        scratch_shapes=[pltpu.VMEM((bb, GLA_PAIRS, 2 * DK_GLA, 2 * DV_GLA), f32),
                        pltpu.VMEM((bb * GLA_PAIRS, chunk, 2 * chunk), f32)],
        compiler_params=_params(("parallel", "arbitrary")),
        name="gla",
    )(k3(qg), k3(kg), v3(vg), k3(g), v3(og), s0, tri, lev2, gla_norm)
    return o.reshape(n_batch * t, WV_GLA), s_out.reshape(n_batch, H_GLA, DK_GLA, DV_GLA)


def _mlp_kernel(a_ref, o_ref, x_ref, mix_gate_ref, sh_ref, sc_ref, gate_ref, nw_ref, wo_ref, wu_ref, wd_ref,
                y_ref, *, ff_chunk):
    m = (jnp.dot(a_ref[...], wo_ref[:W_SB, :], preferred_element_type=f32)
         + jnp.dot(o_ref[...], wo_ref[W_SB:, :], preferred_element_type=f32))
    x = x_ref[...] + mix_gate_ref[...] * m
    ms = jnp.mean(x * x, axis=-1, keepdims=True)
    y = x * lax.rsqrt(ms + EPS) * nw_ref[...]
    h = (y * (1.0 + sc_ref[...]) + sh_ref[...]).astype(bf16)
    acc = jnp.zeros(x.shape, f32)
    for c0 in range(0, D_FF, ff_chunk):
        u = jnp.dot(h, wu_ref[:, c0:c0 + ff_chunk], preferred_element_type=f32)
        u = jnp.square(jnp.maximum(u, 0.0)).astype(bf16)
        acc = acc + jnp.dot(u, wd_ref[c0:c0 + ff_chunk, :], preferred_element_type=f32)
    y_ref[...] = x + gate_ref[...] * acc


def _mlp_call(a, o, x2d, mod, lw, tm, blocks_per_batch):
    n = x2d.shape[0]
    r = mod.shape[1]
    row = lambda w: pl.BlockSpec((tm, w), lambda i: (i, 0))
    const = lambda a: pl.BlockSpec(a.shape, lambda i: (0,) * a.ndim)
    piece = lambda k: _mod_spec(r, blocks_per_batch, k)
    return pl.pallas_call(
        functools.partial(_mlp_kernel, ff_chunk=1024),
        grid=(n // tm,),
        in_specs=[row(W_SB), row(WV_GLA), row(D_MODEL), piece(2), piece(3), piece(4), piece(5),
                  const(lw["norm_mlp"]), const(lw["w_out"]), const(lw["w_up"]), const(lw["w_down"])],
        out_specs=row(D_MODEL),
        out_shape=jax.ShapeDtypeStruct((n, D_MODEL), f32),
        compiler_params=_params(("parallel",)),
        name="out_mlp",
    )(a, o, x2d, mod, mod, mod, mod, lw["norm_mlp"], lw["w_out"], lw["w_up"], lw["w_down"])


def _layer_weights(l, w_in, w_out, w_up, w_down, norm_mix, norm_mlp, q_norm, k_norm, w_gate, b_gate,
                   gla_norm):
    wi = w_in[l]
    c_g = 3 * W_SB
    c_gate = c_g + 2 * WK_GLA + WV_GLA
    w_main = jnp.concatenate(
        [wi[:, c_g:c_gate], wi[:, c_gate + GATE_RANK:], wi[:, c_gate:c_gate + GATE_RANK],
         jnp.zeros((D_MODEL, GATE_PAD - GATE_RANK), f32)], axis=1).astype(bf16)
    w_qkvt = wi[:, :c_g].T.astype(bf16)
    wg = jnp.concatenate([w_gate[l], jnp.zeros((GATE_PAD - GATE_RANK, WK_GLA), f32)], axis=0).astype(bf16)
    return {
        "w_main": w_main, "w_qkvt": w_qkvt, "w_gate": wg,
        "b_gate": b_gate[l].reshape(1, WK_GLA),
        "q_norm": jnp.tile(q_norm[l], H_SB),
        "k_norm": jnp.tile(k_norm[l], H_SB),
        "norm_mix": norm_mix[l].reshape(1, D_MODEL),
        "norm_mlp": norm_mlp[l].reshape(1, D_MODEL),
        "gla_norm": gla_norm[l].reshape(1, DV_GLA),
        "w_out": w_out[l].astype(bf16), "w_up": w_up[l].astype(bf16), "w_down": w_down[l].astype(bf16),
    }


def _group_layer(x2d, mod, lw, cw, n_batch, t, past_kt, past_vt, layer, s0, tm, blocks_per_batch, chunk,
                 gla_bb, prev_kv=()):
    q, kt, vt, ktb, vtb, qg, kg, vg, g, og = _in_call(x2d, mod, lw, tm, blocks_per_batch, prev_kv)
    if past_kt is None:
        a = _sb_call(q, ktb, vtb, ktb, vtb, cw, n_batch=n_batch, t_q=t, prompt=True, layer=layer)
    else:
        a = _sb_call(q, ktb, vtb, past_kt, past_vt, cw, n_batch=n_batch, t_q=t, prompt=False, layer=layer)
    o, s_new = _gla_call(qg, kg, vg, g, og, s0, lw["gla_norm"], n_batch, t, chunk, gla_bb)
    x2d = _mlp_call(a, o, x2d, mod, lw, tm, blocks_per_batch)
    return x2d, kt, vt, s_new


def _heads_last(kt, n_batch, t):
    layers, nb = kt.shape[:2]
    return kt.reshape(layers, nb, H_SB, D_SB, -1).transpose(0, 1, 4, 2, 3).reshape(layers, n_batch, t, H_SB, D_SB)


def kernel(x_prompt, x_sample, c_prompt, c_sample, cache_k, cache_v, state_gla, w_ada, b_ada, norm_mix,
           norm_mlp, w_in, q_norm, k_norm, w_gate, b_gate, gla_norm, w_out, w_up, w_down):
    bp, tp, _ = x_prompt.shape
    bs, ts, _ = x_sample.shape
    past = cache_k.shape[2]
    n_c = bp + bs
    c_rows = -(-n_c // 16) * 16
    c_all = jnp.concatenate([c_prompt, c_sample, jnp.zeros((c_rows - n_c, D_MODEL), f32)], axis=0)
    mods = _ada_call(c_all, w_ada, b_ada)

    kk = np.arange(KB)
    half = np.concatenate([(kk[:, None] > kk[None, :]), np.ones((KB, KB), bool)], axis=1)
    cw = jnp.asarray(np.concatenate([half, half], axis=0).astype(np.float32), dtype=bf16)

    cache_kt = cache_k.transpose(0, 1, 3, 4, 2).reshape(DEPTH, bs, W_SB, past)
    cache_vt = cache_v.transpose(0, 1, 3, 4, 2).reshape(DEPTH, bs, W_SB, past)

    tm_p, tm_s = 512, bs * ts
    xp = x_prompt.reshape(bp * tp, D_MODEL)
    xs = x_sample.reshape(bs * ts, D_MODEL)
    sp_l, ks_l, vs_l, ss_l = [], [], [], []
    prompt_kv = ()
    for l in range(DEPTH):
        lw = _layer_weights(l, w_in, w_out, w_up, w_down, norm_mix, norm_mlp, q_norm, k_norm, w_gate,
                            b_gate, gla_norm)
        mod_p = mods[l, :bp].reshape(bp, 1, N_MOD * D_MODEL)
        mod_s = jnp.repeat(mods[l, bp:n_c], ts, axis=0).reshape(1, bs * ts, N_MOD * D_MODEL)
        s0_p = jnp.zeros((bp, H_GLA, DK_GLA, DV_GLA), f32)
        xp, kp, vp, sp = _group_layer(xp, mod_p, lw, cw, bp, tp, None, None, l, s0_p, tm_p, tp // tm_p, 256, bp,
                                      prompt_kv)
        prompt_kv = (kp, vp)
        xs, kn, vn, sn = _group_layer(xs, mod_s, lw, cw, bs, ts, cache_kt, cache_vt, l,
                                      state_gla[l].astype(f32), tm_s, 1, ts, 4)
        sp_l.append(sp)
        ks_l.append(_heads_last(kn, bs, ts)[0])
        vs_l.append(_heads_last(vn, bs, ts)[0])
        ss_l.append(sn)
    return (xp.reshape(bp, tp, D_MODEL), xs.reshape(bs, ts, D_MODEL),
            _heads_last(kp, bp, tp), _heads_last(vp, bp, tp), jnp.stack(sp_l),
            jnp.stack(ks_l), jnp.stack(vs_l), jnp.stack(ss_l))
```

```python
import functools

import numpy as np
import jax
import jax.numpy as jnp
from jax import lax
from jax.experimental import pallas as pl
from jax.experimental.pallas import tpu as pltpu

D_MODEL = 1024
DEPTH = 2
H_SB = 8
D_SB = 64
W_SB = H_SB * D_SB
H_GLA = 4
DK_GLA = 64
DV_GLA = 128
WK_GLA = H_GLA * DK_GLA
WV_GLA = H_GLA * DV_GLA
GATE_RANK = 16
GATE_TAU = 16.0
D_FF = 4 * D_MODEL
N_MOD = 6
EPS = 1e-6

LANES = 128
KB = 128
SB_FIRST_PASS = 3
SB_LOOP_GROUP = 1
N_PAIR = H_SB // 2
GATE_PAD = LANES
VMEM_LIMIT = 56 * 1024 * 1024
SURVIVAL_ZERO = 104.0
MASKED = -1e30

f32 = jnp.float32
bf16 = jnp.bfloat16
NT = (((1,), (1,)), ((), ()))
TN = (((0,), (0,)), ((), ()))


def _params(sem):
    return pltpu.CompilerParams(dimension_semantics=sem, vmem_limit_bytes=VMEM_LIMIT)


def _log_sigmoid(x):
    return jnp.minimum(x, 0.0) - jnp.log1p(jnp.exp(-jnp.abs(x)))


def _ada_kernel(c_ref, w_ref, b_ref, o_ref):
    c = c_ref[...]
    s = (c * jax.nn.sigmoid(c)).astype(bf16)
    o_ref[...] = jnp.dot(s, w_ref[...].astype(bf16), preferred_element_type=f32) + b_ref[...]


def _ada_call(c_all, w_ada, b_ada):
    rows = c_all.shape[0]
    return pl.pallas_call(
        _ada_kernel,
        grid=(DEPTH, N_MOD),
        in_specs=[
            pl.BlockSpec((rows, D_MODEL), lambda l, p: (0, 0)),
            pl.BlockSpec((None, D_MODEL, D_MODEL), lambda l, p: (l, 0, p)),
            pl.BlockSpec((None, 1, D_MODEL), lambda l, p: (l, 0, p)),
        ],
        out_specs=pl.BlockSpec((None, rows, D_MODEL), lambda l, p: (l, 0, p)),
        out_shape=jax.ShapeDtypeStruct((DEPTH, rows, N_MOD * D_MODEL), f32),
        compiler_params=_params(("parallel", "parallel")),
        name="ada_mod",
    )(c_all, w_ada, b_ada.reshape(DEPTH, 1, N_MOD * D_MODEL))


def _mod_spec(rows_per_block, blocks_per_batch, piece):
    return pl.BlockSpec((None, rows_per_block, D_MODEL),
                        lambda i: (i // blocks_per_batch, 0, piece))


def _head_rms(t):
    t3 = t.reshape(H_SB, D_SB, t.shape[1])
    ms = jnp.mean(t3 * t3, axis=1, keepdims=True)
    return (t3 * lax.rsqrt(ms + EPS)).reshape(t.shape)


def _in_kernel(*refs, n_prev):
    (x_ref, sh_ref, sc_ref, nw_ref, w_ref, wqkv_ref, qn_ref, kn_ref, wg_ref, bg_ref) = refs[:10]
    prev_refs = refs[10:10 + 2 * min(n_prev, 1)]
    (q_ref, kt_ref, vt_ref, ktb_ref, vtb_ref, qg_ref, kg_ref, vg_ref, g_ref, og_ref) = refs[len(refs) - 10:]
    for prev_ref, out_ref in zip(prev_refs, (kt_ref, vt_ref)):
        out_ref[:n_prev] = prev_ref[...]
    x = x_ref[...]
    ms = jnp.mean(x * x, axis=-1, keepdims=True)
    y = x * lax.rsqrt(ms + EPS) * nw_ref[...]
    h = (y * (1.0 + sc_ref[...]) + sh_ref[...]).astype(bf16)

    def proj(lo, hi):
        return jnp.dot(h, w_ref[:, lo:hi], preferred_element_type=f32)

    qkvt = lax.dot_general(wqkv_ref[...], h, NT, preferred_element_type=f32)
    qt = _head_rms(qkvt[:W_SB]) * qn_ref[...] * (D_SB ** -0.5)
    q_ref[...] = qt.T.astype(bf16)
    kt = _head_rms(qkvt[W_SB:2 * W_SB]) * kn_ref[...]
    kt_ref[n_prev] = kt
    ktb_ref[...] = kt.astype(bf16)
    vt = qkvt[2 * W_SB:]
    vt_ref[n_prev] = vt
    vtb_ref[...] = vt.astype(bf16)

    o = 0
    qg_ref[...] = proj(o, o + WK_GLA) * (DK_GLA ** -0.5)
    o += WK_GLA
    kg_ref[...] = proj(o, o + WK_GLA)
    o += WK_GLA
    vg_ref[...] = proj(o, o + WV_GLA).astype(bf16)
    o += WV_GLA
    og_ref[...] = proj(o, o + WV_GLA)
    o += WV_GLA
    gr = proj(o, o + GATE_PAD).astype(bf16)
    xg = jnp.dot(gr, wg_ref[...], preferred_element_type=f32) + bg_ref[...]
    g_ref[...] = _log_sigmoid(xg) * (1.0 / GATE_TAU)


def _in_call(x2d, mod, lw, tm, blocks_per_batch, prev_kv):
    n = x2d.shape[0]
    r = mod.shape[1]
    n_batch = n // (tm * blocks_per_batch)
    t = tm * blocks_per_batch
    n_prev = prev_kv[0].shape[0] if prev_kv else 0
    row = lambda w: pl.BlockSpec((tm, w), lambda i: (i, 0))
    col_map = lambda i: (i // blocks_per_batch, 0, i % blocks_per_batch)
    col = pl.BlockSpec((None, W_SB, tm), col_map)
    stack = lambda depth: pl.BlockSpec((depth, None, W_SB, tm), lambda i: (0,) + col_map(i))
    const = lambda a: pl.BlockSpec(a.shape, lambda i: (0,) * a.ndim)
    qn_t = jnp.broadcast_to(lw["q_norm"].reshape(W_SB, 1), (W_SB, tm))
    kn_t = jnp.broadcast_to(lw["k_norm"].reshape(W_SB, 1), (W_SB, tm))
    rows_out = [(WK_GLA, f32), (WK_GLA, f32), (WV_GLA, bf16), (WK_GLA, f32), (WV_GLA, f32)]
    kv_stack = jax.ShapeDtypeStruct((n_prev + 1, n_batch, W_SB, t), f32)
    kv_b = jax.ShapeDtypeStruct((n_batch, W_SB, t), bf16)
    return pl.pallas_call(
        functools.partial(_in_kernel, n_prev=n_prev),
        grid=(n // tm,),
        in_specs=[row(D_MODEL), _mod_spec(r, blocks_per_batch, 0), _mod_spec(r, blocks_per_batch, 1),
                  const(lw["norm_mix"]), const(lw["w_main"]), const(lw["w_qkvt"]),
                  const(qn_t), const(kn_t), const(lw["w_gate"]), const(lw["b_gate"])]
        + [stack(n_prev)] * len(prev_kv),
        out_specs=[row(W_SB), stack(n_prev + 1), stack(n_prev + 1), col, col] + [row(w) for w, _ in rows_out],
        out_shape=[jax.ShapeDtypeStruct((n, W_SB), bf16), kv_stack, kv_stack, kv_b, kv_b]
        + [jax.ShapeDtypeStruct((n, w), dt) for w, dt in rows_out],
        compiler_params=_params(("parallel",)),
        name="in_proj",
    )(x2d, mod, mod, lw["norm_mix"], lw["w_main"], lw["w_qkvt"], qn_t, kn_t,
      lw["w_gate"], lw["b_gate"], *prev_kv)


def _pair_diag(a):
    z = jnp.zeros((D_SB, a.shape[1]), a.dtype)
    top = jnp.concatenate([a[:D_SB], z], axis=0)
    bot = jnp.concatenate([z, a[D_SB:]], axis=0)
    return jnp.concatenate([top, bot], axis=1)


def _sb_scores(q_pair, k_tiles, biases):
    tiles = []
    for kt, bias in zip(k_tiles, biases):
        z2 = jnp.dot(q_pair, _pair_diag(kt), preferred_element_type=f32)
        for hh in range(2):
            z = z2[:, hh * KB:(hh + 1) * KB] + bias
            sp = jnp.maximum(z, 0.0) + jnp.log(1.0 + jnp.exp(-jnp.abs(z)))
            hi = sp.astype(bf16)
            lo = (sp - hi.astype(f32)).astype(bf16)
            tiles.append((z - sp, jnp.concatenate([hi, lo], axis=1)))
    return tiles


def _sb_apply(tiles, cs, v_tiles, carries, acc, tq):
    new_carries = []
    ws = [[None, None] for _ in v_tiles]
    for hh in range(2):
        c = carries[hh]
        for bi in range(len(v_tiles)):
            ti = bi * 2 + hh
            part = cs[ti * tq:(ti + 1) * tq]
            between = part[:, :KB] if c is None else c + part[:, :KB]
            ws[bi][hh] = jnp.exp(tiles[ti][0] - between).astype(bf16)
            c = part[:, KB:] if c is None else c + part[:, KB:]
        new_carries.append(c)
    for bi, vt in enumerate(v_tiles):
        w_pair = jnp.concatenate(ws[bi], axis=1)
        pv = lax.dot_general(w_pair, _pair_diag(vt), NT, preferred_element_type=f32)
        acc = pv if acc is None else acc + pv
    return acc, new_carries


def _sb_kernel(q_ref, kd_ref, vd_ref, kp_ref, vp_ref, cw_ref, *rest, tq, seg, n_past, layer):
    if n_past is None:
        o_ref, acc_ref, carry_ref = rest
    else:
        kfar_ref, vfar_ref, o_ref, acc_ref, carry_ref, kbuf_ref, vbuf_ref, sem = rest
    b = pl.program_id(0)
    qi = pl.program_id(1)
    n_prev = qi if n_past is None else n_past
    row = lax.broadcasted_iota(jnp.int32, (tq, KB), 0)
    col = lax.broadcasted_iota(jnp.int32, (tq, KB), 1)
    seg_id = b % (KB // seg)
    seg_shift = seg.bit_length() - 1
    own = jnp.logical_and(lax.shift_right_logical(col, seg_shift) == seg_id,
                          jnp.bitwise_and(col, seg - 1) < row)
    bias0 = jnp.where(own, 0.0, MASKED)
    cw = cw_ref[...]
    pair_rows = [slice(p * KB, (p + 1) * KB) for p in range(N_PAIR)]

    def past_block(j, first_pass):
        if n_past is not None and first_pass:
            c0 = (j - (n_past - (SB_FIRST_PASS - 1))) * KB
            return kp_ref[:, c0:c0 + KB].astype(bf16), vp_ref[:, c0:c0 + KB].astype(bf16), 0.0
        jc = pl.multiple_of(jnp.maximum(j, 0) * KB, KB)
        if n_past is None:
            kt, vt = kp_ref[b, :, pl.ds(jc, KB)], vp_ref[b, :, pl.ds(jc, KB)]
        else:
            copies = [pltpu.make_async_copy(src.at[layer, b, :, pl.ds(jc, KB)], dst, sem.at[i])
                      for i, (src, dst) in enumerate(((kfar_ref, kbuf_ref), (vfar_ref, vbuf_ref)))]
            for cp in copies:
                cp.start()
            for cp in copies:
                cp.wait()
            kt, vt = kbuf_ref[...], vbuf_ref[...]
        return kt.astype(bf16), vt.astype(bf16), jnp.where(j < 0, MASKED, 0.0)

    def run_group(blocks, carries_of, acc_of):
        tiles = [_sb_scores(q_ref[:, rows], [kt[rows] for kt, _, _ in blocks], [bias for _, _, bias in blocks])
                 for rows in pair_rows]
        n_rows = 2 * len(blocks) * tq
        cs = jnp.dot(jnp.concatenate([t[1] for pt in tiles for t in pt], axis=0), cw,
                     preferred_element_type=f32)
        alive = None
        for p, rows in enumerate(pair_rows):
            acc, cr = _sb_apply(tiles[p], cs[p * n_rows:(p + 1) * n_rows], [vt[rows] for _, vt, _ in blocks],
                                carries_of(p), acc_of(p), tq)
            acc_ref[p] = acc
            carry_ref[2 * p] = cr[0]
            carry_ref[2 * p + 1] = cr[1]
            m = jnp.minimum(cr[0], cr[1])
            alive = m if alive is None else jnp.minimum(alive, m)
        return (jnp.min(alive) < SURVIVAL_ZERO).astype(jnp.int32)

    j0 = n_prev - 1
    first = [(kd_ref[...], vd_ref[...], bias0)] + [past_block(j0 - d, True) for d in range(SB_FIRST_PASS - 1)]
    alive0 = run_group(first, lambda p: (None, None), lambda p: None)

    def cond(st):
        j, alive = st
        return jnp.logical_and(j >= 0, alive > 0)

    def body(st):
        j, _ = st
        blocks = [past_block(j - d, False) for d in range(SB_LOOP_GROUP)]
        alive = run_group(blocks, lambda p: (carry_ref[2 * p], carry_ref[2 * p + 1]), lambda p: acc_ref[p])
        return j - SB_LOOP_GROUP, alive

    lax.while_loop(cond, body, (j0 - (SB_FIRST_PASS - 1), alive0))
    o_ref[...] = jnp.concatenate([acc_ref[p] for p in range(N_PAIR)], axis=1).astype(bf16)


def _sb_call(q, kd, vd, kp, vp, cw, *, n_batch, t_q, prompt, layer):
    tq = min(KB, t_q)
    nq = t_q // tq
    q_spec = pl.BlockSpec((tq, W_SB), lambda b, i: (b * nq + i, 0))
    if prompt:
        d_spec = pl.BlockSpec((None, W_SB, KB), lambda b, i: (b, 0, i))
        p_spec = pl.BlockSpec(kp.shape, lambda b, i: (0, 0, 0))
        seg, n_past = KB, None
        far_specs, far_args, far_scratch = [], (), []
    else:
        past = kp.shape[-1]
        tail = (SB_FIRST_PASS - 1) * KB
        assert past % tail == 0 and SB_LOOP_GROUP == 1
        d_spec = pl.BlockSpec((None, W_SB, KB), lambda b, i: (0, 0, (b * t_q) // KB))
        p_spec = pl.BlockSpec((None, None, W_SB, tail), lambda b, i: (layer, b, 0, past // tail - 1))
        seg, n_past = t_q, past // KB
        far_specs, far_args = [pl.BlockSpec(memory_space=pl.ANY)] * 2, (kp, vp)
        far_scratch = [pltpu.VMEM((W_SB, KB), kp.dtype), pltpu.VMEM((W_SB, KB), vp.dtype),
                       pltpu.SemaphoreType.DMA((2,))]
    return pl.pallas_call(
        functools.partial(_sb_kernel, tq=tq, seg=seg, n_past=n_past, layer=layer),
        grid=(n_batch, nq),
        in_specs=[q_spec, d_spec, d_spec, p_spec, p_spec, pl.BlockSpec(cw.shape, lambda b, i: (0, 0))] + far_specs,
        out_specs=q_spec,
        out_shape=jax.ShapeDtypeStruct((n_batch * t_q, W_SB), bf16),
        scratch_shapes=[pltpu.VMEM((N_PAIR, tq, LANES), f32), pltpu.VMEM((H_SB, tq, LANES), f32)] + far_scratch,
        compiler_params=_params(("parallel", "arbitrary")),
        name="stick_breaking",
    )(q, kd, vd, kp, vp, cw, *far_args)


GLA_PAIRS = H_GLA // 2
GLA_SAFE_DECAY = 60.0


def _gla_tables(chunk):
    idx = np.arange(chunk)
    t, s = idx[:, None], idx[None, :]
    tri = jnp.asarray((s <= t).astype(np.float32), dtype=bf16)
    level = np.where(t > s, np.floor(np.log2(np.maximum(t ^ s, 1))).astype(np.int32), -1)
    level = np.where(t < s, -2, level).astype(np.int32)
    return tri, jnp.asarray(np.concatenate([level, level], axis=1))


def _level_reference(b, h):
    L = b.shape[0]
    if h >= 4:
        return jnp.concatenate(
            [jnp.broadcast_to(b[c0 + h - 1:c0 + h, :], (2 * h, LANES)) for c0 in range(0, L, 2 * h)], axis=0)
    b3 = b.reshape(L // 8, 8, LANES)
    pos = jnp.bitwise_and(lax.broadcasted_iota(jnp.int32, b3.shape, 1), 2 * h - 1)
    up = lambda n: pltpu.roll(b3, n, 1)
    if h == 1:
        ref = jnp.where(pos == 0, b3, up(1))
    else:
        ref = jnp.where(pos == 0, up(7), jnp.where(pos == 1, b3, jnp.where(pos == 2, up(1), up(2))))
    return ref.reshape(L, LANES)


def _two_heads(a, first_head):
    return jnp.concatenate([jnp.where(first_head, a, 0.0).astype(bf16),
                            jnp.where(first_head, 0.0, a).astype(bf16)], axis=0)


def _gla_att_single(q_in, k, b, lev2, first_head):
    a = lax.dot_general(q_in, _two_heads(k * jnp.exp(-b), first_head), NT, preferred_element_type=f32)
    return jnp.where(lev2 >= -1, a, 0.0)


def _gla_att_levels(q, k, b, lev2, first_head, n_lev):
    a = lax.dot_general(q.astype(bf16), _two_heads(k, first_head), NT, preferred_element_type=f32)
    att = jnp.where(lev2 == -1, a, 0.0)
    for lev in range(n_lev):
        d = b - _level_reference(b, 1 << lev)
        e = jnp.exp(-jnp.abs(d))
        is_query = d < 0.0
        q_l = (q * jnp.where(is_query, e, 1.0)).astype(bf16)
        k_l = k * jnp.where(is_query, 1.0, e)
        a = lax.dot_general(q_l, _two_heads(k_l, first_head), NT, preferred_element_type=f32)
        att = jnp.where(lev2 == lev, a, att)
    return att


def _gla_kernel(q_ref, k_ref, v_ref, g_ref, og_ref, s0_ref, tri_ref, lev_ref, gn_ref,
                o_ref, sout_ref, s_ref, att_ref, *, chunk, bb):
    c = pl.program_id(1)
    L = chunk
    n_lev = L.bit_length() - 1
    chains = [(bi, p) for bi in range(bb) for p in range(GLA_PAIRS)]
    zeros_s = jnp.zeros((DK_GLA, DV_GLA), f32)

    @pl.when(c == 0)
    def _():
        for bi, p in chains:
            s0 = s0_ref[bi, p]
            s_ref[bi, p] = jnp.concatenate(
                [jnp.concatenate([s0[:DK_GLA], zeros_s], axis=1),
                 jnp.concatenate([zeros_s, s0[DK_GLA:]], axis=1)], axis=0)

    lane = lax.broadcasted_iota(jnp.int32, (L, LANES), 1)
    first_head = lane < DK_GLA
    lev2 = lev_ref[...]
    tri = tri_ref[...]

    pre = []
    floor = None
    for bi, p in chains:
        lanes = slice(p * LANES, (p + 1) * LANES)
        q = q_ref[bi, :, lanes]
        k = k_ref[bi, :, lanes]
        g = g_ref[bi, :, lanes]
        g_hi = g.astype(bf16)
        g_lo = (g - g_hi.astype(f32)).astype(bf16)
        g2 = jnp.concatenate([g_hi, g_lo], axis=1)
        bs = jnp.dot(tri, g2, preferred_element_type=f32)
        b = bs[:, :LANES] + bs[:, LANES:]
        b_last = b[L - 1:L, :]
        q_in = (q * jnp.exp(b)).astype(bf16)
        k_end = (k * jnp.exp(b_last - b)).astype(bf16)
        pre.append((q, k, g2, b, q_in, k_end))
        floor = b_last if floor is None else jnp.minimum(floor, b_last)
    single = jnp.min(floor) >= -GLA_SAFE_DECAY

    @pl.when(single)
    def _():
        for ci, (q, k, g2, b, q_in, k_end) in enumerate(pre):
            att_ref[ci] = _gla_att_single(q_in, k, b, lev2, first_head)

    @pl.when(jnp.logical_not(single))
    def _():
        for ci, (q, k, g2, b, q_in, k_end) in enumerate(pre):
            att_ref[ci] = _gla_att_levels(q, k, b, lev2, first_head, n_lev)

    rowi = lax.broadcasted_iota(jnp.int32, (2 * DK_GLA, 2 * DV_GLA), 0)
    coli = lax.broadcasted_iota(jnp.int32, (2 * DK_GLA, 2 * DV_GLA), 1)
    own_block = (rowi < DK_GLA) == (coli < DV_GLA)
    ones = jnp.ones((L, LANES), bf16)
    zeros_v = jnp.zeros((L, DV_GLA), bf16)
    gn = gn_ref[...]
    for ci, (bi, p) in enumerate(chains):
        q, k, g2, b, q_in, k_end = pre[ci]
        cols = slice(p * 2 * DV_GLA, (p + 1) * 2 * DV_GLA)
        v = v_ref[bi, :, cols].astype(bf16)
        v_diag = jnp.concatenate([jnp.concatenate([v[:, :DV_GLA], zeros_v], axis=1),
                                  jnp.concatenate([zeros_v, v[:, DV_GLA:]], axis=1)], axis=0)
        s_old = s_ref[bi, p]
        o2 = (jnp.dot(att_ref[ci].astype(bf16), v_diag, preferred_element_type=f32)
              + jnp.dot(q_in, s_old.astype(bf16), preferred_element_type=f32))
        tot = lax.dot_general(g2, ones, TN, preferred_element_type=f32)
        decay = jnp.exp(tot[:LANES] + tot[LANES:])
        upd = lax.dot_general(k_end, v, TN, preferred_element_type=f32)
        s_ref[bi, p] = jnp.concatenate([decay, decay], axis=1) * s_old + jnp.where(own_block, upd, 0.0)
        og = og_ref[bi, :, cols]
        outs = []
        for hh in range(2):
            o = o2[:, hh * DV_GLA:(hh + 1) * DV_GLA]
            ms = jnp.mean(o * o, axis=-1, keepdims=True)
            gate = og[:, hh * DV_GLA:(hh + 1) * DV_GLA]
            outs.append(o * lax.rsqrt(ms + EPS) * gn * (gate * jax.nn.sigmoid(gate)))
        o_ref[bi, :, cols] = jnp.concatenate(outs, axis=1).astype(bf16)

    @pl.when(c == pl.num_programs(1) - 1)
    def _():
        for bi, p in chains:
            s = s_ref[bi, p]
            sout_ref[bi, p] = jnp.concatenate([s[:DK_GLA, :DV_GLA], s[DK_GLA:, DV_GLA:]], axis=0)


def _gla_call(qg, kg, vg, g, og, s0, gla_norm, n_batch, t, chunk, bb):
    nc = t // chunk
    tri, lev2 = _gla_tables(chunk)
    kspec = pl.BlockSpec((bb, chunk, WK_GLA), lambda i, c: (i, c, 0))
    vspec = pl.BlockSpec((bb, chunk, WV_GLA), lambda i, c: (i, c, 0))
    sspec = pl.BlockSpec((bb, GLA_PAIRS, 2 * DK_GLA, DV_GLA), lambda i, c: (i, 0, 0, 0))
    const = lambda a: pl.BlockSpec(a.shape, lambda i, c: (0,) * a.ndim)
    s0 = s0.reshape(n_batch, GLA_PAIRS, 2 * DK_GLA, DV_GLA)
    k3 = lambda a: a.reshape(n_batch, t, WK_GLA)
    v3 = lambda a: a.reshape(n_batch, t, WV_GLA)
    o, s_out = pl.pallas_call(
        functools.partial(_gla_kernel, chunk=chunk, bb=bb),
        grid=(n_batch // bb, nc),
        in_specs=[kspec, kspec, vspec, kspec, vspec, sspec, const(tri), const(lev2), const(gla_norm)],
        out_specs=[vspec, sspec],
        out_shape=[jax.ShapeDtypeStruct((n_batch, t, WV_GLA), bf16),
                   jax.ShapeDtypeStruct(s0.shape, f32)],
        scratch_shapes=[pltpu.VMEM((bb, GLA_PAIRS, 2 * DK_GLA, 2 * DV_GLA), f32),
                        pltpu.VMEM((bb * GLA_PAIRS, chunk, 2 * chunk), f32)],
        compiler_params=_params(("parallel", "arbitrary")),
        name="gla",
    )(k3(qg), k3(kg), v3(vg), k3(g), v3(og), s0, tri, lev2, gla_norm)
    return o.reshape(n_batch * t, WV_GLA), s_out.reshape(n_batch, H_GLA, DK_GLA, DV_GLA)


def _mlp_kernel(a_ref, o_ref, x_ref, mix_gate_ref, sh_ref, sc_ref, gate_ref, nw_ref, wo_ref, wu_ref, wd_ref,
                y_ref, *, ff_chunk):
    m = (jnp.dot(a_ref[...], wo_ref[:W_SB, :], preferred_element_type=f32)
         + jnp.dot(o_ref[...], wo_ref[W_SB:, :], preferred_element_type=f32))
    x = x_ref[...] + mix_gate_ref[...] * m
    ms = jnp.mean(x * x, axis=-1, keepdims=True)
    y = x * lax.rsqrt(ms + EPS) * nw_ref[...]
    h = (y * (1.0 + sc_ref[...]) + sh_ref[...]).astype(bf16)
    acc = jnp.zeros(x.shape, f32)
    for c0 in range(0, D_FF, ff_chunk):
        u = jnp.dot(h, wu_ref[:, c0:c0 + ff_chunk], preferred_element_type=f32)
        u = jnp.square(jnp.maximum(u, 0.0)).astype(bf16)
        acc = acc + jnp.dot(u, wd_ref[c0:c0 + ff_chunk, :], preferred_element_type=f32)
    y_ref[...] = x + gate_ref[...] * acc


def _mlp_call(a, o, x2d, mod, lw, tm, blocks_per_batch):
    n = x2d.shape[0]
    r = mod.shape[1]
    row = lambda w: pl.BlockSpec((tm, w), lambda i: (i, 0))
    const = lambda a: pl.BlockSpec(a.shape, lambda i: (0,) * a.ndim)
    piece = lambda k: _mod_spec(r, blocks_per_batch, k)
    return pl.pallas_call(
        functools.partial(_mlp_kernel, ff_chunk=1024),
        grid=(n // tm,),
        in_specs=[row(W_SB), row(WV_GLA), row(D_MODEL), piece(2), piece(3), piece(4), piece(5),
                  const(lw["norm_mlp"]), const(lw["w_out"]), const(lw["w_up"]), const(lw["w_down"])],
        out_specs=row(D_MODEL),
        out_shape=jax.ShapeDtypeStruct((n, D_MODEL), f32),
        compiler_params=_params(("parallel",)),
        name="out_mlp",
    )(a, o, x2d, mod, mod, mod, mod, lw["norm_mlp"], lw["w_out"], lw["w_up"], lw["w_down"])


def _layer_weights(l, w_in, w_out, w_up, w_down, norm_mix, norm_mlp, q_norm, k_norm, w_gate, b_gate,
                   gla_norm):
    wi = w_in[l]
    c_g = 3 * W_SB
    c_gate = c_g + 2 * WK_GLA + WV_GLA
    w_main = jnp.concatenate(
        [wi[:, c_g:c_gate], wi[:, c_gate + GATE_RANK:], wi[:, c_gate:c_gate + GATE_RANK],
         jnp.zeros((D_MODEL, GATE_PAD - GATE_RANK), f32)], axis=1).astype(bf16)
    w_qkvt = wi[:, :c_g].T.astype(bf16)
    wg = jnp.concatenate([w_gate[l], jnp.zeros((GATE_PAD - GATE_RANK, WK_GLA), f32)], axis=0).astype(bf16)
    return {
        "w_main": w_main, "w_qkvt": w_qkvt, "w_gate": wg,
        "b_gate": b_gate[l].reshape(1, WK_GLA),
        "q_norm": jnp.tile(q_norm[l], H_SB),
        "k_norm": jnp.tile(k_norm[l], H_SB),
        "norm_mix": norm_mix[l].reshape(1, D_MODEL),
        "norm_mlp": norm_mlp[l].reshape(1, D_MODEL),
        "gla_norm": gla_norm[l].reshape(1, DV_GLA),
        "w_out": w_out[l].astype(bf16), "w_up": w_up[l].astype(bf16), "w_down": w_down[l].astype(bf16),
    }


def _group_layer(x2d, mod, lw, cw, n_batch, t, past_kt, past_vt, layer, s0, tm, blocks_per_batch, chunk,
                 gla_bb, prev_kv=()):
    q, kt, vt, ktb, vtb, qg, kg, vg, g, og = _in_call(x2d, mod, lw, tm, blocks_per_batch, prev_kv)
    if past_kt is None:
        a = _sb_call(q, ktb, vtb, ktb, vtb, cw, n_batch=n_batch, t_q=t, prompt=True, layer=layer)
    else:
        a = _sb_call(q, ktb, vtb, past_kt, past_vt, cw, n_batch=n_batch, t_q=t, prompt=False, layer=layer)
    o, s_new = _gla_call(qg, kg, vg, g, og, s0, lw["gla_norm"], n_batch, t, chunk, gla_bb)
    x2d = _mlp_call(a, o, x2d, mod, lw, tm, blocks_per_batch)
    return x2d, kt, vt, s_new


def _heads_last(kt, n_batch, t):
    layers, nb = kt.shape[:2]
    return kt.reshape(layers, nb, H_SB, D_SB, -1).transpose(0, 1, 4, 2, 3).reshape(layers, n_batch, t, H_SB, D_SB)


def kernel(x_prompt, x_sample, c_prompt, c_sample, cache_k, cache_v, state_gla, w_ada, b_ada, norm_mix,
           norm_mlp, w_in, q_norm, k_norm, w_gate, b_gate, gla_norm, w_out, w_up, w_down):
    bp, tp, _ = x_prompt.shape
    bs, ts, _ = x_sample.shape
    past = cache_k.shape[2]
    n_c = bp + bs
    c_rows = -(-n_c // 16) * 16
    c_all = jnp.concatenate([c_prompt, c_sample, jnp.zeros((c_rows - n_c, D_MODEL), f32)], axis=0)
    mods = _ada_call(c_all, w_ada, b_ada)

    kk = np.arange(KB)
    half = np.concatenate([(kk[:, None] > kk[None, :]), np.ones((KB, KB), bool)], axis=1)
    cw = jnp.asarray(np.concatenate([half, half], axis=0).astype(np.float32), dtype=bf16)

    cache_kt = cache_k.transpose(0, 1, 3, 4, 2).reshape(DEPTH, bs, W_SB, past)
    cache_vt = cache_v.transpose(0, 1, 3, 4, 2).reshape(DEPTH, bs, W_SB, past)

    tm_p, tm_s = 512, bs * ts
    xp = x_prompt.reshape(bp * tp, D_MODEL)
    xs = x_sample.reshape(bs * ts, D_MODEL)
    sp_l, ks_l, vs_l, ss_l = [], [], [], []
    prompt_kv = ()
    for l in range(DEPTH):
        lw = _layer_weights(l, w_in, w_out, w_up, w_down, norm_mix, norm_mlp, q_norm, k_norm, w_gate,
                            b_gate, gla_norm)
        mod_p = mods[l, :bp].reshape(bp, 1, N_MOD * D_MODEL)
        mod_s = jnp.repeat(mods[l, bp:n_c], ts, axis=0).reshape(1, bs * ts, N_MOD * D_MODEL)
        s0_p = jnp.zeros((bp, H_GLA, DK_GLA, DV_GLA), f32)
        xp, kp, vp, sp = _group_layer(xp, mod_p, lw, cw, bp, tp, None, None, l, s0_p, tm_p, tp // tm_p, 256, bp,
                                      prompt_kv)
        prompt_kv = (kp, vp)
        xs, kn, vn, sn = _group_layer(xs, mod_s, lw, cw, bs, ts, cache_kt, cache_vt, l,
                                      state_gla[l].astype(f32), tm_s, 1, ts, 4)
        sp_l.append(sp)
        ks_l.append(_heads_last(kn, bs, ts)[0])
        vs_l.append(_heads_last(vn, bs, ts)[0])
        ss_l.append(sn)
    return (xp.reshape(bp, tp, D_MODEL), xs.reshape(bs, ts, D_MODEL),
            _heads_last(kp, bp, tp), _heads_last(vp, bp, tp), jnp.stack(sp_l),
            jnp.stack(ks_l), jnp.stack(vs_l), jnp.stack(ss_l))
```

```python
import functools

import numpy as np
import jax
import jax.numpy as jnp
from jax import lax
from jax.experimental import pallas as pl
from jax.experimental.pallas import tpu as pltpu

D_MODEL = 1024
DEPTH = 2
H_SB = 8
D_SB = 64
W_SB = H_SB * D_SB
H_GLA = 4
DK_GLA = 64
DV_GLA = 128
WK_GLA = H_GLA * DK_GLA
WV_GLA = H_GLA * DV_GLA
GATE_RANK = 16
GATE_TAU = 16.0
D_FF = 4 * D_MODEL
N_MOD = 6
EPS = 1e-6

LANES = 128
KB = 128
SB_FIRST_PASS = 3
SB_LOOP_GROUP = 1
SB_QUERY_BLOCKS = 2
N_PAIR = H_SB // 2
GATE_PAD = LANES
VMEM_LIMIT = 56 * 1024 * 1024
SURVIVAL_ZERO = 104.0
MASKED = -1e30

f32 = jnp.float32
bf16 = jnp.bfloat16
NT = (((1,), (1,)), ((), ()))
TN = (((0,), (0,)), ((), ()))


def _params(sem):
    return pltpu.CompilerParams(dimension_semantics=sem, vmem_limit_bytes=VMEM_LIMIT)


def _log_sigmoid(x):
    return jnp.minimum(x, 0.0) - jnp.log1p(jnp.exp(-jnp.abs(x)))


def _ada_kernel(c_ref, w_ref, b_ref, o_ref):
    c = c_ref[...]
    s = (c * jax.nn.sigmoid(c)).astype(bf16)
    o_ref[...] = jnp.dot(s, w_ref[...].astype(bf16), preferred_element_type=f32) + b_ref[...]


def _ada_call(c_all, w_ada, b_ada):
    rows = c_all.shape[0]
    return pl.pallas_call(
        _ada_kernel,
        grid=(DEPTH, N_MOD),
        in_specs=[
            pl.BlockSpec((rows, D_MODEL), lambda l, p: (0, 0)),
            pl.BlockSpec((None, D_MODEL, D_MODEL), lambda l, p: (l, 0, p)),
            pl.BlockSpec((None, 1, D_MODEL), lambda l, p: (l, 0, p)),
        ],
        out_specs=pl.BlockSpec((None, rows, D_MODEL), lambda l, p: (l, 0, p)),
        out_shape=jax.ShapeDtypeStruct((DEPTH, rows, N_MOD * D_MODEL), f32),
        compiler_params=_params(("parallel", "parallel")),
        name="ada_mod",
    )(c_all, w_ada, b_ada.reshape(DEPTH, 1, N_MOD * D_MODEL))


def _mod_spec(rows_per_block, blocks_per_batch, piece):
    return pl.BlockSpec((None, rows_per_block, D_MODEL),
                        lambda i: (i // blocks_per_batch, 0, piece))


def _head_rms(t):
    t3 = t.reshape(H_SB, D_SB, t.shape[1])
    ms = jnp.mean(t3 * t3, axis=1, keepdims=True)
    return (t3 * lax.rsqrt(ms + EPS)).reshape(t.shape)


def _in_kernel(*refs, n_prev):
    (x_ref, sh_ref, sc_ref, nw_ref, w_ref, wqkv_ref, qn_ref, kn_ref, wg_ref, bg_ref) = refs[:10]
    prev_refs = refs[10:10 + 2 * min(n_prev, 1)]
    (q_ref, kt_ref, vt_ref, ktb_ref, vtb_ref, qg_ref, kg_ref, vg_ref, g_ref, og_ref) = refs[len(refs) - 10:]
    for prev_ref, out_ref in zip(prev_refs, (kt_ref, vt_ref)):
        out_ref[:n_prev] = prev_ref[...]
    x = x_ref[...]
    ms = jnp.mean(x * x, axis=-1, keepdims=True)
    y = x * lax.rsqrt(ms + EPS) * nw_ref[...]
    h = (y * (1.0 + sc_ref[...]) + sh_ref[...]).astype(bf16)

    def proj(lo, hi):
        return jnp.dot(h, w_ref[:, lo:hi], preferred_element_type=f32)

    qkvt = lax.dot_general(wqkv_ref[...], h, NT, preferred_element_type=f32)
    qt = _head_rms(qkvt[:W_SB]) * qn_ref[...] * (D_SB ** -0.5)
    q_ref[...] = qt.T.astype(bf16)
    kt = _head_rms(qkvt[W_SB:2 * W_SB]) * kn_ref[...]
    kt_ref[n_prev] = kt
    ktb_ref[...] = kt.astype(bf16)
    vt = qkvt[2 * W_SB:]
    vt_ref[n_prev] = vt
    vtb_ref[...] = vt.astype(bf16)

    o = 0
    qg_ref[...] = proj(o, o + WK_GLA) * (DK_GLA ** -0.5)
    o += WK_GLA
    kg_ref[...] = proj(o, o + WK_GLA)
    o += WK_GLA
    vg_ref[...] = proj(o, o + WV_GLA).astype(bf16)
    o += WV_GLA
    og_ref[...] = proj(o, o + WV_GLA)
    o += WV_GLA
    gr = proj(o, o + GATE_PAD).astype(bf16)
    xg = jnp.dot(gr, wg_ref[...], preferred_element_type=f32) + bg_ref[...]
    g_ref[...] = _log_sigmoid(xg) * (1.0 / GATE_TAU)


def _in_call(x2d, mod, lw, tm, blocks_per_batch, prev_kv):
    n = x2d.shape[0]
    r = mod.shape[1]
    n_batch = n // (tm * blocks_per_batch)
    t = tm * blocks_per_batch
    n_prev = prev_kv[0].shape[0] if prev_kv else 0
    row = lambda w: pl.BlockSpec((tm, w), lambda i: (i, 0))
    col_map = lambda i: (i // blocks_per_batch, 0, i % blocks_per_batch)
    col = pl.BlockSpec((None, W_SB, tm), col_map)
    stack = lambda depth: pl.BlockSpec((depth, None, W_SB, tm), lambda i: (0,) + col_map(i))
    const = lambda a: pl.BlockSpec(a.shape, lambda i: (0,) * a.ndim)
    qn_t = jnp.broadcast_to(lw["q_norm"].reshape(W_SB, 1), (W_SB, tm))
    kn_t = jnp.broadcast_to(lw["k_norm"].reshape(W_SB, 1), (W_SB, tm))
    rows_out = [(WK_GLA, f32), (WK_GLA, f32), (WV_GLA, bf16), (WK_GLA, f32), (WV_GLA, f32)]
    kv_stack = jax.ShapeDtypeStruct((n_prev + 1, n_batch, W_SB, t), f32)
    kv_b = jax.ShapeDtypeStruct((n_batch, W_SB, t), bf16)
    return pl.pallas_call(
        functools.partial(_in_kernel, n_prev=n_prev),
        grid=(n // tm,),
        in_specs=[row(D_MODEL), _mod_spec(r, blocks_per_batch, 0), _mod_spec(r, blocks_per_batch, 1),
                  const(lw["norm_mix"]), const(lw["w_main"]), const(lw["w_qkvt"]),
                  const(qn_t), const(kn_t), const(lw["w_gate"]), const(lw["b_gate"])]
        + [stack(n_prev)] * len(prev_kv),
        out_specs=[row(W_SB), stack(n_prev + 1), stack(n_prev + 1), col, col] + [row(w) for w, _ in rows_out],
        out_shape=[jax.ShapeDtypeStruct((n, W_SB), bf16), kv_stack, kv_stack, kv_b, kv_b]
        + [jax.ShapeDtypeStruct((n, w), dt) for w, dt in rows_out],
        compiler_params=_params(("parallel",)),
        name="in_proj",
    )(x2d, mod, mod, lw["norm_mix"], lw["w_main"], lw["w_qkvt"], qn_t, kn_t,
      lw["w_gate"], lw["b_gate"], *prev_kv)


def _pair_diag(a):
    z = jnp.zeros((D_SB, a.shape[1]), a.dtype)
    top = jnp.concatenate([a[:D_SB], z], axis=0)
    bot = jnp.concatenate([z, a[D_SB:]], axis=0)
    return jnp.concatenate([top, bot], axis=1)


def _sb_scores(q_pair, k_tiles, biases):
    tiles = []
    for kt, bias in zip(k_tiles, biases):
        z2 = jnp.dot(q_pair, _pair_diag(kt), preferred_element_type=f32)
        for hh in range(2):
            z = z2[:, hh * KB:(hh + 1) * KB] + bias
            sp = jnp.maximum(z, 0.0) + jnp.log(1.0 + jnp.exp(-jnp.abs(z)))
            hi = sp.astype(bf16)
            lo = (sp - hi.astype(f32)).astype(bf16)
            tiles.append((z, jnp.concatenate([hi, lo], axis=1)))
    return tiles


def _sb_apply(tiles, cs, v_tiles, carries, acc, tq):
    new_carries = []
    ws = [[None, None] for _ in v_tiles]
    for hh in range(2):
        c = carries[hh]
        for bi in range(len(v_tiles)):
            ti = bi * 2 + hh
            part = cs[ti * tq:(ti + 1) * tq]
            total = jnp.broadcast_to(part[:, :1], (tq, KB))
            ws[bi][hh] = jnp.exp(tiles[ti][0] - (part if c is None else c + part)).astype(bf16)
            c = total if c is None else c + total
        new_carries.append(c)
    for bi, vt in enumerate(v_tiles):
        w_pair = jnp.concatenate(ws[bi], axis=1)
        pv = lax.dot_general(w_pair, _pair_diag(vt), NT, preferred_element_type=f32)
        acc = pv if acc is None else acc + pv
    return acc, new_carries


def _sb_kernel(q_ref, kd_ref, vd_ref, kp_ref, vp_ref, cw_ref, *rest, tq, n_sub, seg, n_past, layer):
    if n_past is None:
        o_ref, acc_ref, carry_ref = rest
    else:
        kfar_ref, vfar_ref, o_ref, acc_ref, carry_ref, kbuf_ref, vbuf_ref, sem = rest
    b = pl.program_id(0)
    step = pl.program_id(1)
    row = lax.broadcasted_iota(jnp.int32, (tq, KB), 0)
    col = lax.broadcasted_iota(jnp.int32, (tq, KB), 1)
    seg_id = b % (KB // seg)
    seg_shift = seg.bit_length() - 1
    own = jnp.logical_and(lax.shift_right_logical(col, seg_shift) == seg_id,
                          jnp.bitwise_and(col, seg - 1) < row)
    bias0 = jnp.where(own, 0.0, MASKED)
    cw = cw_ref[...]
    pair_rows = [slice(p * KB, (p + 1) * KB) for p in range(N_PAIR)]

    def past_block(j, first_pass):
        if n_past is not None and first_pass:
            c0 = (j - (n_past - (SB_FIRST_PASS - 1))) * KB
            return kp_ref[:, c0:c0 + KB].astype(bf16), vp_ref[:, c0:c0 + KB].astype(bf16), 0.0
        jc = pl.multiple_of(jnp.maximum(j, 0) * KB, KB)
        if n_past is None:
            kt, vt = kp_ref[b, :, pl.ds(jc, KB)], vp_ref[b, :, pl.ds(jc, KB)]
        else:
            copies = [pltpu.make_async_copy(src.at[layer, b, :, pl.ds(jc, KB)], dst, sem.at[i])
                      for i, (src, dst) in enumerate(((kfar_ref, kbuf_ref), (vfar_ref, vbuf_ref)))]
            for cp in copies:
                cp.start()
            for cp in copies:
                cp.wait()
            kt, vt = kbuf_ref[...], vbuf_ref[...]
        return kt.astype(bf16), vt.astype(bf16), jnp.where(j < 0, MASKED, 0.0)

    def run_group(subs, first):
        order = [(s, p) for s, _ in subs for p in range(N_PAIR)]
        blocks_of = dict(subs)
        tiles = {(s, p): _sb_scores(q_ref[s * tq:(s + 1) * tq, pair_rows[p]],
                                    [kt[pair_rows[p]] for kt, _, _ in blocks_of[s]],
                                    [bias for _, _, bias in blocks_of[s]]) for s, p in order}
        cs = jnp.dot(jnp.concatenate([t[1] for key in order for t in tiles[key]], axis=0), cw,
                     preferred_element_type=f32)
        start, floor = 0, {}
        for s, p in order:
            n_rows = 2 * len(blocks_of[s]) * tq
            carries = (None, None) if first else (carry_ref[s * H_SB + 2 * p], carry_ref[s * H_SB + 2 * p + 1])
            acc, cr = _sb_apply(tiles[s, p], cs[start:start + n_rows], [vt[pair_rows[p]] for _, vt, _ in blocks_of[s]],
                                carries, None if first else acc_ref[s * N_PAIR + p], tq)
            start += n_rows
            acc_ref[s * N_PAIR + p] = acc
            carry_ref[s * H_SB + 2 * p] = cr[0]
            carry_ref[s * H_SB + 2 * p + 1] = cr[1]
            m = jnp.minimum(cr[0], cr[1])
            floor[s] = m if s not in floor else jnp.minimum(floor[s], m)
        return {s: (jnp.min(m) < SURVIVAL_ZERO).astype(jnp.int32) for s, m in floor.items()}

    n_prev = [step * n_sub + s if n_past is None else n_past for s in range(n_sub)]
    first = [(s, [(kd_ref[:, s * KB:(s + 1) * KB], vd_ref[:, s * KB:(s + 1) * KB], bias0)]
              + [past_block(n_prev[s] - 1 - d, True) for d in range(SB_FIRST_PASS - 1)]) for s in range(n_sub)]
    alive0 = run_group(first, True)

    def cond(st):
        j, alive = st
        return jnp.logical_and(j >= 0, alive > 0)

    for s in range(n_sub):
        def body(st, s=s):
            j, _ = st
            blocks = [past_block(j - d, False) for d in range(SB_LOOP_GROUP)]
            return j - SB_LOOP_GROUP, run_group([(s, blocks)], False)[s]

        lax.while_loop(cond, body, (n_prev[s] - SB_FIRST_PASS, alive0[s]))
        o_ref[s * tq:(s + 1) * tq, :] = jnp.concatenate(
            [acc_ref[s * N_PAIR + p] for p in range(N_PAIR)], axis=1).astype(bf16)


def _sb_call(q, kd, vd, kp, vp, cw, *, n_batch, t_q, prompt, layer):
    tq = min(KB, t_q)
    n_sub = SB_QUERY_BLOCKS if prompt else 1
    nq = t_q // (tq * n_sub)
    q_spec = pl.BlockSpec((n_sub * tq, W_SB), lambda b, i: (b * nq + i, 0))
    if prompt:
        d_spec = pl.BlockSpec((None, W_SB, n_sub * KB), lambda b, i: (b, 0, i))
        p_spec = pl.BlockSpec(kp.shape, lambda b, i: (0, 0, 0))
        seg, n_past = KB, None
        far_specs, far_args, far_scratch = [], (), []
    else:
        past = kp.shape[-1]
        tail = (SB_FIRST_PASS - 1) * KB
        assert past % tail == 0 and SB_LOOP_GROUP == 1
        d_spec = pl.BlockSpec((None, W_SB, KB), lambda b, i: (0, 0, (b * t_q) // KB))
        p_spec = pl.BlockSpec((None, None, W_SB, tail), lambda b, i: (layer, b, 0, past // tail - 1))
        seg, n_past = t_q, past // KB
        far_specs, far_args = [pl.BlockSpec(memory_space=pl.ANY)] * 2, (kp, vp)
        far_scratch = [pltpu.VMEM((W_SB, KB), kp.dtype), pltpu.VMEM((W_SB, KB), vp.dtype),
                       pltpu.SemaphoreType.DMA((2,))]
    return pl.pallas_call(
        functools.partial(_sb_kernel, tq=tq, n_sub=n_sub, seg=seg, n_past=n_past, layer=layer),
        grid=(n_batch, nq),
        in_specs=[q_spec, d_spec, d_spec, p_spec, p_spec, pl.BlockSpec(cw.shape, lambda b, i: (0, 0))] + far_specs,
        out_specs=q_spec,
        out_shape=jax.ShapeDtypeStruct((n_batch * t_q, W_SB), bf16),
        scratch_shapes=[pltpu.VMEM((n_sub * N_PAIR, tq, LANES), f32),
                        pltpu.VMEM((n_sub * H_SB, tq, LANES), f32)] + far_scratch,
        compiler_params=_params(("parallel", "arbitrary")),
        name="stick_breaking",
    )(q, kd, vd, kp, vp, cw, *far_args)


GLA_PAIRS = H_GLA // 2
GLA_SAFE_DECAY = 60.0


def _gla_tables(chunk):
    idx = np.arange(chunk)
    t, s = idx[:, None], idx[None, :]
    tri = jnp.asarray((s <= t).astype(np.float32), dtype=bf16)
    level = np.where(t > s, np.floor(np.log2(np.maximum(t ^ s, 1))).astype(np.int32), -1)
    level = np.where(t < s, -2, level).astype(np.int32)
    return tri, jnp.asarray(np.concatenate([level, level], axis=1))


def _level_reference(b, h):
    L = b.shape[0]
    if h >= 4:
        return jnp.concatenate(
            [jnp.broadcast_to(b[c0 + h - 1:c0 + h, :], (2 * h, LANES)) for c0 in range(0, L, 2 * h)], axis=0)
    b3 = b.reshape(L // 8, 8, LANES)
    pos = jnp.bitwise_and(lax.broadcasted_iota(jnp.int32, b3.shape, 1), 2 * h - 1)
    up = lambda n: pltpu.roll(b3, n, 1)
    if h == 1:
        ref = jnp.where(pos == 0, b3, up(1))
    else:
        ref = jnp.where(pos == 0, up(7), jnp.where(pos == 1, b3, jnp.where(pos == 2, up(1), up(2))))
    return ref.reshape(L, LANES)


def _two_heads(a, first_head):
    return jnp.concatenate([jnp.where(first_head, a, 0.0).astype(bf16),
                            jnp.where(first_head, 0.0, a).astype(bf16)], axis=0)


def _gla_att_single(q_in, k, b, lev2, first_head):
    a = lax.dot_general(q_in, _two_heads(k * jnp.exp(-b), first_head), NT, preferred_element_type=f32)
    return jnp.where(lev2 >= -1, a, 0.0)


def _gla_att_levels(q, k, b, lev2, first_head, n_lev):
    a = lax.dot_general(q.astype(bf16), _two_heads(k, first_head), NT, preferred_element_type=f32)
    att = jnp.where(lev2 == -1, a, 0.0)
    for lev in range(n_lev):
        d = b - _level_reference(b, 1 << lev)
        e = jnp.exp(-jnp.abs(d))
        is_query = d < 0.0
        q_l = (q * jnp.where(is_query, e, 1.0)).astype(bf16)
        k_l = k * jnp.where(is_query, 1.0, e)
        a = lax.dot_general(q_l, _two_heads(k_l, first_head), NT, preferred_element_type=f32)
        att = jnp.where(lev2 == lev, a, att)
    return att


def _gla_kernel(q_ref, k_ref, v_ref, g_ref, og_ref, s0_ref, tri_ref, lev_ref, gn_ref,
                o_ref, sout_ref, s_ref, att_ref, *, chunk, bb):
    c = pl.program_id(1)
    L = chunk
    n_lev = L.bit_length() - 1
    chains = [(bi, p) for bi in range(bb) for p in range(GLA_PAIRS)]
    zeros_s = jnp.zeros((DK_GLA, DV_GLA), f32)

    @pl.when(c == 0)
    def _():
        for bi, p in chains:
            s0 = s0_ref[bi, p]
            s_ref[bi, p] = jnp.concatenate(
                [jnp.concatenate([s0[:DK_GLA], zeros_s], axis=1),
                 jnp.concatenate([zeros_s, s0[DK_GLA:]], axis=1)], axis=0)

    lane = lax.broadcasted_iota(jnp.int32, (L, LANES), 1)
    first_head = lane < DK_GLA
    lev2 = lev_ref[...]
    tri = tri_ref[...]

    pre = []
    floor = None
    for bi, p in chains:
        lanes = slice(p * LANES, (p + 1) * LANES)
        q = q_ref[bi, :, lanes]
        k = k_ref[bi, :, lanes]
        g = g_ref[bi, :, lanes]
        g_hi = g.astype(bf16)
        g_lo = (g - g_hi.astype(f32)).astype(bf16)
        g2 = jnp.concatenate([g_hi, g_lo], axis=1)
        bs = jnp.dot(tri, g2, preferred_element_type=f32)
        b = bs[:, :LANES] + bs[:, LANES:]
        b_last = b[L - 1:L, :]
        q_in = (q * jnp.exp(b)).astype(bf16)
        k_end = (k * jnp.exp(b_last - b)).astype(bf16)
        pre.append((q, k, g2, b, q_in, k_end))
        floor = b_last if floor is None else jnp.minimum(floor, b_last)
    single = jnp.min(floor) >= -GLA_SAFE_DECAY

    @pl.when(single)
    def _():
        for ci, (q, k, g2, b, q_in, k_end) in enumerate(pre):
            att_ref[ci] = _gla_att_single(q_in, k, b, lev2, first_head)

    @pl.when(jnp.logical_not(single))
    def _():
        for ci, (q, k, g2, b, q_in, k_end) in enumerate(pre):
            att_ref[ci] = _gla_att_levels(q, k, b, lev2, first_head, n_lev)

    rowi = lax.broadcasted_iota(jnp.int32, (2 * DK_GLA, 2 * DV_GLA), 0)
    coli = lax.broadcasted_iota(jnp.int32, (2 * DK_GLA, 2 * DV_GLA), 1)
    own_block = (rowi < DK_GLA) == (coli < DV_GLA)
    ones = jnp.ones((L, LANES), bf16)
    zeros_v = jnp.zeros((L, DV_GLA), bf16)
    gn = gn_ref[...]
    for ci, (bi, p) in enumerate(chains):
        q, k, g2, b, q_in, k_end = pre[ci]
        cols = slice(p * 2 * DV_GLA, (p + 1) * 2 * DV_GLA)
        v = v_ref[bi, :, cols].astype(bf16)
        v_diag = jnp.concatenate([jnp.concatenate([v[:, :DV_GLA], zeros_v], axis=1),
                                  jnp.concatenate([zeros_v, v[:, DV_GLA:]], axis=1)], axis=0)
        s_old = s_ref[bi, p]
        o2 = (jnp.dot(att_ref[ci].astype(bf16), v_diag, preferred_element_type=f32)
              + jnp.dot(q_in, s_old.astype(bf16), preferred_element_type=f32))
        tot = lax.dot_general(g2, ones, TN, preferred_element_type=f32)
        decay = jnp.exp(tot[:LANES] + tot[LANES:])
        upd = lax.dot_general(k_end, v, TN, preferred_element_type=f32)
        s_ref[bi, p] = jnp.concatenate([decay, decay], axis=1) * s_old + jnp.where(own_block, upd, 0.0)
        og = og_ref[bi, :, cols]
        outs = []
        for hh in range(2):
            o = o2[:, hh * DV_GLA:(hh + 1) * DV_GLA]
            ms = jnp.mean(o * o, axis=-1, keepdims=True)
            gate = og[:, hh * DV_GLA:(hh + 1) * DV_GLA]
            outs.append(o * lax.rsqrt(ms + EPS) * gn * (gate * jax.nn.sigmoid(gate)))
        o_ref[bi, :, cols] = jnp.concatenate(outs, axis=1).astype(bf16)

    @pl.when(c == pl.num_programs(1) - 1)
    def _():
        for bi, p in chains:
            s = s_ref[bi, p]
            sout_ref[bi, p] = jnp.concatenate([s[:DK_GLA, :DV_GLA], s[DK_GLA:, DV_GLA:]], axis=0)


def _gla_call(qg, kg, vg, g, og, s0, gla_norm, n_batch, t, chunk, bb):
    nc = t // chunk
    tri, lev2 = _gla_tables(chunk)
    kspec = pl.BlockSpec((bb, chunk, WK_GLA), lambda i, c: (i, c, 0))
    vspec = pl.BlockSpec((bb, chunk, WV_GLA), lambda i, c: (i, c, 0))
    sspec = pl.BlockSpec((bb, GLA_PAIRS, 2 * DK_GLA, DV_GLA), lambda i, c: (i, 0, 0, 0))
    const = lambda a: pl.BlockSpec(a.shape, lambda i, c: (0,) * a.ndim)
    s0 = s0.reshape(n_batch, GLA_PAIRS, 2 * DK_GLA, DV_GLA)
    k3 = lambda a: a.reshape(n_batch, t, WK_GLA)
    v3 = lambda a: a.reshape(n_batch, t, WV_GLA)
    o, s_out = pl.pallas_call(
        functools.partial(_gla_kernel, chunk=chunk, bb=bb),
        grid=(n_batch // bb, nc),
        in_specs=[kspec, kspec, vspec, kspec, vspec, sspec, const(tri), const(lev2), const(gla_norm)],
        out_specs=[vspec, sspec],
        out_shape=[jax.ShapeDtypeStruct((n_batch, t, WV_GLA), bf16),
                   jax.ShapeDtypeStruct(s0.shape, f32)],
        scratch_shapes=[pltpu.VMEM((bb, GLA_PAIRS, 2 * DK_GLA, 2 * DV_GLA), f32),
                        pltpu.VMEM((bb * GLA_PAIRS, chunk, 2 * chunk), f32)],
        compiler_params=_params(("parallel", "arbitrary")),
        name="gla",
    )(k3(qg), k3(kg), v3(vg), k3(g), v3(og), s0, tri, lev2, gla_norm)
    return o.reshape(n_batch * t, WV_GLA), s_out.reshape(n_batch, H_GLA, DK_GLA, DV_GLA)


def _mlp_kernel(a_ref, o_ref, x_ref, mix_gate_ref, sh_ref, sc_ref, gate_ref, nw_ref, wo_ref, wu_ref, wd_ref,
                y_ref, *, ff_chunk):
    m = (jnp.dot(a_ref[...], wo_ref[:W_SB, :], preferred_element_type=f32)
         + jnp.dot(o_ref[...], wo_ref[W_SB:, :], preferred_element_type=f32))
    x = x_ref[...] + mix_gate_ref[...] * m
    ms = jnp.mean(x * x, axis=-1, keepdims=True)
    y = x * lax.rsqrt(ms + EPS) * nw_ref[...]
    h = (y * (1.0 + sc_ref[...]) + sh_ref[...]).astype(bf16)
    acc = jnp.zeros(x.shape, f32)
    for c0 in range(0, D_FF, ff_chunk):
        u = jnp.dot(h, wu_ref[:, c0:c0 + ff_chunk], preferred_element_type=f32)
        u = jnp.square(jnp.maximum(u, 0.0)).astype(bf16)
        acc = acc + jnp.dot(u, wd_ref[c0:c0 + ff_chunk, :], preferred_element_type=f32)
    y_ref[...] = x + gate_ref[...] * acc


def _mlp_call(a, o, x2d, mod, lw, tm, blocks_per_batch):
    n = x2d.shape[0]
    r = mod.shape[1]
    row = lambda w: pl.BlockSpec((tm, w), lambda i: (i, 0))
    const = lambda a: pl.BlockSpec(a.shape, lambda i: (0,) * a.ndim)
    piece = lambda k: _mod_spec(r, blocks_per_batch, k)
    return pl.pallas_call(
        functools.partial(_mlp_kernel, ff_chunk=1024),
        grid=(n // tm,),
        in_specs=[row(W_SB), row(WV_GLA), row(D_MODEL), piece(2), piece(3), piece(4), piece(5),
                  const(lw["norm_mlp"]), const(lw["w_out"]), const(lw["w_up"]), const(lw["w_down"])],
        out_specs=row(D_MODEL),
        out_shape=jax.ShapeDtypeStruct((n, D_MODEL), f32),
        compiler_params=_params(("parallel",)),
        name="out_mlp",
    )(a, o, x2d, mod, mod, mod, mod, lw["norm_mlp"], lw["w_out"], lw["w_up"], lw["w_down"])


def _layer_weights(l, w_in, w_out, w_up, w_down, norm_mix, norm_mlp, q_norm, k_norm, w_gate, b_gate,
                   gla_norm):
    wi = w_in[l]
    c_g = 3 * W_SB
    c_gate = c_g + 2 * WK_GLA + WV_GLA
    w_main = jnp.concatenate(
        [wi[:, c_g:c_gate], wi[:, c_gate + GATE_RANK:], wi[:, c_gate:c_gate + GATE_RANK],
         jnp.zeros((D_MODEL, GATE_PAD - GATE_RANK), f32)], axis=1).astype(bf16)
    w_qkvt = wi[:, :c_g].T.astype(bf16)
    wg = jnp.concatenate([w_gate[l], jnp.zeros((GATE_PAD - GATE_RANK, WK_GLA), f32)], axis=0).astype(bf16)
    return {
        "w_main": w_main, "w_qkvt": w_qkvt, "w_gate": wg,
        "b_gate": b_gate[l].reshape(1, WK_GLA),
        "q_norm": jnp.tile(q_norm[l], H_SB),
        "k_norm": jnp.tile(k_norm[l], H_SB),
        "norm_mix": norm_mix[l].reshape(1, D_MODEL),
        "norm_mlp": norm_mlp[l].reshape(1, D_MODEL),
        "gla_norm": gla_norm[l].reshape(1, DV_GLA),
        "w_out": w_out[l].astype(bf16), "w_up": w_up[l].astype(bf16), "w_down": w_down[l].astype(bf16),
    }


def _group_layer(x2d, mod, lw, cw, n_batch, t, past_kt, past_vt, layer, s0, tm, blocks_per_batch, chunk,
                 gla_bb, prev_kv=()):
    q, kt, vt, ktb, vtb, qg, kg, vg, g, og = _in_call(x2d, mod, lw, tm, blocks_per_batch, prev_kv)
    if past_kt is None:
        a = _sb_call(q, ktb, vtb, ktb, vtb, cw, n_batch=n_batch, t_q=t, prompt=True, layer=layer)
    else:
        a = _sb_call(q, ktb, vtb, past_kt, past_vt, cw, n_batch=n_batch, t_q=t, prompt=False, layer=layer)
    o, s_new = _gla_call(qg, kg, vg, g, og, s0, lw["gla_norm"], n_batch, t, chunk, gla_bb)
    x2d = _mlp_call(a, o, x2d, mod, lw, tm, blocks_per_batch)
    return x2d, kt, vt, s_new


def _heads_last(kt, n_batch, t):
    layers, nb = kt.shape[:2]
    return kt.reshape(layers, nb, H_SB, D_SB, -1).transpose(0, 1, 4, 2, 3).reshape(layers, n_batch, t, H_SB, D_SB)


def kernel(x_prompt, x_sample, c_prompt, c_sample, cache_k, cache_v, state_gla, w_ada, b_ada, norm_mix,
           norm_mlp, w_in, q_norm, k_norm, w_gate, b_gate, gla_norm, w_out, w_up, w_down):
    bp, tp, _ = x_prompt.shape
    bs, ts, _ = x_sample.shape
    past = cache_k.shape[2]
    n_c = bp + bs
    c_rows = -(-n_c // 16) * 16
    c_all = jnp.concatenate([c_prompt, c_sample, jnp.zeros((c_rows - n_c, D_MODEL), f32)], axis=0)
    mods = _ada_call(c_all, w_ada, b_ada)

    kk = np.arange(KB)
    half = kk[:, None] >= kk[None, :]
    cw = jnp.asarray(np.concatenate([half, half], axis=0).astype(np.float32), dtype=bf16)

    cache_kt = cache_k.transpose(0, 1, 3, 4, 2).reshape(DEPTH, bs, W_SB, past)
    cache_vt = cache_v.transpose(0, 1, 3, 4, 2).reshape(DEPTH, bs, W_SB, past)

    tm_p, tm_s = 512, bs * ts
    xp = x_prompt.reshape(bp * tp, D_MODEL)
    xs = x_sample.reshape(bs * ts, D_MODEL)
    sp_l, ks_l, vs_l, ss_l = [], [], [], []
    prompt_kv = ()
    for l in range(DEPTH):
        lw = _layer_weights(l, w_in, w_out, w_up, w_down, norm_mix, norm_mlp, q_norm, k_norm, w_gate,
                            b_gate, gla_norm)
        mod_p = mods[l, :bp].reshape(bp, 1, N_MOD * D_MODEL)
        mod_s = jnp.repeat(mods[l, bp:n_c], ts, axis=0).reshape(1, bs * ts, N_MOD * D_MODEL)
        s0_p = jnp.zeros((bp, H_GLA, DK_GLA, DV_GLA), f32)
        xp, kp, vp, sp = _group_layer(xp, mod_p, lw, cw, bp, tp, None, None, l, s0_p, tm_p, tp // tm_p, 256, bp,
                                      prompt_kv)
        prompt_kv = (kp, vp)
        xs, kn, vn, sn = _group_layer(xs, mod_s, lw, cw, bs, ts, cache_kt, cache_vt, l,
                                      state_gla[l].astype(f32), tm_s, 1, ts, 4)
        sp_l.append(sp)
        ks_l.append(_heads_last(kn, bs, ts)[0])
        vs_l.append(_heads_last(vn, bs, ts)[0])
        ss_l.append(sn)
    return (xp.reshape(bp, tp, D_MODEL), xs.reshape(bs, ts, D_MODEL),
            _heads_last(kp, bp, tp), _heads_last(vp, bp, tp), jnp.stack(sp_l),
            jnp.stack(ks_l), jnp.stack(vs_l), jnp.stack(ss_l))
```

```python
import functools

import numpy as np
import jax
import jax.numpy as jnp
from jax import lax
from jax.experimental import pallas as pl
from jax.experimental.pallas import tpu as pltpu

D_MODEL = 1024
DEPTH = 2
H_SB = 8
D_SB = 64
W_SB = H_SB * D_SB
H_GLA = 4
DK_GLA = 64
DV_GLA = 128
WK_GLA = H_GLA * DK_GLA
WV_GLA = H_GLA * DV_GLA
GATE_RANK = 16
GATE_TAU = 16.0
D_FF = 4 * D_MODEL
N_MOD = 6
EPS = 1e-6

LANES = 128
KB = 128
SB_FIRST_PASS = 3
SB_LOOP_GROUP = 1
SB_QUERY_BLOCKS = 4
N_PAIR = H_SB // 2
GATE_PAD = LANES
VMEM_LIMIT = 56 * 1024 * 1024
SURVIVAL_ZERO = 104.0
MASKED = -1e30

f32 = jnp.float32
bf16 = jnp.bfloat16
NT = (((1,), (1,)), ((), ()))
TN = (((0,), (0,)), ((), ()))


def _params(sem):
    return pltpu.CompilerParams(dimension_semantics=sem, vmem_limit_bytes=VMEM_LIMIT)


def _log_sigmoid(x):
    return jnp.minimum(x, 0.0) - jnp.log1p(jnp.exp(-jnp.abs(x)))


def _ada_kernel(c_ref, w_ref, b_ref, o_ref):
    c = c_ref[...]
    s = (c * jax.nn.sigmoid(c)).astype(bf16)
    o_ref[...] = jnp.dot(s, w_ref[...].astype(bf16), preferred_element_type=f32) + b_ref[...]


def _ada_call(c_all, w_ada, b_ada):
    rows = c_all.shape[0]
    return pl.pallas_call(
        _ada_kernel,
        grid=(DEPTH, N_MOD),
        in_specs=[
            pl.BlockSpec((rows, D_MODEL), lambda l, p: (0, 0)),
            pl.BlockSpec((None, D_MODEL, D_MODEL), lambda l, p: (l, 0, p)),
            pl.BlockSpec((None, 1, D_MODEL), lambda l, p: (l, 0, p)),
        ],
        out_specs=pl.BlockSpec((None, rows, D_MODEL), lambda l, p: (l, 0, p)),
        out_shape=jax.ShapeDtypeStruct((DEPTH, rows, N_MOD * D_MODEL), f32),
        compiler_params=_params(("parallel", "parallel")),
        name="ada_mod",
    )(c_all, w_ada, b_ada.reshape(DEPTH, 1, N_MOD * D_MODEL))


def _mod_spec(rows_per_block, blocks_per_batch, piece):
    return pl.BlockSpec((None, rows_per_block, D_MODEL),
                        lambda i: (i // blocks_per_batch, 0, piece))


def _head_rms(t):
    t3 = t.reshape(H_SB, D_SB, t.shape[1])
    ms = jnp.mean(t3 * t3, axis=1, keepdims=True)
    return (t3 * lax.rsqrt(ms + EPS)).reshape(t.shape)


def _in_kernel(*refs, n_prev):
    (x_ref, sh_ref, sc_ref, nw_ref, w_ref, wqkv_ref, qn_ref, kn_ref, wg_ref, bg_ref) = refs[:10]
    prev_refs = refs[10:10 + 2 * min(n_prev, 1)]
    (q_ref, kt_ref, vt_ref, ktb_ref, vtb_ref, qg_ref, kg_ref, vg_ref, g_ref, og_ref) = refs[len(refs) - 10:]
    for prev_ref, out_ref in zip(prev_refs, (kt_ref, vt_ref)):
        out_ref[:n_prev] = prev_ref[...]
    x = x_ref[...]
    ms = jnp.mean(x * x, axis=-1, keepdims=True)
    y = x * lax.rsqrt(ms + EPS) * nw_ref[...]
    h = (y * (1.0 + sc_ref[...]) + sh_ref[...]).astype(bf16)

    def proj(lo, hi):
        return jnp.dot(h, w_ref[:, lo:hi], preferred_element_type=f32)

    qkvt = lax.dot_general(wqkv_ref[...], h, NT, preferred_element_type=f32)
    qt = _head_rms(qkvt[:W_SB]) * qn_ref[...] * (D_SB ** -0.5)
    q_ref[...] = qt.T.astype(bf16)
    kt = _head_rms(qkvt[W_SB:2 * W_SB]) * kn_ref[...]
    kt_ref[n_prev] = kt
    ktb_ref[...] = kt.astype(bf16)
    vt = qkvt[2 * W_SB:]
    vt_ref[n_prev] = vt
    vtb_ref[...] = vt.astype(bf16)

    o = 0
    qg_ref[...] = proj(o, o + WK_GLA) * (DK_GLA ** -0.5)
    o += WK_GLA
    kg_ref[...] = proj(o, o + WK_GLA)
    o += WK_GLA
    vg_ref[...] = proj(o, o + WV_GLA).astype(bf16)
    o += WV_GLA
    og_ref[...] = proj(o, o + WV_GLA)
    o += WV_GLA
    gr = proj(o, o + GATE_PAD).astype(bf16)
    xg = jnp.dot(gr, wg_ref[...], preferred_element_type=f32) + bg_ref[...]
    g_ref[...] = _log_sigmoid(xg) * (1.0 / GATE_TAU)


def _in_call(x2d, mod, lw, tm, blocks_per_batch, prev_kv):
    n = x2d.shape[0]
    r = mod.shape[1]
    n_batch = n // (tm * blocks_per_batch)
    t = tm * blocks_per_batch
    n_prev = prev_kv[0].shape[0] if prev_kv else 0
    row = lambda w: pl.BlockSpec((tm, w), lambda i: (i, 0))
    col_map = lambda i: (i // blocks_per_batch, 0, i % blocks_per_batch)
    col = pl.BlockSpec((None, W_SB, tm), col_map)
    stack = lambda depth: pl.BlockSpec((depth, None, W_SB, tm), lambda i: (0,) + col_map(i))
    const = lambda a: pl.BlockSpec(a.shape, lambda i: (0,) * a.ndim)
    qn_t = jnp.broadcast_to(lw["q_norm"].reshape(W_SB, 1), (W_SB, tm))
    kn_t = jnp.broadcast_to(lw["k_norm"].reshape(W_SB, 1), (W_SB, tm))
    rows_out = [(WK_GLA, f32), (WK_GLA, f32), (WV_GLA, bf16), (WK_GLA, f32), (WV_GLA, f32)]
    kv_stack = jax.ShapeDtypeStruct((n_prev + 1, n_batch, W_SB, t), f32)
    kv_b = jax.ShapeDtypeStruct((n_batch, W_SB, t), bf16)
    return pl.pallas_call(
        functools.partial(_in_kernel, n_prev=n_prev),
        grid=(n // tm,),
        in_specs=[row(D_MODEL), _mod_spec(r, blocks_per_batch, 0), _mod_spec(r, blocks_per_batch, 1),
                  const(lw["norm_mix"]), const(lw["w_main"]), const(lw["w_qkvt"]),
                  const(qn_t), const(kn_t), const(lw["w_gate"]), const(lw["b_gate"])]
        + [stack(n_prev)] * len(prev_kv),
        out_specs=[row(W_SB), stack(n_prev + 1), stack(n_prev + 1), col, col] + [row(w) for w, _ in rows_out],
        out_shape=[jax.ShapeDtypeStruct((n, W_SB), bf16), kv_stack, kv_stack, kv_b, kv_b]
        + [jax.ShapeDtypeStruct((n, w), dt) for w, dt in rows_out],
        compiler_params=_params(("parallel",)),
        name="in_proj",
    )(x2d, mod, mod, lw["norm_mix"], lw["w_main"], lw["w_qkvt"], qn_t, kn_t,
      lw["w_gate"], lw["b_gate"], *prev_kv)


def _pair_diag(a):
    z = jnp.zeros((D_SB, a.shape[1]), a.dtype)
    top = jnp.concatenate([a[:D_SB], z], axis=0)
    bot = jnp.concatenate([z, a[D_SB:]], axis=0)
    return jnp.concatenate([top, bot], axis=1)


def _sb_scores(q_pair, k_tiles, biases):
    tiles = []
    for kt, bias in zip(k_tiles, biases):
        z2 = jnp.dot(q_pair, _pair_diag(kt), preferred_element_type=f32)
        for hh in range(2):
            z = z2[:, hh * KB:(hh + 1) * KB] + bias
            sp = jnp.maximum(z, 0.0) + jnp.log(1.0 + jnp.exp(-jnp.abs(z)))
            hi = sp.astype(bf16)
            lo = (sp - hi.astype(f32)).astype(bf16)
            tiles.append((z, jnp.concatenate([hi, lo], axis=1)))
    return tiles


def _sb_apply(tiles, cs, v_tiles, carries, acc, tq):
    new_carries = []
    ws = [[None, None] for _ in v_tiles]
    for hh in range(2):
        c = carries[hh]
        for bi in range(len(v_tiles)):
            ti = bi * 2 + hh
            part = cs[ti * tq:(ti + 1) * tq]
            total = jnp.broadcast_to(part[:, :1], (tq, KB))
            ws[bi][hh] = jnp.exp(tiles[ti][0] - (part if c is None else c + part)).astype(bf16)
            c = total if c is None else c + total
        new_carries.append(c)
    for bi, vt in enumerate(v_tiles):
        w_pair = jnp.concatenate(ws[bi], axis=1)
        pv = lax.dot_general(w_pair, _pair_diag(vt), NT, preferred_element_type=f32)
        acc = pv if acc is None else acc + pv
    return acc, new_carries


def _sb_kernel(q_ref, kd_ref, vd_ref, kp_ref, vp_ref, cw_ref, *rest, tq, n_sub, seg, n_past, layer):
    if n_past is None:
        o_ref, acc_ref, carry_ref = rest
    else:
        kfar_ref, vfar_ref, o_ref, acc_ref, carry_ref, kbuf_ref, vbuf_ref, sem = rest
    b = pl.program_id(0)
    step = pl.program_id(1)
    row = lax.broadcasted_iota(jnp.int32, (tq, KB), 0)
    col = lax.broadcasted_iota(jnp.int32, (tq, KB), 1)
    seg_id = b % (KB // seg)
    seg_shift = seg.bit_length() - 1
    own = jnp.logical_and(lax.shift_right_logical(col, seg_shift) == seg_id,
                          jnp.bitwise_and(col, seg - 1) < row)
    bias0 = jnp.where(own, 0.0, MASKED)
    cw = cw_ref[...]
    pair_rows = [slice(p * KB, (p + 1) * KB) for p in range(N_PAIR)]

    def past_block(j, first_pass):
        if n_past is not None and first_pass:
            c0 = (j - (n_past - (SB_FIRST_PASS - 1))) * KB
            return kp_ref[:, c0:c0 + KB].astype(bf16), vp_ref[:, c0:c0 + KB].astype(bf16), 0.0
        jc = pl.multiple_of(jnp.maximum(j, 0) * KB, KB)
        if n_past is None:
            kt, vt = kp_ref[b, :, pl.ds(jc, KB)], vp_ref[b, :, pl.ds(jc, KB)]
        else:
            copies = [pltpu.make_async_copy(src.at[layer, b, :, pl.ds(jc, KB)], dst, sem.at[i])
                      for i, (src, dst) in enumerate(((kfar_ref, kbuf_ref), (vfar_ref, vbuf_ref)))]
            for cp in copies:
                cp.start()
            for cp in copies:
                cp.wait()
            kt, vt = kbuf_ref[...], vbuf_ref[...]
        return kt.astype(bf16), vt.astype(bf16), jnp.where(j < 0, MASKED, 0.0)

    def run_group(subs, first):
        order = [(s, p) for s, _ in subs for p in range(N_PAIR)]
        blocks_of = dict(subs)
        tiles = {(s, p): _sb_scores(q_ref[s * tq:(s + 1) * tq, pair_rows[p]],
                                    [kt[pair_rows[p]] for kt, _, _ in blocks_of[s]],
                                    [bias for _, _, bias in blocks_of[s]]) for s, p in order}
        cs = jnp.dot(jnp.concatenate([t[1] for key in order for t in tiles[key]], axis=0), cw,
                     preferred_element_type=f32)
        start, floor = 0, {}
        for s, p in order:
            n_rows = 2 * len(blocks_of[s]) * tq
            carries = (None, None) if first else (carry_ref[s * H_SB + 2 * p], carry_ref[s * H_SB + 2 * p + 1])
            acc, cr = _sb_apply(tiles[s, p], cs[start:start + n_rows], [vt[pair_rows[p]] for _, vt, _ in blocks_of[s]],
                                carries, None if first else acc_ref[s * N_PAIR + p], tq)
            start += n_rows
            acc_ref[s * N_PAIR + p] = acc
            carry_ref[s * H_SB + 2 * p] = cr[0]
            carry_ref[s * H_SB + 2 * p + 1] = cr[1]
            m = jnp.minimum(cr[0], cr[1])
            floor[s] = m if s not in floor else jnp.minimum(floor[s], m)
        return {s: (jnp.min(m) < SURVIVAL_ZERO).astype(jnp.int32) for s, m in floor.items()}

    n_prev = [step * n_sub + s if n_past is None else n_past for s in range(n_sub)]
    first = [(s, [(kd_ref[:, s * KB:(s + 1) * KB], vd_ref[:, s * KB:(s + 1) * KB], bias0)]
              + [past_block(n_prev[s] - 1 - d, True) for d in range(SB_FIRST_PASS - 1)]) for s in range(n_sub)]
    alive0 = run_group(first, True)

    def cond(st):
        j, alive = st
        return jnp.logical_and(j >= 0, alive > 0)

    for s in range(n_sub):
        def body(st, s=s):
            j, _ = st
            blocks = [past_block(j - d, False) for d in range(SB_LOOP_GROUP)]
            return j - SB_LOOP_GROUP, run_group([(s, blocks)], False)[s]

        lax.while_loop(cond, body, (n_prev[s] - SB_FIRST_PASS, alive0[s]))
        o_ref[s * tq:(s + 1) * tq, :] = jnp.concatenate(
            [acc_ref[s * N_PAIR + p] for p in range(N_PAIR)], axis=1).astype(bf16)


def _sb_call(q, kd, vd, kp, vp, cw, *, n_batch, t_q, prompt, layer):
    tq = min(KB, t_q)
    n_sub = SB_QUERY_BLOCKS if prompt else 1
    nq = t_q // (tq * n_sub)
    q_spec = pl.BlockSpec((n_sub * tq, W_SB), lambda b, i: (b * nq + i, 0))
    if prompt:
        d_spec = pl.BlockSpec((None, W_SB, n_sub * KB), lambda b, i: (b, 0, i))
        p_spec = pl.BlockSpec(kp.shape, lambda b, i: (0, 0, 0))
        seg, n_past = KB, None
        far_specs, far_args, far_scratch = [], (), []
    else:
        past = kp.shape[-1]
        tail = (SB_FIRST_PASS - 1) * KB
        assert past % tail == 0 and SB_LOOP_GROUP == 1
        d_spec = pl.BlockSpec((None, W_SB, KB), lambda b, i: (0, 0, (b * t_q) // KB))
        p_spec = pl.BlockSpec((None, None, W_SB, tail), lambda b, i: (layer, b, 0, past // tail - 1))
        seg, n_past = t_q, past // KB
        far_specs, far_args = [pl.BlockSpec(memory_space=pl.ANY)] * 2, (kp, vp)
        far_scratch = [pltpu.VMEM((W_SB, KB), kp.dtype), pltpu.VMEM((W_SB, KB), vp.dtype),
                       pltpu.SemaphoreType.DMA((2,))]
    return pl.pallas_call(
        functools.partial(_sb_kernel, tq=tq, n_sub=n_sub, seg=seg, n_past=n_past, layer=layer),
        grid=(n_batch, nq),
        in_specs=[q_spec, d_spec, d_spec, p_spec, p_spec, pl.BlockSpec(cw.shape, lambda b, i: (0, 0))] + far_specs,
        out_specs=q_spec,
        out_shape=jax.ShapeDtypeStruct((n_batch * t_q, W_SB), bf16),
        scratch_shapes=[pltpu.VMEM((n_sub * N_PAIR, tq, LANES), f32),
                        pltpu.VMEM((n_sub * H_SB, tq, LANES), f32)] + far_scratch,
        compiler_params=_params(("parallel", "arbitrary")),
        name="stick_breaking",
    )(q, kd, vd, kp, vp, cw, *far_args)


GLA_PAIRS = H_GLA // 2
GLA_SAFE_DECAY = 60.0


def _gla_tables(chunk):
    idx = np.arange(chunk)
    t, s = idx[:, None], idx[None, :]
    tri = jnp.asarray((s <= t).astype(np.float32), dtype=bf16)
    level = np.where(t > s, np.floor(np.log2(np.maximum(t ^ s, 1))).astype(np.int32), -1)
    level = np.where(t < s, -2, level).astype(np.int32)
    return tri, jnp.asarray(np.concatenate([level, level], axis=1))


def _level_reference(b, h):
    L = b.shape[0]
    if h >= 4:
        return jnp.concatenate(
            [jnp.broadcast_to(b[c0 + h - 1:c0 + h, :], (2 * h, LANES)) for c0 in range(0, L, 2 * h)], axis=0)
    b3 = b.reshape(L // 8, 8, LANES)
    pos = jnp.bitwise_and(lax.broadcasted_iota(jnp.int32, b3.shape, 1), 2 * h - 1)
    up = lambda n: pltpu.roll(b3, n, 1)
    if h == 1:
        ref = jnp.where(pos == 0, b3, up(1))
    else:
        ref = jnp.where(pos == 0, up(7), jnp.where(pos == 1, b3, jnp.where(pos == 2, up(1), up(2))))
    return ref.reshape(L, LANES)


def _two_heads(a, first_head):
    return jnp.concatenate([jnp.where(first_head, a, 0.0).astype(bf16),
                            jnp.where(first_head, 0.0, a).astype(bf16)], axis=0)


def _gla_att_single(q_in, k, b, lev2, first_head):
    a = lax.dot_general(q_in, _two_heads(k * jnp.exp(-b), first_head), NT, preferred_element_type=f32)
    return jnp.where(lev2 >= -1, a, 0.0)


def _gla_att_levels(q, k, b, lev2, first_head, n_lev):
    a = lax.dot_general(q.astype(bf16), _two_heads(k, first_head), NT, preferred_element_type=f32)
    att = jnp.where(lev2 == -1, a, 0.0)
    for lev in range(n_lev):
        d = b - _level_reference(b, 1 << lev)
        e = jnp.exp(-jnp.abs(d))
        is_query = d < 0.0
        q_l = (q * jnp.where(is_query, e, 1.0)).astype(bf16)
        k_l = k * jnp.where(is_query, 1.0, e)
        a = lax.dot_general(q_l, _two_heads(k_l, first_head), NT, preferred_element_type=f32)
        att = jnp.where(lev2 == lev, a, att)
    return att


def _gla_kernel(q_ref, k_ref, v_ref, g_ref, og_ref, s0_ref, tri_ref, lev_ref, gn_ref,
                o_ref, sout_ref, s_ref, att_ref, *, chunk, bb):
    c = pl.program_id(1)
    L = chunk
    n_lev = L.bit_length() - 1
    chains = [(bi, p) for bi in range(bb) for p in range(GLA_PAIRS)]
    zeros_s = jnp.zeros((DK_GLA, DV_GLA), f32)

    @pl.when(c == 0)
    def _():
        for bi, p in chains:
            s0 = s0_ref[bi, p]
            s_ref[bi, p] = jnp.concatenate(
                [jnp.concatenate([s0[:DK_GLA], zeros_s], axis=1),
                 jnp.concatenate([zeros_s, s0[DK_GLA:]], axis=1)], axis=0)

    lane = lax.broadcasted_iota(jnp.int32, (L, LANES), 1)
    first_head = lane < DK_GLA
    lev2 = lev_ref[...]
    tri = tri_ref[...]

    pre = []
    floor = None
    for bi, p in chains:
        lanes = slice(p * LANES, (p + 1) * LANES)
        q = q_ref[bi, :, lanes]
        k = k_ref[bi, :, lanes]
        g = g_ref[bi, :, lanes]
        g_hi = g.astype(bf16)
        g_lo = (g - g_hi.astype(f32)).astype(bf16)
        g2 = jnp.concatenate([g_hi, g_lo], axis=1)
        bs = jnp.dot(tri, g2, preferred_element_type=f32)
        b = bs[:, :LANES] + bs[:, LANES:]
        b_last = b[L - 1:L, :]
        q_in = (q * jnp.exp(b)).astype(bf16)
        k_end = (k * jnp.exp(b_last - b)).astype(bf16)
        pre.append((q, k, g2, b, q_in, k_end))
        floor = b_last if floor is None else jnp.minimum(floor, b_last)
    single = jnp.min(floor) >= -GLA_SAFE_DECAY

    @pl.when(single)
    def _():
        for ci, (q, k, g2, b, q_in, k_end) in enumerate(pre):
            att_ref[ci] = _gla_att_single(q_in, k, b, lev2, first_head)

    @pl.when(jnp.logical_not(single))
    def _():
        for ci, (q, k, g2, b, q_in, k_end) in enumerate(pre):
            att_ref[ci] = _gla_att_levels(q, k, b, lev2, first_head, n_lev)

    rowi = lax.broadcasted_iota(jnp.int32, (2 * DK_GLA, 2 * DV_GLA), 0)
    coli = lax.broadcasted_iota(jnp.int32, (2 * DK_GLA, 2 * DV_GLA), 1)
    own_block = (rowi < DK_GLA) == (coli < DV_GLA)
    ones = jnp.ones((L, LANES), bf16)
    zeros_v = jnp.zeros((L, DV_GLA), bf16)
    gn = gn_ref[...]
    for ci, (bi, p) in enumerate(chains):
        q, k, g2, b, q_in, k_end = pre[ci]
        cols = slice(p * 2 * DV_GLA, (p + 1) * 2 * DV_GLA)
        v = v_ref[bi, :, cols].astype(bf16)
        v_diag = jnp.concatenate([jnp.concatenate([v[:, :DV_GLA], zeros_v], axis=1),
                                  jnp.concatenate([zeros_v, v[:, DV_GLA:]], axis=1)], axis=0)
        s_old = s_ref[bi, p]
        o2 = (jnp.dot(att_ref[ci].astype(bf16), v_diag, preferred_element_type=f32)
              + jnp.dot(q_in, s_old.astype(bf16), preferred_element_type=f32))
        tot = lax.dot_general(g2, ones, TN, preferred_element_type=f32)
        decay = jnp.exp(tot[:LANES] + tot[LANES:])
        upd = lax.dot_general(k_end, v, TN, preferred_element_type=f32)
        s_ref[bi, p] = jnp.concatenate([decay, decay], axis=1) * s_old + jnp.where(own_block, upd, 0.0)
        og = og_ref[bi, :, cols]
        outs = []
        for hh in range(2):
            o = o2[:, hh * DV_GLA:(hh + 1) * DV_GLA]
            ms = jnp.mean(o * o, axis=-1, keepdims=True)
            gate = og[:, hh * DV_GLA:(hh + 1) * DV_GLA]
            outs.append(o * lax.rsqrt(ms + EPS) * gn * (gate * jax.nn.sigmoid(gate)))
        o_ref[bi, :, cols] = jnp.concatenate(outs, axis=1).astype(bf16)

    @pl.when(c == pl.num_programs(1) - 1)
    def _():
        for bi, p in chains:
            s = s_ref[bi, p]
            sout_ref[bi, p] = jnp.concatenate([s[:DK_GLA, :DV_GLA], s[DK_GLA:, DV_GLA:]], axis=0)


def _gla_call(qg, kg, vg, g, og, s0, gla_norm, n_batch, t, chunk, bb):
    nc = t // chunk
    tri, lev2 = _gla_tables(chunk)
    kspec = pl.BlockSpec((bb, chunk, WK_GLA), lambda i, c: (i, c, 0))
    vspec = pl.BlockSpec((bb, chunk, WV_GLA), lambda i, c: (i, c, 0))
    sspec = pl.BlockSpec((bb, GLA_PAIRS, 2 * DK_GLA, DV_GLA), lambda i, c: (i, 0, 0, 0))
    const = lambda a: pl.BlockSpec(a.shape, lambda i, c: (0,) * a.ndim)
    s0 = s0.reshape(n_batch, GLA_PAIRS, 2 * DK_GLA, DV_GLA)
    k3 = lambda a: a.reshape(n_batch, t, WK_GLA)
    v3 = lambda a: a.reshape(n_batch, t, WV_GLA)
    o, s_out = pl.pallas_call(
        functools.partial(_gla_kernel, chunk=chunk, bb=bb),
        grid=(n_batch // bb, nc),
        in_specs=[kspec, kspec, vspec, kspec, vspec, sspec, const(tri), const(lev2), const(gla_norm)],
        out_specs=[vspec, sspec],
        out_shape=[jax.ShapeDtypeStruct((n_batch, t, WV_GLA), bf16),
                   jax.ShapeDtypeStruct(s0.shape, f32)],
        scratch_shapes=[pltpu.VMEM((bb, GLA_PAIRS, 2 * DK_GLA, 2 * DV_GLA), f32),
                        pltpu.VMEM((bb * GLA_PAIRS, chunk, 2 * chunk), f32)],
        compiler_params=_params(("parallel", "arbitrary")),
        name="gla",
    )(k3(qg), k3(kg), v3(vg), k3(g), v3(og), s0, tri, lev2, gla_norm)
    return o.reshape(n_batch * t, WV_GLA), s_out.reshape(n_batch, H_GLA, DK_GLA, DV_GLA)


def _mlp_rows(rows, weights, y_ref, ff_chunk):
    a_ref, o_ref, x_ref, mix_gate_ref, sh_ref, sc_ref, gate_ref = rows
    nw_ref, wo_ref, wu_ref, wd_ref = weights
    m = (jnp.dot(a_ref[...], wo_ref[:W_SB, :], preferred_element_type=f32)
         + jnp.dot(o_ref[...], wo_ref[W_SB:, :], preferred_element_type=f32))
    x = x_ref[...] + mix_gate_ref[...] * m
    ms = jnp.mean(x * x, axis=-1, keepdims=True)
    y = x * lax.rsqrt(ms + EPS) * nw_ref[...]
    h = (y * (1.0 + sc_ref[...]) + sh_ref[...]).astype(bf16)
    acc = jnp.zeros(x.shape, f32)
    for c0 in range(0, D_FF, ff_chunk):
        u = jnp.dot(h, wu_ref[:, c0:c0 + ff_chunk], preferred_element_type=f32)
        u = jnp.square(jnp.maximum(u, 0.0)).astype(bf16)
        acc = acc + jnp.dot(u, wd_ref[c0:c0 + ff_chunk, :], preferred_element_type=f32)
    y_ref[...] = x + gate_ref[...] * acc


def _mlp_kernel(*refs, ff_chunk):
    big, weights, small = refs[:7], refs[7:11], refs[11:18]
    y_ref, y_small_ref = refs[18:]
    _mlp_rows(big, weights, y_ref, ff_chunk)

    @pl.when(pl.program_id(0) == pl.num_programs(0) - 1)
    def _():
        _mlp_rows(small, weights, y_small_ref, ff_chunk)


def _mlp_call(big, small, lw, tm, blocks_per_batch):
    a, o, x2d, mod = big
    a_s, o_s, x_s, mod_s = small
    n, n_s = x2d.shape[0], x_s.shape[0]
    row = lambda w: pl.BlockSpec((tm, w), lambda i: (i, 0))
    const = lambda arr: pl.BlockSpec(arr.shape, lambda i: (0,) * arr.ndim)
    piece = lambda k: _mod_spec(mod.shape[1], blocks_per_batch, k)
    piece_s = lambda k: pl.BlockSpec((None, n_s, D_MODEL), lambda i: (0, 0, k))
    return pl.pallas_call(
        functools.partial(_mlp_kernel, ff_chunk=1024),
        grid=(n // tm,),
        in_specs=[row(W_SB), row(WV_GLA), row(D_MODEL), piece(2), piece(3), piece(4), piece(5),
                  const(lw["norm_mlp"]), const(lw["w_out"]), const(lw["w_up"]), const(lw["w_down"]),
                  const(a_s), const(o_s), const(x_s), piece_s(2), piece_s(3), piece_s(4), piece_s(5)],
        out_specs=[row(D_MODEL), pl.BlockSpec((n_s, D_MODEL), lambda i: (0, 0))],
        out_shape=[jax.ShapeDtypeStruct((n, D_MODEL), f32), jax.ShapeDtypeStruct((n_s, D_MODEL), f32)],
        compiler_params=_params(("arbitrary",)),
        name="out_mlp",
    )(a, o, x2d, mod, mod, mod, mod, lw["norm_mlp"], lw["w_out"], lw["w_up"], lw["w_down"],
      a_s, o_s, x_s, mod_s, mod_s, mod_s, mod_s)


def _layer_weights(l, w_in, w_out, w_up, w_down, norm_mix, norm_mlp, q_norm, k_norm, w_gate, b_gate,
                   gla_norm):
    wi = w_in[l]
    c_g = 3 * W_SB
    c_gate = c_g + 2 * WK_GLA + WV_GLA
    w_main = jnp.concatenate(
        [wi[:, c_g:c_gate], wi[:, c_gate + GATE_RANK:], wi[:, c_gate:c_gate + GATE_RANK],
         jnp.zeros((D_MODEL, GATE_PAD - GATE_RANK), f32)], axis=1).astype(bf16)
    w_qkvt = wi[:, :c_g].T.astype(bf16)
    wg = jnp.concatenate([w_gate[l], jnp.zeros((GATE_PAD - GATE_RANK, WK_GLA), f32)], axis=0).astype(bf16)
    return {
        "w_main": w_main, "w_qkvt": w_qkvt, "w_gate": wg,
        "b_gate": b_gate[l].reshape(1, WK_GLA),
        "q_norm": jnp.tile(q_norm[l], H_SB),
        "k_norm": jnp.tile(k_norm[l], H_SB),
        "norm_mix": norm_mix[l].reshape(1, D_MODEL),
        "norm_mlp": norm_mlp[l].reshape(1, D_MODEL),
        "gla_norm": gla_norm[l].reshape(1, DV_GLA),
        "w_out": w_out[l].astype(bf16), "w_up": w_up[l].astype(bf16), "w_down": w_down[l].astype(bf16),
    }


def _group_layer(x2d, mod, lw, cw, n_batch, t, past_kt, past_vt, layer, s0, tm, blocks_per_batch, chunk,
                 gla_bb, prev_kv=()):
    q, kt, vt, ktb, vtb, qg, kg, vg, g, og = _in_call(x2d, mod, lw, tm, blocks_per_batch, prev_kv)
    if past_kt is None:
        a = _sb_call(q, ktb, vtb, ktb, vtb, cw, n_batch=n_batch, t_q=t, prompt=True, layer=layer)
    else:
        a = _sb_call(q, ktb, vtb, past_kt, past_vt, cw, n_batch=n_batch, t_q=t, prompt=False, layer=layer)
    o, s_new = _gla_call(qg, kg, vg, g, og, s0, lw["gla_norm"], n_batch, t, chunk, gla_bb)
    return (a, o, x2d, mod), kt, vt, s_new


def _heads_last(kt, n_batch, t):
    layers, nb = kt.shape[:2]
    return kt.reshape(layers, nb, H_SB, D_SB, -1).transpose(0, 1, 4, 2, 3).reshape(layers, n_batch, t, H_SB, D_SB)


def kernel(x_prompt, x_sample, c_prompt, c_sample, cache_k, cache_v, state_gla, w_ada, b_ada, norm_mix,
           norm_mlp, w_in, q_norm, k_norm, w_gate, b_gate, gla_norm, w_out, w_up, w_down):
    bp, tp, _ = x_prompt.shape
    bs, ts, _ = x_sample.shape
    past = cache_k.shape[2]
    n_c = bp + bs
    c_rows = -(-n_c // 16) * 16
    c_all = jnp.concatenate([c_prompt, c_sample, jnp.zeros((c_rows - n_c, D_MODEL), f32)], axis=0)
    mods = _ada_call(c_all, w_ada, b_ada)

    kk = np.arange(KB)
    half = kk[:, None] >= kk[None, :]
    cw = jnp.asarray(np.concatenate([half, half], axis=0).astype(np.float32), dtype=bf16)

    cache_kt = cache_k.transpose(0, 1, 3, 4, 2).reshape(DEPTH, bs, W_SB, past)
    cache_vt = cache_v.transpose(0, 1, 3, 4, 2).reshape(DEPTH, bs, W_SB, past)

    tm_p, tm_s = 512, bs * ts
    xp = x_prompt.reshape(bp * tp, D_MODEL)
    xs = x_sample.reshape(bs * ts, D_MODEL)
    sp_l, ks_l, vs_l, ss_l = [], [], [], []
    prompt_kv = ()
    for l in range(DEPTH):
        lw = _layer_weights(l, w_in, w_out, w_up, w_down, norm_mix, norm_mlp, q_norm, k_norm, w_gate,
                            b_gate, gla_norm)
        mod_p = mods[l, :bp].reshape(bp, 1, N_MOD * D_MODEL)
        mod_s = jnp.repeat(mods[l, bp:n_c], ts, axis=0).reshape(1, bs * ts, N_MOD * D_MODEL)
        s0_p = jnp.zeros((bp, H_GLA, DK_GLA, DV_GLA), f32)
        mixed_p, kp, vp, sp = _group_layer(xp, mod_p, lw, cw, bp, tp, None, None, l, s0_p, tm_p, tp // tm_p, 256,
                                           bp, prompt_kv)
        prompt_kv = (kp, vp)
        mixed_s, kn, vn, sn = _group_layer(xs, mod_s, lw, cw, bs, ts, cache_kt, cache_vt, l,
                                           state_gla[l].astype(f32), tm_s, 1, ts, 4)
        xp, xs = _mlp_call(mixed_p, mixed_s, lw, tm_p, tp // tm_p)
        sp_l.append(sp)
        ks_l.append(_heads_last(kn, bs, ts)[0])
        vs_l.append(_heads_last(vn, bs, ts)[0])
        ss_l.append(sn)
    return (xp.reshape(bp, tp, D_MODEL), xs.reshape(bs, ts, D_MODEL),
            _heads_last(kp, bp, tp), _heads_last(vp, bp, tp), jnp.stack(sp_l),
            jnp.stack(ks_l), jnp.stack(vs_l), jnp.stack(ss_l))
```

```python
import functools

import numpy as np
import jax
import jax.numpy as jnp
from jax import lax
from jax.experimental import pallas as pl
from jax.experimental.pallas import tpu as pltpu

D_MODEL = 1024
DEPTH = 2
H_SB = 8
D_SB = 64
W_SB = H_SB * D_SB
H_GLA = 4
DK_GLA = 64
DV_GLA = 128
WK_GLA = H_GLA * DK_GLA
WV_GLA = H_GLA * DV_GLA
GATE_RANK = 16
GATE_TAU = 16.0
D_FF = 4 * D_MODEL
N_MOD = 6
EPS = 1e-6

LANES = 128
KB = 128
SB_FIRST_PASS = 3
SB_LOOP_GROUP = 1
SB_QUERY_BLOCKS = 4
N_PAIR = H_SB // 2
GATE_PAD = LANES
VMEM_LIMIT = 56 * 1024 * 1024
SURVIVAL_ZERO = 104.0
MASKED = -1e30

f32 = jnp.float32
bf16 = jnp.bfloat16
NT = (((1,), (1,)), ((), ()))
TN = (((0,), (0,)), ((), ()))


def _params(sem):
    return pltpu.CompilerParams(dimension_semantics=sem, vmem_limit_bytes=VMEM_LIMIT)


def _log_sigmoid(x):
    return jnp.minimum(x, 0.0) - jnp.log1p(jnp.exp(-jnp.abs(x)))


def _ada_kernel(c_ref, w_ref, b_ref, o_ref):
    c = c_ref[...]
    s = (c * jax.nn.sigmoid(c)).astype(bf16)
    o_ref[...] = jnp.dot(s, w_ref[...].astype(bf16), preferred_element_type=f32) + b_ref[...]


def _ada_call(c_all, w_ada, b_ada):
    rows = c_all.shape[0]
    return pl.pallas_call(
        _ada_kernel,
        grid=(DEPTH, N_MOD),
        in_specs=[
            pl.BlockSpec((rows, D_MODEL), lambda l, p: (0, 0)),
            pl.BlockSpec((None, D_MODEL, D_MODEL), lambda l, p: (l, 0, p)),
            pl.BlockSpec((None, 1, D_MODEL), lambda l, p: (l, 0, p)),
        ],
        out_specs=pl.BlockSpec((None, rows, D_MODEL), lambda l, p: (l, 0, p)),
        out_shape=jax.ShapeDtypeStruct((DEPTH, rows, N_MOD * D_MODEL), f32),
        compiler_params=_params(("parallel", "parallel")),
        name="ada_mod",
    )(c_all, w_ada, b_ada.reshape(DEPTH, 1, N_MOD * D_MODEL))


def _mod_spec(rows_per_block, blocks_per_batch, piece):
    return pl.BlockSpec((None, rows_per_block, D_MODEL),
                        lambda i: (i // blocks_per_batch, 0, piece))


def _layer_spec(stack, layer):
    zeros = (0,) * (stack.ndim - 1)
    return pl.BlockSpec((None,) + stack.shape[1:], lambda *_: (layer,) + zeros, pipeline_mode=pl.Buffered(1))


def _head_rms(t):
    t3 = t.reshape(H_SB, D_SB, t.shape[1])
    ms = jnp.mean(t3 * t3, axis=1, keepdims=True)
    return (t3 * lax.rsqrt(ms + EPS)).reshape(t.shape)


def _in_kernel(*refs, n_prev):
    (x_ref, sh_ref, sc_ref, nw_ref, w_ref, wqkv_ref, qn_ref, kn_ref, wg_ref, bg_ref) = refs[:10]
    prev_refs = refs[10:10 + 2 * min(n_prev, 1)]
    (q_ref, kt_ref, vt_ref, ktb_ref, vtb_ref, qg_ref, kg_ref, vg_ref, g_ref, og_ref) = refs[len(refs) - 10:]
    for prev_ref, out_ref in zip(prev_refs, (kt_ref, vt_ref)):
        out_ref[:n_prev] = prev_ref[...]
    x = x_ref[...]
    ms = jnp.mean(x * x, axis=-1, keepdims=True)
    y = x * lax.rsqrt(ms + EPS) * nw_ref[...]
    h = (y * (1.0 + sc_ref[...]) + sh_ref[...]).astype(bf16)

    def proj(lo, hi):
        return jnp.dot(h, w_ref[:, lo:hi], preferred_element_type=f32)

    qkvt = lax.dot_general(wqkv_ref[...], h, NT, preferred_element_type=f32)
    qt = _head_rms(qkvt[:W_SB]) * qn_ref[...] * (D_SB ** -0.5)
    q_ref[...] = qt.T.astype(bf16)
    kt = _head_rms(qkvt[W_SB:2 * W_SB]) * kn_ref[...]
    kt_ref[n_prev] = kt
    ktb_ref[...] = kt.astype(bf16)
    vt = qkvt[2 * W_SB:]
    vt_ref[n_prev] = vt
    vtb_ref[...] = vt.astype(bf16)

    o = 0
    qg_ref[...] = proj(o, o + WK_GLA) * (DK_GLA ** -0.5)
    o += WK_GLA
    kg_ref[...] = proj(o, o + WK_GLA)
    o += WK_GLA
    vg_ref[...] = proj(o, o + WV_GLA).astype(bf16)
    o += WV_GLA
    og_ref[...] = proj(o, o + WV_GLA)
    o += WV_GLA
    gr = proj(o, o + GATE_PAD).astype(bf16)
    xg = jnp.dot(gr, wg_ref[...], preferred_element_type=f32) + bg_ref[...]
    g_ref[...] = _log_sigmoid(xg) * (1.0 / GATE_TAU)


def _in_call(x2d, mod, lw, tm, blocks_per_batch, prev_kv):
    n = x2d.shape[0]
    r = mod.shape[1]
    n_batch = n // (tm * blocks_per_batch)
    t = tm * blocks_per_batch
    n_prev = prev_kv[0].shape[0] if prev_kv else 0
    row = lambda w: pl.BlockSpec((tm, w), lambda i: (i, 0))
    col_map = lambda i: (i // blocks_per_batch, 0, i % blocks_per_batch)
    col = pl.BlockSpec((None, W_SB, tm), col_map)
    stack = lambda depth: pl.BlockSpec((depth, None, W_SB, tm), lambda i: (0,) + col_map(i))
    const = lambda a: pl.BlockSpec(a.shape, lambda i: (0,) * a.ndim)
    qn_t = jnp.broadcast_to(lw["q_norm"].reshape(W_SB, 1), (W_SB, tm))
    kn_t = jnp.broadcast_to(lw["k_norm"].reshape(W_SB, 1), (W_SB, tm))
    rows_out = [(WK_GLA, f32), (WK_GLA, f32), (WV_GLA, bf16), (WK_GLA, f32), (WV_GLA, f32)]
    kv_stack = jax.ShapeDtypeStruct((n_prev + 1, n_batch, W_SB, t), f32)
    kv_b = jax.ShapeDtypeStruct((n_batch, W_SB, t), bf16)
    return pl.pallas_call(
        functools.partial(_in_kernel, n_prev=n_prev),
        grid=(n // tm,),
        in_specs=[row(D_MODEL), _mod_spec(r, blocks_per_batch, 0), _mod_spec(r, blocks_per_batch, 1),
                  const(lw["norm_mix"]), _layer_spec(lw["w_main"], lw["layer"]),
                  _layer_spec(lw["w_qkvt"], lw["layer"]),
                  const(qn_t), const(kn_t), const(lw["w_gate"]), const(lw["b_gate"])]
        + [stack(n_prev)] * len(prev_kv),
        out_specs=[row(W_SB), stack(n_prev + 1), stack(n_prev + 1), col, col] + [row(w) for w, _ in rows_out],
        out_shape=[jax.ShapeDtypeStruct((n, W_SB), bf16), kv_stack, kv_stack, kv_b, kv_b]
        + [jax.ShapeDtypeStruct((n, w), dt) for w, dt in rows_out],
        compiler_params=_params(("parallel",)),
        name="in_proj",
    )(x2d, mod, mod, lw["norm_mix"], lw["w_main"], lw["w_qkvt"], qn_t, kn_t,
      lw["w_gate"], lw["b_gate"], *prev_kv)


def _pair_diag(a):
    z = jnp.zeros((D_SB, a.shape[1]), a.dtype)
    top = jnp.concatenate([a[:D_SB], z], axis=0)
    bot = jnp.concatenate([z, a[D_SB:]], axis=0)
    return jnp.concatenate([top, bot], axis=1)


def _sb_scores(q_pair, k_tiles, biases):
    tiles = []
    for kt, bias in zip(k_tiles, biases):
        z2 = jnp.dot(q_pair, _pair_diag(kt), preferred_element_type=f32)
        for hh in range(2):
            z = z2[:, hh * KB:(hh + 1) * KB] + bias
            sp = jnp.maximum(z, 0.0) + jnp.log(1.0 + jnp.exp(-jnp.abs(z)))
            hi = sp.astype(bf16)
            lo = (sp - hi.astype(f32)).astype(bf16)
            tiles.append((z, jnp.concatenate([hi, lo], axis=1)))
    return tiles


def _sb_apply(tiles, cs, v_tiles, carries, acc, tq):
    new_carries = []
    ws = [[None, None] for _ in v_tiles]
    for hh in range(2):
        c = carries[hh]
        for bi in range(len(v_tiles)):
            ti = bi * 2 + hh
            part = cs[ti * tq:(ti + 1) * tq]
            total = jnp.broadcast_to(part[:, :1], (tq, KB))
            ws[bi][hh] = jnp.exp(tiles[ti][0] - (part if c is None else c + part)).astype(bf16)
            c = total if c is None else c + total
        new_carries.append(c)
    for bi, vt in enumerate(v_tiles):
        w_pair = jnp.concatenate(ws[bi], axis=1)
        pv = lax.dot_general(w_pair, _pair_diag(vt), NT, preferred_element_type=f32)
        acc = pv if acc is None else acc + pv
    return acc, new_carries


def _sb_kernel(q_ref, kd_ref, vd_ref, kp_ref, vp_ref, cw_ref, *rest, tq, n_sub, seg, n_past, layer):
    if n_past is None:
        o_ref, acc_ref, carry_ref = rest
    else:
        kfar_ref, vfar_ref, o_ref, acc_ref, carry_ref, kbuf_ref, vbuf_ref, sem = rest
    b = pl.program_id(0)
    step = pl.program_id(1)
    row = lax.broadcasted_iota(jnp.int32, (tq, KB), 0)
    col = lax.broadcasted_iota(jnp.int32, (tq, KB), 1)
    seg_id = b % (KB // seg)
    seg_shift = seg.bit_length() - 1
    own = jnp.logical_and(lax.shift_right_logical(col, seg_shift) == seg_id,
                          jnp.bitwise_and(col, seg - 1) < row)
    bias0 = jnp.where(own, 0.0, MASKED)
    cw = cw_ref[...]
    pair_rows = [slice(p * KB, (p + 1) * KB) for p in range(N_PAIR)]

    def past_block(j, first_pass):
        if n_past is not None and first_pass:
            c0 = (j - (n_past - (SB_FIRST_PASS - 1))) * KB
            return kp_ref[:, c0:c0 + KB].astype(bf16), vp_ref[:, c0:c0 + KB].astype(bf16), 0.0
        jc = pl.multiple_of(jnp.maximum(j, 0) * KB, KB)
        if n_past is None:
            kt, vt = kp_ref[b, :, pl.ds(jc, KB)], vp_ref[b, :, pl.ds(jc, KB)]
        else:
            copies = [pltpu.make_async_copy(src.at[layer, b, :, pl.ds(jc, KB)], dst, sem.at[i])
                      for i, (src, dst) in enumerate(((kfar_ref, kbuf_ref), (vfar_ref, vbuf_ref)))]
            for cp in copies:
                cp.start()
            for cp in copies:
                cp.wait()
            kt, vt = kbuf_ref[...], vbuf_ref[...]
        return kt.astype(bf16), vt.astype(bf16), jnp.where(j < 0, MASKED, 0.0)

    def run_group(subs, first):
        order = [(s, p) for s, _ in subs for p in range(N_PAIR)]
        blocks_of = dict(subs)
        tiles = {(s, p): _sb_scores(q_ref[s * tq:(s + 1) * tq, pair_rows[p]],
                                    [kt[pair_rows[p]] for kt, _, _ in blocks_of[s]],
                                    [bias for _, _, bias in blocks_of[s]]) for s, p in order}
        cs = jnp.dot(jnp.concatenate([t[1] for key in order for t in tiles[key]], axis=0), cw,
                     preferred_element_type=f32)
        start, floor = 0, {}
        for s, p in order:
            n_rows = 2 * len(blocks_of[s]) * tq
            carries = (None, None) if first else (carry_ref[s * H_SB + 2 * p], carry_ref[s * H_SB + 2 * p + 1])
            acc, cr = _sb_apply(tiles[s, p], cs[start:start + n_rows], [vt[pair_rows[p]] for _, vt, _ in blocks_of[s]],
                                carries, None if first else acc_ref[s * N_PAIR + p], tq)
            start += n_rows
            acc_ref[s * N_PAIR + p] = acc
            carry_ref[s * H_SB + 2 * p] = cr[0]
            carry_ref[s * H_SB + 2 * p + 1] = cr[1]
            m = jnp.minimum(cr[0], cr[1])
            floor[s] = m if s not in floor else jnp.minimum(floor[s], m)
        return {s: (jnp.min(m) < SURVIVAL_ZERO).astype(jnp.int32) for s, m in floor.items()}

    n_prev = [step * n_sub + s if n_past is None else n_past for s in range(n_sub)]
    first = [(s, [(kd_ref[:, s * KB:(s + 1) * KB], vd_ref[:, s * KB:(s + 1) * KB], bias0)]
              + [past_block(n_prev[s] - 1 - d, True) for d in range(SB_FIRST_PASS - 1)]) for s in range(n_sub)]
    alive0 = run_group(first, True)

    def cond(st):
        j, alive = st
        return jnp.logical_and(j >= 0, alive > 0)

    for s in range(n_sub):
        def body(st, s=s):
            j, _ = st
            blocks = [past_block(j - d, False) for d in range(SB_LOOP_GROUP)]
            return j - SB_LOOP_GROUP, run_group([(s, blocks)], False)[s]

        lax.while_loop(cond, body, (n_prev[s] - SB_FIRST_PASS, alive0[s]))
        o_ref[s * tq:(s + 1) * tq, :] = jnp.concatenate(
            [acc_ref[s * N_PAIR + p] for p in range(N_PAIR)], axis=1).astype(bf16)


def _sb_call(q, kd, vd, kp, vp, cw, *, n_batch, t_q, prompt, layer):
    tq = min(KB, t_q)
    n_sub = SB_QUERY_BLOCKS if prompt else 1
    nq = t_q // (tq * n_sub)
    q_spec = pl.BlockSpec((n_sub * tq, W_SB), lambda b, i: (b * nq + i, 0))
    if prompt:
        d_spec = pl.BlockSpec((None, W_SB, n_sub * KB), lambda b, i: (b, 0, i))
        p_spec = pl.BlockSpec(kp.shape, lambda b, i: (0, 0, 0))
        seg, n_past = KB, None
        far_specs, far_args, far_scratch = [], (), []
    else:
        past = kp.shape[-1]
        tail = (SB_FIRST_PASS - 1) * KB
        assert past % tail == 0 and SB_LOOP_GROUP == 1
        d_spec = pl.BlockSpec((None, W_SB, KB), lambda b, i: (0, 0, (b * t_q) // KB))
        p_spec = pl.BlockSpec((None, None, W_SB, tail), lambda b, i: (layer, b, 0, past // tail - 1))
        seg, n_past = t_q, past // KB
        far_specs, far_args = [pl.BlockSpec(memory_space=pl.ANY)] * 2, (kp, vp)
        far_scratch = [pltpu.VMEM((W_SB, KB), kp.dtype), pltpu.VMEM((W_SB, KB), vp.dtype),
                       pltpu.SemaphoreType.DMA((2,))]
    return pl.pallas_call(
        functools.partial(_sb_kernel, tq=tq, n_sub=n_sub, seg=seg, n_past=n_past, layer=layer),
        grid=(n_batch, nq),
        in_specs=[q_spec, d_spec, d_spec, p_spec, p_spec, pl.BlockSpec(cw.shape, lambda b, i: (0, 0))] + far_specs,
        out_specs=q_spec,
        out_shape=jax.ShapeDtypeStruct((n_batch * t_q, W_SB), bf16),
        scratch_shapes=[pltpu.VMEM((n_sub * N_PAIR, tq, LANES), f32),
                        pltpu.VMEM((n_sub * H_SB, tq, LANES), f32)] + far_scratch,
        compiler_params=_params(("parallel", "arbitrary")),
        name="stick_breaking",
    )(q, kd, vd, kp, vp, cw, *far_args)


GLA_PAIRS = H_GLA // 2
GLA_SAFE_DECAY = 60.0


def _gla_tables(chunk):
    idx = np.arange(chunk)
    t, s = idx[:, None], idx[None, :]
    tri = jnp.asarray((s <= t).astype(np.float32), dtype=bf16)
    level = np.where(t > s, np.floor(np.log2(np.maximum(t ^ s, 1))).astype(np.int32), -1)
    level = np.where(t < s, -2, level).astype(np.int32)
    return tri, jnp.asarray(np.concatenate([level, level], axis=1))


def _level_reference(b, h):
    L = b.shape[0]
    if h >= 4:
        return jnp.concatenate(
            [jnp.broadcast_to(b[c0 + h - 1:c0 + h, :], (2 * h, LANES)) for c0 in range(0, L, 2 * h)], axis=0)
    b3 = b.reshape(L // 8, 8, LANES)
    pos = jnp.bitwise_and(lax.broadcasted_iota(jnp.int32, b3.shape, 1), 2 * h - 1)
    up = lambda n: pltpu.roll(b3, n, 1)
    if h == 1:
        ref = jnp.where(pos == 0, b3, up(1))
    else:
        ref = jnp.where(pos == 0, up(7), jnp.where(pos == 1, b3, jnp.where(pos == 2, up(1), up(2))))
    return ref.reshape(L, LANES)


def _two_heads(a, first_head):
    return jnp.concatenate([jnp.where(first_head, a, 0.0).astype(bf16),
                            jnp.where(first_head, 0.0, a).astype(bf16)], axis=0)


def _gla_att_single(q_in, k, b, lev2, first_head):
    a = lax.dot_general(q_in, _two_heads(k * jnp.exp(-b), first_head), NT, preferred_element_type=f32)
    return jnp.where(lev2 >= -1, a, 0.0)


def _gla_att_levels(q, k, b, lev2, first_head, n_lev):
    a = lax.dot_general(q.astype(bf16), _two_heads(k, first_head), NT, preferred_element_type=f32)
    att = jnp.where(lev2 == -1, a, 0.0)
    for lev in range(n_lev):
        d = b - _level_reference(b, 1 << lev)
        e = jnp.exp(-jnp.abs(d))
        is_query = d < 0.0
        q_l = (q * jnp.where(is_query, e, 1.0)).astype(bf16)
        k_l = k * jnp.where(is_query, 1.0, e)
        a = lax.dot_general(q_l, _two_heads(k_l, first_head), NT, preferred_element_type=f32)
        att = jnp.where(lev2 == lev, a, att)
    return att


def _gla_kernel(q_ref, k_ref, v_ref, g_ref, og_ref, s0_ref, tri_ref, lev_ref, gn_ref,
                o_ref, sout_ref, s_ref, att_ref, *, chunk, bb):
    c = pl.program_id(1)
    L = chunk
    n_lev = L.bit_length() - 1
    chains = [(bi, p) for bi in range(bb) for p in range(GLA_PAIRS)]
    zeros_s = jnp.zeros((DK_GLA, DV_GLA), f32)

    @pl.when(c == 0)
    def _():
        for bi, p in chains:
            s0 = s0_ref[bi, p]
            s_ref[bi, p] = jnp.concatenate(
                [jnp.concatenate([s0[:DK_GLA], zeros_s], axis=1),
                 jnp.concatenate([zeros_s, s0[DK_GLA:]], axis=1)], axis=0)

    lane = lax.broadcasted_iota(jnp.int32, (L, LANES), 1)
    first_head = lane < DK_GLA
    lev2 = lev_ref[...]
    tri = tri_ref[...]

    pre = []
    floor = None
    for bi, p in chains:
        lanes = slice(p * LANES, (p + 1) * LANES)
        q = q_ref[bi, :, lanes]
        k = k_ref[bi, :, lanes]
        g = g_ref[bi, :, lanes]
        g_hi = g.astype(bf16)
        g_lo = (g - g_hi.astype(f32)).astype(bf16)
        g2 = jnp.concatenate([g_hi, g_lo], axis=1)
        bs = jnp.dot(tri, g2, preferred_element_type=f32)
        b = bs[:, :LANES] + bs[:, LANES:]
        b_last = b[L - 1:L, :]
        q_in = (q * jnp.exp(b)).astype(bf16)
        k_end = (k * jnp.exp(b_last - b)).astype(bf16)
        pre.append((q, k, g2, b, q_in, k_end))
        floor = b_last if floor is None else jnp.minimum(floor, b_last)
    single = jnp.min(floor) >= -GLA_SAFE_DECAY

    @pl.when(single)
    def _():
        for ci, (q, k, g2, b, q_in, k_end) in enumerate(pre):
            att_ref[ci] = _gla_att_single(q_in, k, b, lev2, first_head)

    @pl.when(jnp.logical_not(single))
    def _():
        for ci, (q, k, g2, b, q_in, k_end) in enumerate(pre):
            att_ref[ci] = _gla_att_levels(q, k, b, lev2, first_head, n_lev)

    rowi = lax.broadcasted_iota(jnp.int32, (2 * DK_GLA, 2 * DV_GLA), 0)
    coli = lax.broadcasted_iota(jnp.int32, (2 * DK_GLA, 2 * DV_GLA), 1)
    own_block = (rowi < DK_GLA) == (coli < DV_GLA)
    ones = jnp.ones((L, LANES), bf16)
    zeros_v = jnp.zeros((L, DV_GLA), bf16)
    gn = gn_ref[...]
    for ci, (bi, p) in enumerate(chains):
        q, k, g2, b, q_in, k_end = pre[ci]
        cols = slice(p * 2 * DV_GLA, (p + 1) * 2 * DV_GLA)
        v = v_ref[bi, :, cols].astype(bf16)
        v_diag = jnp.concatenate([jnp.concatenate([v[:, :DV_GLA], zeros_v], axis=1),
                                  jnp.concatenate([zeros_v, v[:, DV_GLA:]], axis=1)], axis=0)
        s_old = s_ref[bi, p]
        o2 = (jnp.dot(att_ref[ci].astype(bf16), v_diag, preferred_element_type=f32)
              + jnp.dot(q_in, s_old.astype(bf16), preferred_element_type=f32))
        tot = lax.dot_general(g2, ones, TN, preferred_element_type=f32)
        decay = jnp.exp(tot[:LANES] + tot[LANES:])
        upd = lax.dot_general(k_end, v, TN, preferred_element_type=f32)
        s_ref[bi, p] = jnp.concatenate([decay, decay], axis=1) * s_old + jnp.where(own_block, upd, 0.0)
        og = og_ref[bi, :, cols]
        outs = []
        for hh in range(2):
            o = o2[:, hh * DV_GLA:(hh + 1) * DV_GLA]
            ms = jnp.mean(o * o, axis=-1, keepdims=True)
            gate = og[:, hh * DV_GLA:(hh + 1) * DV_GLA]
            outs.append(o * lax.rsqrt(ms + EPS) * gn * (gate * jax.nn.sigmoid(gate)))
        o_ref[bi, :, cols] = jnp.concatenate(outs, axis=1).astype(bf16)

    @pl.when(c == pl.num_programs(1) - 1)
    def _():
        for bi, p in chains:
            s = s_ref[bi, p]
            sout_ref[bi, p] = jnp.concatenate([s[:DK_GLA, :DV_GLA], s[DK_GLA:, DV_GLA:]], axis=0)


def _gla_call(qg, kg, vg, g, og, s0, gla_norm, n_batch, t, chunk, bb):
    nc = t // chunk
    tri, lev2 = _gla_tables(chunk)
    kspec = pl.BlockSpec((bb, chunk, WK_GLA), lambda i, c: (i, c, 0))
    vspec = pl.BlockSpec((bb, chunk, WV_GLA), lambda i, c: (i, c, 0))
    sspec = pl.BlockSpec((bb, GLA_PAIRS, 2 * DK_GLA, DV_GLA), lambda i, c: (i, 0, 0, 0))
    const = lambda a: pl.BlockSpec(a.shape, lambda i, c: (0,) * a.ndim)
    s0 = s0.reshape(n_batch, GLA_PAIRS, 2 * DK_GLA, DV_GLA)
    k3 = lambda a: a.reshape(n_batch, t, WK_GLA)
    v3 = lambda a: a.reshape(n_batch, t, WV_GLA)
    o, s_out = pl.pallas_call(
        functools.partial(_gla_kernel, chunk=chunk, bb=bb),
        grid=(n_batch // bb, nc),
        in_specs=[kspec, kspec, vspec, kspec, vspec, sspec, const(tri), const(lev2), const(gla_norm)],
        out_specs=[vspec, sspec],
        out_shape=[jax.ShapeDtypeStruct((n_batch, t, WV_GLA), bf16),
                   jax.ShapeDtypeStruct(s0.shape, f32)],
        scratch_shapes=[pltpu.VMEM((bb, GLA_PAIRS, 2 * DK_GLA, 2 * DV_GLA), f32),
                        pltpu.VMEM((bb * GLA_PAIRS, chunk, 2 * chunk), f32)],
        compiler_params=_params(("parallel", "arbitrary")),
        name="gla",
    )(k3(qg), k3(kg), v3(vg), k3(g), v3(og), s0, tri, lev2, gla_norm)
    return o.reshape(n_batch * t, WV_GLA), s_out.reshape(n_batch, H_GLA, DK_GLA, DV_GLA)


def _mlp_rows(rows, weights, y_ref, ff_chunk):
    a_ref, o_ref, x_ref, mix_gate_ref, sh_ref, sc_ref, gate_ref = rows
    nw_ref, wo_ref, wu_ref, wd_ref = weights
    m = (jnp.dot(a_ref[...], wo_ref[:W_SB, :], preferred_element_type=f32)
         + jnp.dot(o_ref[...], wo_ref[W_SB:, :], preferred_element_type=f32))
    x = x_ref[...] + mix_gate_ref[...] * m
    ms = jnp.mean(x * x, axis=-1, keepdims=True)
    y = x * lax.rsqrt(ms + EPS) * nw_ref[...]
    h = (y * (1.0 + sc_ref[...]) + sh_ref[...]).astype(bf16)
    acc = jnp.zeros(x.shape, f32)
    for c0 in range(0, D_FF, ff_chunk):
        u = jnp.dot(h, wu_ref[:, c0:c0 + ff_chunk], preferred_element_type=f32)
        u = jnp.square(jnp.maximum(u, 0.0)).astype(bf16)
        acc = acc + jnp.dot(u, wd_ref[c0:c0 + ff_chunk, :], preferred_element_type=f32)
    y_ref[...] = x + gate_ref[...] * acc


def _mlp_kernel(*refs, ff_chunk):
    big, weights, small = refs[:7], refs[7:11], refs[11:18]
    y_ref, y_small_ref = refs[18:]
    _mlp_rows(big, weights, y_ref, ff_chunk)

    @pl.when(pl.program_id(0) == pl.num_programs(0) - 1)
    def _():
        _mlp_rows(small, weights, y_small_ref, ff_chunk)


def _mlp_call(big, small, lw, tm, blocks_per_batch):
    a, o, x2d, mod = big
    a_s, o_s, x_s, mod_s = small
    n, n_s = x2d.shape[0], x_s.shape[0]
    row = lambda w: pl.BlockSpec((tm, w), lambda i: (i, 0))
    const = lambda arr: pl.BlockSpec(arr.shape, lambda i: (0,) * arr.ndim)
    piece = lambda k: _mod_spec(mod.shape[1], blocks_per_batch, k)
    piece_s = lambda k: pl.BlockSpec((None, n_s, D_MODEL), lambda i: (0, 0, k))
    return pl.pallas_call(
        functools.partial(_mlp_kernel, ff_chunk=1024),
        grid=(n // tm,),
        in_specs=[row(W_SB), row(WV_GLA), row(D_MODEL), piece(2), piece(3), piece(4), piece(5),
                  const(lw["norm_mlp"]), _layer_spec(lw["w_out"], lw["layer"]),
                  _layer_spec(lw["w_up"], lw["layer"]), _layer_spec(lw["w_down"], lw["layer"]),
                  const(a_s), const(o_s), const(x_s), piece_s(2), piece_s(3), piece_s(4), piece_s(5)],
        out_specs=[row(D_MODEL), pl.BlockSpec((n_s, D_MODEL), lambda i: (0, 0))],
        out_shape=[jax.ShapeDtypeStruct((n, D_MODEL), f32), jax.ShapeDtypeStruct((n_s, D_MODEL), f32)],
        compiler_params=_params(("arbitrary",)),
        name="out_mlp",
    )(a, o, x2d, mod, mod, mod, mod, lw["norm_mlp"], lw["w_out"], lw["w_up"], lw["w_down"],
      a_s, o_s, x_s, mod_s, mod_s, mod_s, mod_s)


def _stacked_weights(w_in, w_out, w_up, w_down):
    c_g = 3 * W_SB
    c_gate = c_g + 2 * WK_GLA + WV_GLA
    w_main = jnp.concatenate(
        [w_in[:, :, c_g:c_gate], w_in[:, :, c_gate + GATE_RANK:], w_in[:, :, c_gate:c_gate + GATE_RANK],
         jnp.zeros((DEPTH, D_MODEL, GATE_PAD - GATE_RANK), f32)], axis=2).astype(bf16)
    w_qkvt = w_in[:, :, :c_g].transpose(0, 2, 1).astype(bf16)
    return {"w_main": w_main, "w_qkvt": w_qkvt, "w_out": w_out.astype(bf16), "w_up": w_up.astype(bf16),
            "w_down": w_down.astype(bf16)}


def _layer_weights(l, stacked, norm_mix, norm_mlp, q_norm, k_norm, w_gate, b_gate, gla_norm):
    wg = jnp.concatenate([w_gate[l], jnp.zeros((GATE_PAD - GATE_RANK, WK_GLA), f32)], axis=0).astype(bf16)
    return dict(
        stacked, layer=l, w_gate=wg,
        b_gate=b_gate[l].reshape(1, WK_GLA),
        q_norm=jnp.tile(q_norm[l], H_SB),
        k_norm=jnp.tile(k_norm[l], H_SB),
        norm_mix=norm_mix[l].reshape(1, D_MODEL),
        norm_mlp=norm_mlp[l].reshape(1, D_MODEL),
        gla_norm=gla_norm[l].reshape(1, DV_GLA),
    )


def _group_layer(x2d, mod, lw, cw, n_batch, t, past_kt, past_vt, layer, s0, tm, blocks_per_batch, chunk,
                 gla_bb, prev_kv=()):
    q, kt, vt, ktb, vtb, qg, kg, vg, g, og = _in_call(x2d, mod, lw, tm, blocks_per_batch, prev_kv)
    if past_kt is None:
        a = _sb_call(q, ktb, vtb, ktb, vtb, cw, n_batch=n_batch, t_q=t, prompt=True, layer=layer)
    else:
        a = _sb_call(q, ktb, vtb, past_kt, past_vt, cw, n_batch=n_batch, t_q=t, prompt=False, layer=layer)
    o, s_new = _gla_call(qg, kg, vg, g, og, s0, lw["gla_norm"], n_batch, t, chunk, gla_bb)
    return (a, o, x2d, mod), kt, vt, s_new


def _heads_last(kt, n_batch, t):
    layers, nb = kt.shape[:2]
    return kt.reshape(layers, nb, H_SB, D_SB, -1).transpose(0, 1, 4, 2, 3).reshape(layers, n_batch, t, H_SB, D_SB)


def kernel(x_prompt, x_sample, c_prompt, c_sample, cache_k, cache_v, state_gla, w_ada, b_ada, norm_mix,
           norm_mlp, w_in, q_norm, k_norm, w_gate, b_gate, gla_norm, w_out, w_up, w_down):
    bp, tp, _ = x_prompt.shape
    bs, ts, _ = x_sample.shape
    past = cache_k.shape[2]
    n_c = bp + bs
    c_rows = -(-n_c // 16) * 16
    c_all = jnp.concatenate([c_prompt, c_sample, jnp.zeros((c_rows - n_c, D_MODEL), f32)], axis=0)
    mods = _ada_call(c_all, w_ada, b_ada)

    kk = np.arange(KB)
    half = kk[:, None] >= kk[None, :]
    cw = jnp.asarray(np.concatenate([half, half], axis=0).astype(np.float32), dtype=bf16)

    cache_kt = cache_k.transpose(0, 1, 3, 4, 2).reshape(DEPTH, bs, W_SB, past)
    cache_vt = cache_v.transpose(0, 1, 3, 4, 2).reshape(DEPTH, bs, W_SB, past)

    tm_p, tm_s = 512, bs * ts
    xp = x_prompt.reshape(bp * tp, D_MODEL)
    xs = x_sample.reshape(bs * ts, D_MODEL)
    sp_l, ks_l, vs_l, ss_l = [], [], [], []
    stacked = _stacked_weights(w_in, w_out, w_up, w_down)
    prompt_kv = ()
    for l in range(DEPTH):
        lw = _layer_weights(l, stacked, norm_mix, norm_mlp, q_norm, k_norm, w_gate, b_gate, gla_norm)
        mod_p = mods[l, :bp].reshape(bp, 1, N_MOD * D_MODEL)
        mod_s = jnp.repeat(mods[l, bp:n_c], ts, axis=0).reshape(1, bs * ts, N_MOD * D_MODEL)
        s0_p = jnp.zeros((bp, H_GLA, DK_GLA, DV_GLA), f32)
        mixed_p, kp, vp, sp = _group_layer(xp, mod_p, lw, cw, bp, tp, None, None, l, s0_p, tm_p, tp // tm_p, 256,
                                           bp, prompt_kv)
        prompt_kv = (kp, vp)
        mixed_s, kn, vn, sn = _group_layer(xs, mod_s, lw, cw, bs, ts, cache_kt, cache_vt, l,
                                           state_gla[l].astype(f32), tm_s, 1, ts, 4)
        xp, xs = _mlp_call(mixed_p, mixed_s, lw, tm_p, tp // tm_p)
        sp_l.append(sp)
        ks_l.append(_heads_last(kn, bs, ts)[0])
        vs_l.append(_heads_last(vn, bs, ts)[0])
        ss_l.append(sn)
    return (xp.reshape(bp, tp, D_MODEL), xs.reshape(bs, ts, D_MODEL),
            _heads_last(kp, bp, tp), _heads_last(vp, bp, tp), jnp.stack(sp_l),
            jnp.stack(ks_l), jnp.stack(vs_l), jnp.stack(ss_l))
```

```python
import functools

import numpy as np
import jax
import jax.numpy as jnp
from jax import lax
from jax.experimental import pallas as pl
from jax.experimental.pallas import tpu as pltpu

D_MODEL = 1024
DEPTH = 2
H_SB = 8
D_SB = 64
W_SB = H_SB * D_SB
H_GLA = 4
DK_GLA = 64
DV_GLA = 128
WK_GLA = H_GLA * DK_GLA
WV_GLA = H_GLA * DV_GLA
GATE_RANK = 16
GATE_TAU = 16.0
D_FF = 4 * D_MODEL
N_MOD = 6
EPS = 1e-6

LANES = 128
KB = 128
SB_FIRST_PASS = 3
SB_LOOP_GROUP = 1
SB_QUERY_BLOCKS = 4
N_PAIR = H_SB // 2
GATE_PAD = LANES
VMEM_LIMIT = 56 * 1024 * 1024
SURVIVAL_ZERO = 104.0
MASKED = -1e30

f32 = jnp.float32
bf16 = jnp.bfloat16
NT = (((1,), (1,)), ((), ()))
TN = (((0,), (0,)), ((), ()))


def _params(sem):
    return pltpu.CompilerParams(dimension_semantics=sem, vmem_limit_bytes=VMEM_LIMIT)


def _log_sigmoid(x):
    return jnp.minimum(x, 0.0) - jnp.log1p(jnp.exp(-jnp.abs(x)))


def _ada_kernel(c_ref, w_ref, b_ref, o_ref):
    c = c_ref[...]
    s = (c * jax.nn.sigmoid(c)).astype(bf16)
    o_ref[...] = jnp.dot(s, w_ref[...].astype(bf16), preferred_element_type=f32) + b_ref[...]


def _ada_call(c_all, w_ada, b_ada):
    rows = c_all.shape[0]
    return pl.pallas_call(
        _ada_kernel,
        grid=(DEPTH, N_MOD),
        in_specs=[
            pl.BlockSpec((rows, D_MODEL), lambda l, p: (0, 0)),
            pl.BlockSpec((None, D_MODEL, D_MODEL), lambda l, p: (l, 0, p)),
            pl.BlockSpec((None, 1, D_MODEL), lambda l, p: (l, 0, p)),
        ],
        out_specs=pl.BlockSpec((None, rows, D_MODEL), lambda l, p: (l, 0, p)),
        out_shape=jax.ShapeDtypeStruct((DEPTH, rows, N_MOD * D_MODEL), f32),
        compiler_params=_params(("parallel", "parallel")),
        name="ada_mod",
    )(c_all, w_ada, b_ada.reshape(DEPTH, 1, N_MOD * D_MODEL))


def _mod_spec(rows_per_block, blocks_per_batch, piece):
    return pl.BlockSpec((None, rows_per_block, D_MODEL),
                        lambda i: (i // blocks_per_batch, 0, piece))


def _layer_spec(stack, layer):
    zeros = (0,) * (stack.ndim - 1)
    return pl.BlockSpec((None,) + stack.shape[1:], lambda *_: (layer,) + zeros, pipeline_mode=pl.Buffered(1))


def _head_rms(t):
    t3 = t.reshape(H_SB, D_SB, t.shape[1])
    ms = jnp.mean(t3 * t3, axis=1, keepdims=True)
    return (t3 * lax.rsqrt(ms + EPS)).reshape(t.shape)


def _in_kernel(*refs, n_prev):
    (x_ref, sh_ref, sc_ref, nw_ref, w_ref, wqkv_ref, qn_ref, kn_ref, wg_ref, bg_ref) = refs[:10]
    prev_refs = refs[10:10 + 2 * min(n_prev, 1)]
    (q_ref, kt_ref, vt_ref, ktb_ref, vtb_ref, qg_ref, kg_ref, vg_ref, g_ref, og_ref) = refs[len(refs) - 10:]
    for prev_ref, out_ref in zip(prev_refs, (kt_ref, vt_ref)):
        out_ref[:n_prev] = prev_ref[...]
    x = x_ref[...]
    ms = jnp.mean(x * x, axis=-1, keepdims=True)
    y = x * lax.rsqrt(ms + EPS) * nw_ref[...]
    h = (y * (1.0 + sc_ref[...]) + sh_ref[...]).astype(bf16)

    def proj(lo, hi):
        return jnp.dot(h, w_ref[:, lo:hi], preferred_element_type=f32)

    qkvt = lax.dot_general(wqkv_ref[...], h, NT, preferred_element_type=f32)
    qt = _head_rms(qkvt[:W_SB]) * qn_ref[...] * (D_SB ** -0.5)
    q_ref[...] = qt.T.astype(bf16)
    kt = _head_rms(qkvt[W_SB:2 * W_SB]) * kn_ref[...]
    kt_ref[n_prev] = kt
    ktb_ref[...] = kt.astype(bf16)
    vt = qkvt[2 * W_SB:]
    vt_ref[n_prev] = vt
    vtb_ref[...] = vt.astype(bf16)

    o = 0
    qg_ref[...] = proj(o, o + WK_GLA) * (DK_GLA ** -0.5)
    o += WK_GLA
    kg_ref[...] = proj(o, o + WK_GLA)
    o += WK_GLA
    vg_ref[...] = proj(o, o + WV_GLA).astype(bf16)
    o += WV_GLA
    og_ref[...] = proj(o, o + WV_GLA)
    o += WV_GLA
    gr = proj(o, o + GATE_PAD).astype(bf16)
    xg = jnp.dot(gr, wg_ref[...], preferred_element_type=f32) + bg_ref[...]
    g_ref[...] = _log_sigmoid(xg) * (1.0 / GATE_TAU)


def _in_call(x2d, mod, lw, tm, blocks_per_batch, prev_kv):
    n = x2d.shape[0]
    r = mod.shape[1]
    n_batch = n // (tm * blocks_per_batch)
    t = tm * blocks_per_batch
    n_prev = prev_kv[0].shape[0] if prev_kv else 0
    row = lambda w: pl.BlockSpec((tm, w), lambda i: (i, 0))
    col_map = lambda i: (i // blocks_per_batch, 0, i % blocks_per_batch)
    col = pl.BlockSpec((None, W_SB, tm), col_map)
    stack = lambda depth: pl.BlockSpec((depth, None, W_SB, tm), lambda i: (0,) + col_map(i))
    const = lambda a: pl.BlockSpec(a.shape, lambda i: (0,) * a.ndim)
    qn_t = jnp.broadcast_to(lw["q_norm"].reshape(W_SB, 1), (W_SB, tm))
    kn_t = jnp.broadcast_to(lw["k_norm"].reshape(W_SB, 1), (W_SB, tm))
    rows_out = [(WK_GLA, f32), (WK_GLA, f32), (WV_GLA, bf16), (WK_GLA, f32), (WV_GLA, f32)]
    kv_stack = jax.ShapeDtypeStruct((n_prev + 1, n_batch, W_SB, t), f32)
    kv_b = jax.ShapeDtypeStruct((n_batch, W_SB, t), bf16)
    return pl.pallas_call(
        functools.partial(_in_kernel, n_prev=n_prev),
        grid=(n // tm,),
        in_specs=[row(D_MODEL), _mod_spec(r, blocks_per_batch, 0), _mod_spec(r, blocks_per_batch, 1),
                  const(lw["norm_mix"]), _layer_spec(lw["w_main"], lw["layer"]),
                  _layer_spec(lw["w_qkvt"], lw["layer"]),
                  const(qn_t), const(kn_t), const(lw["w_gate"]), const(lw["b_gate"])]
        + [stack(n_prev)] * len(prev_kv),
        out_specs=[row(W_SB), stack(n_prev + 1), stack(n_prev + 1), col, col] + [row(w) for w, _ in rows_out],
        out_shape=[jax.ShapeDtypeStruct((n, W_SB), bf16), kv_stack, kv_stack, kv_b, kv_b]
        + [jax.ShapeDtypeStruct((n, w), dt) for w, dt in rows_out],
        compiler_params=_params(("parallel",)),
        name="in_proj",
    )(x2d, mod, mod, lw["norm_mix"], lw["w_main"], lw["w_qkvt"], qn_t, kn_t,
      lw["w_gate"], lw["b_gate"], *prev_kv)


def _pair_diag(a):
    z = jnp.zeros((D_SB, a.shape[1]), a.dtype)
    top = jnp.concatenate([a[:D_SB], z], axis=0)
    bot = jnp.concatenate([z, a[D_SB:]], axis=0)
    return jnp.concatenate([top, bot], axis=1)


def _sb_scores(q_pair, k_tiles, biases):
    tiles = []
    for kt, bias in zip(k_tiles, biases):
        z2 = jnp.dot(q_pair, _pair_diag(kt), preferred_element_type=f32)
        for hh in range(2):
            z = z2[:, hh * KB:(hh + 1) * KB] + bias
            sp = jnp.maximum(z, 0.0) + jnp.log(1.0 + jnp.exp(-jnp.abs(z)))
            hi = sp.astype(bf16)
            lo = (sp - hi.astype(f32)).astype(bf16)
            tiles.append((z, jnp.concatenate([hi, lo], axis=1)))
    return tiles


def _sb_apply(tiles, cs, v_tiles, carries, acc, tq):
    new_carries = []
    ws = [[None, None] for _ in v_tiles]
    for hh in range(2):
        c = carries[hh]
        for bi in range(len(v_tiles)):
            ti = bi * 2 + hh
            part = cs[ti * tq:(ti + 1) * tq]
            total = jnp.broadcast_to(part[:, :1], (tq, KB))
            ws[bi][hh] = jnp.exp(tiles[ti][0] - (part if c is None else c + part)).astype(bf16)
            c = total if c is None else c + total
        new_carries.append(c)
    for bi, vt in enumerate(v_tiles):
        w_pair = jnp.concatenate(ws[bi], axis=1)
        pv = lax.dot_general(w_pair, _pair_diag(vt), NT, preferred_element_type=f32)
        acc = pv if acc is None else acc + pv
    return acc, new_carries


def _sb_kernel(q_ref, kd_ref, vd_ref, kp_ref, vp_ref, cw_ref, *rest, tq, n_sub, seg, n_past, layer):
    if n_past is None:
        o_ref, acc_ref, carry_ref = rest
    else:
        kfar_ref, vfar_ref, o_ref, acc_ref, carry_ref, kbuf_ref, vbuf_ref, sem = rest
    b = pl.program_id(0)
    step = pl.program_id(1)
    row = lax.broadcasted_iota(jnp.int32, (tq, KB), 0)
    col = lax.broadcasted_iota(jnp.int32, (tq, KB), 1)
    seg_id = b % (KB // seg)
    seg_shift = seg.bit_length() - 1
    own = jnp.logical_and(lax.shift_right_logical(col, seg_shift) == seg_id,
                          jnp.bitwise_and(col, seg - 1) < row)
    bias0 = jnp.where(own, 0.0, MASKED)
    cw = cw_ref[...]
    pair_rows = [slice(p * KB, (p + 1) * KB) for p in range(N_PAIR)]

    def past_block(j, first_pass):
        if n_past is not None and first_pass:
            c0 = (j - (n_past - (SB_FIRST_PASS - 1))) * KB
            return kp_ref[:, c0:c0 + KB].astype(bf16), vp_ref[:, c0:c0 + KB].astype(bf16), 0.0
        jc = pl.multiple_of(jnp.maximum(j, 0) * KB, KB)
        if n_past is None:
            kt, vt = kp_ref[b, :, pl.ds(jc, KB)], vp_ref[b, :, pl.ds(jc, KB)]
        else:
            copies = [pltpu.make_async_copy(src.at[layer, b, :, pl.ds(jc, KB)], dst, sem.at[i])
                      for i, (src, dst) in enumerate(((kfar_ref, kbuf_ref), (vfar_ref, vbuf_ref)))]
            for cp in copies:
                cp.start()
            for cp in copies:
                cp.wait()
            kt, vt = kbuf_ref[...], vbuf_ref[...]
        return kt.astype(bf16), vt.astype(bf16), jnp.where(j < 0, MASKED, 0.0)

    def run_group(subs, first):
        order = [(s, p) for s, _ in subs for p in range(N_PAIR)]
        blocks_of = dict(subs)
        tiles = {(s, p): _sb_scores(q_ref[s * tq:(s + 1) * tq, pair_rows[p]],
                                    [kt[pair_rows[p]] for kt, _, _ in blocks_of[s]],
                                    [bias for _, _, bias in blocks_of[s]]) for s, p in order}
        cs = jnp.dot(jnp.concatenate([t[1] for key in order for t in tiles[key]], axis=0), cw,
                     preferred_element_type=f32)
        start, floor = 0, {}
        for s, p in order:
            n_rows = 2 * len(blocks_of[s]) * tq
            carries = (None, None) if first else (carry_ref[s * H_SB + 2 * p], carry_ref[s * H_SB + 2 * p + 1])
            acc, cr = _sb_apply(tiles[s, p], cs[start:start + n_rows], [vt[pair_rows[p]] for _, vt, _ in blocks_of[s]],
                                carries, None if first else acc_ref[s * N_PAIR + p], tq)
            start += n_rows
            acc_ref[s * N_PAIR + p] = acc
            carry_ref[s * H_SB + 2 * p] = cr[0]
            carry_ref[s * H_SB + 2 * p + 1] = cr[1]
            m = jnp.minimum(cr[0], cr[1])
            floor[s] = m if s not in floor else jnp.minimum(floor[s], m)
        return {s: (jnp.min(m) < SURVIVAL_ZERO).astype(jnp.int32) for s, m in floor.items()}

    n_prev = [step * n_sub + s if n_past is None else n_past for s in range(n_sub)]
    first = [(s, [(kd_ref[:, s * KB:(s + 1) * KB], vd_ref[:, s * KB:(s + 1) * KB], bias0)]
              + [past_block(n_prev[s] - 1 - d, True) for d in range(SB_FIRST_PASS - 1)]) for s in range(n_sub)]
    alive0 = run_group(first, True)

    def cond(st):
        j, alive = st
        return jnp.logical_and(j >= 0, alive > 0)

    for s in range(n_sub):
        def body(st, s=s):
            j, _ = st
            blocks = [past_block(j - d, False) for d in range(SB_LOOP_GROUP)]
            return j - SB_LOOP_GROUP, run_group([(s, blocks)], False)[s]

        lax.while_loop(cond, body, (n_prev[s] - SB_FIRST_PASS, alive0[s]))
        o_ref[s * tq:(s + 1) * tq, :] = jnp.concatenate(
            [acc_ref[s * N_PAIR + p] for p in range(N_PAIR)], axis=1).astype(bf16)


def _sb_call(q, kd, vd, kp, vp, cw, *, n_batch, t_q, prompt, layer):
    tq = min(KB, t_q)
    n_sub = SB_QUERY_BLOCKS if prompt else 1
    nq = t_q // (tq * n_sub)
    q_spec = pl.BlockSpec((n_sub * tq, W_SB), lambda b, i: (b * nq + i, 0))
    if prompt:
        d_spec = pl.BlockSpec((None, W_SB, n_sub * KB), lambda b, i: (b, 0, i))
        p_spec = pl.BlockSpec(kp.shape, lambda b, i: (0, 0, 0))
        seg, n_past = KB, None
        far_specs, far_args, far_scratch = [], (), []
    else:
        past = kp.shape[-1]
        tail = (SB_FIRST_PASS - 1) * KB
        assert past % tail == 0 and SB_LOOP_GROUP == 1
        d_spec = pl.BlockSpec((None, W_SB, KB), lambda b, i: (0, 0, (b * t_q) // KB))
        p_spec = pl.BlockSpec((None, None, W_SB, tail), lambda b, i: (layer, b, 0, past // tail - 1))
        seg, n_past = t_q, past // KB
        far_specs, far_args = [pl.BlockSpec(memory_space=pl.ANY)] * 2, (kp, vp)
        far_scratch = [pltpu.VMEM((W_SB, KB), kp.dtype), pltpu.VMEM((W_SB, KB), vp.dtype),
                       pltpu.SemaphoreType.DMA((2,))]
    return pl.pallas_call(
        functools.partial(_sb_kernel, tq=tq, n_sub=n_sub, seg=seg, n_past=n_past, layer=layer),
        grid=(n_batch, nq),
        in_specs=[q_spec, d_spec, d_spec, p_spec, p_spec, pl.BlockSpec(cw.shape, lambda b, i: (0, 0))] + far_specs,
        out_specs=q_spec,
        out_shape=jax.ShapeDtypeStruct((n_batch * t_q, W_SB), bf16),
        scratch_shapes=[pltpu.VMEM((n_sub * N_PAIR, tq, LANES), f32),
                        pltpu.VMEM((n_sub * H_SB, tq, LANES), f32)] + far_scratch,
        compiler_params=_params(("parallel", "arbitrary")),
        name="stick_breaking",
    )(q, kd, vd, kp, vp, cw, *far_args)


GLA_PAIRS = H_GLA // 2
GLA_SAFE_DECAY = 60.0


def _gla_tables(chunk):
    idx = np.arange(chunk)
    t, s = idx[:, None], idx[None, :]
    tri = jnp.asarray((s <= t).astype(np.float32), dtype=bf16)
    level = np.where(t > s, np.floor(np.log2(np.maximum(t ^ s, 1))).astype(np.int32), -1)
    level = np.where(t < s, -2, level).astype(np.int32)
    return tri, jnp.asarray(np.concatenate([level, level], axis=1))


def _level_reference(b, h):
    L = b.shape[0]
    if h >= 4:
        return jnp.concatenate(
            [jnp.broadcast_to(b[c0 + h - 1:c0 + h, :], (2 * h, LANES)) for c0 in range(0, L, 2 * h)], axis=0)
    b3 = b.reshape(L // 8, 8, LANES)
    pos = jnp.bitwise_and(lax.broadcasted_iota(jnp.int32, b3.shape, 1), 2 * h - 1)
    up = lambda n: pltpu.roll(b3, n, 1)
    if h == 1:
        ref = jnp.where(pos == 0, b3, up(1))
    else:
        ref = jnp.where(pos == 0, up(7), jnp.where(pos == 1, b3, jnp.where(pos == 2, up(1), up(2))))
    return ref.reshape(L, LANES)


def _two_heads(a, first_head):
    return jnp.concatenate([jnp.where(first_head, a, 0.0).astype(bf16),
                            jnp.where(first_head, 0.0, a).astype(bf16)], axis=0)


def _gla_row_split(L):
    return L // 2 if L >= 2 * LANES else L


def _gla_att_single(att_ref, q_in, k, b, lev2, first_head):
    L = q_in.shape[0]
    H = _gla_row_split(L)
    k2 = _two_heads(k * jnp.exp(-b), first_head)
    causal = lev2 >= -1
    if H == L:
        a = lax.dot_general(q_in, k2, NT, preferred_element_type=f32)
        att_ref[...] = jnp.where(causal, a, 0.0)
        return
    k_top = jnp.concatenate([k2[:H], k2[L:L + H]], axis=0)
    a = lax.dot_general(q_in[:H], k_top, NT, preferred_element_type=f32)
    tri = causal[:H, :H]
    att_ref[:H, :H] = jnp.where(tri, a[:, :H], 0.0)
    att_ref[:H, L:L + H] = jnp.where(tri, a[:, H:], 0.0)
    a = lax.dot_general(q_in[H:], k2, NT, preferred_element_type=f32)
    att_ref[H:] = jnp.where(causal[H:], a, 0.0)


def _gla_att_levels(q, k, b, lev2, first_head, n_lev):
    a = lax.dot_general(q.astype(bf16), _two_heads(k, first_head), NT, preferred_element_type=f32)
    att = jnp.where(lev2 == -1, a, 0.0)
    for lev in range(n_lev):
        d = b - _level_reference(b, 1 << lev)
        e = jnp.exp(-jnp.abs(d))
        is_query = d < 0.0
        q_l = (q * jnp.where(is_query, e, 1.0)).astype(bf16)
        k_l = k * jnp.where(is_query, 1.0, e)
        a = lax.dot_general(q_l, _two_heads(k_l, first_head), NT, preferred_element_type=f32)
        att = jnp.where(lev2 == lev, a, att)
    return att


def _gla_kernel(q_ref, k_ref, v_ref, g_ref, og_ref, s0_ref, tri_ref, lev_ref, gn_ref,
                o_ref, sout_ref, s_ref, att_ref, *, chunk, bb):
    c = pl.program_id(1)
    L = chunk
    n_lev = L.bit_length() - 1
    chains = [(bi, p) for bi in range(bb) for p in range(GLA_PAIRS)]
    zeros_s = jnp.zeros((DK_GLA, DV_GLA), f32)

    @pl.when(c == 0)
    def _():
        for bi, p in chains:
            s0 = s0_ref[bi, p]
            s_ref[bi, p] = jnp.concatenate(
                [jnp.concatenate([s0[:DK_GLA], zeros_s], axis=1),
                 jnp.concatenate([zeros_s, s0[DK_GLA:]], axis=1)], axis=0)

    lane = lax.broadcasted_iota(jnp.int32, (L, LANES), 1)
    first_head = lane < DK_GLA
    lev2 = lev_ref[...]
    tri = tri_ref[...]

    pre = []
    floor = None
    for bi, p in chains:
        lanes = slice(p * LANES, (p + 1) * LANES)
        q = q_ref[bi, :, lanes]
        k = k_ref[bi, :, lanes]
        g = g_ref[bi, :, lanes]
        g_hi = g.astype(bf16)
        g_lo = (g - g_hi.astype(f32)).astype(bf16)
        g2 = jnp.concatenate([g_hi, g_lo], axis=1)
        bs = jnp.dot(tri, g2, preferred_element_type=f32)
        b = bs[:, :LANES] + bs[:, LANES:]
        b_last = b[L - 1:L, :]
        q_in = (q * jnp.exp(b)).astype(bf16)
        k_end = (k * jnp.exp(b_last - b)).astype(bf16)
        pre.append((q, k, g2, b, q_in, k_end))
        floor = b_last if floor is None else jnp.minimum(floor, b_last)
    single = jnp.min(floor) >= -GLA_SAFE_DECAY

    @pl.when(single)
    def _():
        for ci, (q, k, g2, b, q_in, k_end) in enumerate(pre):
            _gla_att_single(att_ref.at[ci], q_in, k, b, lev2, first_head)

    @pl.when(jnp.logical_not(single))
    def _():
        for ci, (q, k, g2, b, q_in, k_end) in enumerate(pre):
            att_ref[ci] = _gla_att_levels(q, k, b, lev2, first_head, n_lev)

    rowi = lax.broadcasted_iota(jnp.int32, (2 * DK_GLA, 2 * DV_GLA), 0)
    coli = lax.broadcasted_iota(jnp.int32, (2 * DK_GLA, 2 * DV_GLA), 1)
    own_block = (rowi < DK_GLA) == (coli < DV_GLA)
    ones = jnp.ones((L, LANES), bf16)
    zeros_v = jnp.zeros((L, DV_GLA), bf16)
    gn = gn_ref[...]
    for ci, (bi, p) in enumerate(chains):
        q, k, g2, b, q_in, k_end = pre[ci]
        cols = slice(p * 2 * DV_GLA, (p + 1) * 2 * DV_GLA)
        v = v_ref[bi, :, cols].astype(bf16)
        v_diag = jnp.concatenate([jnp.concatenate([v[:, :DV_GLA], zeros_v], axis=1),
                                  jnp.concatenate([zeros_v, v[:, DV_GLA:]], axis=1)], axis=0)
        s_old = s_ref[bi, p]
        H = _gla_row_split(L)
        if H == L:
            o2 = jnp.dot(att_ref[ci].astype(bf16), v_diag, preferred_element_type=f32)
        else:
            att_top = jnp.concatenate([att_ref[ci, :H, :H], att_ref[ci, :H, L:L + H]], axis=1).astype(bf16)
            v_top = jnp.concatenate([v_diag[:H], v_diag[L:L + H]], axis=0)
            o2 = jnp.concatenate([jnp.dot(att_top, v_top, preferred_element_type=f32),
                                  jnp.dot(att_ref[ci, H:].astype(bf16), v_diag, preferred_element_type=f32)], axis=0)
        o2 = o2 + jnp.dot(q_in, s_old.astype(bf16), preferred_element_type=f32)
        tot = lax.dot_general(g2, ones, TN, preferred_element_type=f32)
        decay = jnp.exp(tot[:LANES] + tot[LANES:])
        upd = lax.dot_general(k_end, v, TN, preferred_element_type=f32)
        s_ref[bi, p] = jnp.concatenate([decay, decay], axis=1) * s_old + jnp.where(own_block, upd, 0.0)
        og = og_ref[bi, :, cols]
        outs = []
        for hh in range(2):
            o = o2[:, hh * DV_GLA:(hh + 1) * DV_GLA]
            ms = jnp.mean(o * o, axis=-1, keepdims=True)
            gate = og[:, hh * DV_GLA:(hh + 1) * DV_GLA]
            outs.append(o * lax.rsqrt(ms + EPS) * gn * (gate * jax.nn.sigmoid(gate)))
        o_ref[bi, :, cols] = jnp.concatenate(outs, axis=1).astype(bf16)

    @pl.when(c == pl.num_programs(1) - 1)
    def _():
        for bi, p in chains:
            s = s_ref[bi, p]
            sout_ref[bi, p] = jnp.concatenate([s[:DK_GLA, :DV_GLA], s[DK_GLA:, DV_GLA:]], axis=0)


def _gla_call(qg, kg, vg, g, og, s0, gla_norm, n_batch, t, chunk, bb):
    nc = t // chunk
    tri, lev2 = _gla_tables(chunk)
    kspec = pl.BlockSpec((bb, chunk, WK_GLA), lambda i, c: (i, c, 0))
    vspec = pl.BlockSpec((bb, chunk, WV_GLA), lambda i, c: (i, c, 0))
    sspec = pl.BlockSpec((bb, GLA_PAIRS, 2 * DK_GLA, DV_GLA), lambda i, c: (i, 0, 0, 0))
    const = lambda a: pl.BlockSpec(a.shape, lambda i, c: (0,) * a.ndim)
    s0 = s0.reshape(n_batch, GLA_PAIRS, 2 * DK_GLA, DV_GLA)
    k3 = lambda a: a.reshape(n_batch, t, WK_GLA)
    v3 = lambda a: a.reshape(n_batch, t, WV_GLA)
    o, s_out = pl.pallas_call(
        functools.partial(_gla_kernel, chunk=chunk, bb=bb),
        grid=(n_batch // bb, nc),
        in_specs=[kspec, kspec, vspec, kspec, vspec, sspec, const(tri), const(lev2), const(gla_norm)],
        out_specs=[vspec, sspec],
        out_shape=[jax.ShapeDtypeStruct((n_batch, t, WV_GLA), bf16),
                   jax.ShapeDtypeStruct(s0.shape, f32)],
        scratch_shapes=[pltpu.VMEM((bb, GLA_PAIRS, 2 * DK_GLA, 2 * DV_GLA), f32),
                        pltpu.VMEM((bb * GLA_PAIRS, chunk, 2 * chunk), f32)],
        compiler_params=_params(("parallel", "arbitrary")),
        name="gla",
    )(k3(qg), k3(kg), v3(vg), k3(g), v3(og), s0, tri, lev2, gla_norm)
    return o.reshape(n_batch * t, WV_GLA), s_out.reshape(n_batch, H_GLA, DK_GLA, DV_GLA)


def _mlp_rows(rows, weights, y_ref, ff_chunk):
    a_ref, o_ref, x_ref, mix_gate_ref, sh_ref, sc_ref, gate_ref = rows
    nw_ref, wo_ref, wu_ref, wd_ref = weights
    m = (jnp.dot(a_ref[...], wo_ref[:W_SB, :], preferred_element_type=f32)
         + jnp.dot(o_ref[...], wo_ref[W_SB:, :], preferred_element_type=f32))
    x = x_ref[...] + mix_gate_ref[...] * m
    ms = jnp.mean(x * x, axis=-1, keepdims=True)
    y = x * lax.rsqrt(ms + EPS) * nw_ref[...]
    h = (y * (1.0 + sc_ref[...]) + sh_ref[...]).astype(bf16)
    acc = jnp.zeros(x.shape, f32)
    for c0 in range(0, D_FF, ff_chunk):
        u = jnp.dot(h, wu_ref[:, c0:c0 + ff_chunk], preferred_element_type=f32)
        u = jnp.square(jnp.maximum(u, 0.0)).astype(bf16)
        acc = acc + jnp.dot(u, wd_ref[c0:c0 + ff_chunk, :], preferred_element_type=f32)
    y_ref[...] = x + gate_ref[...] * acc


def _mlp_kernel(*refs, ff_chunk):
    big, weights, small = refs[:7], refs[7:11], refs[11:18]
    y_ref, y_small_ref = refs[18:]
    _mlp_rows(big, weights, y_ref, ff_chunk)

    @pl.when(pl.program_id(0) == pl.num_programs(0) - 1)
    def _():
        _mlp_rows(small, weights, y_small_ref, ff_chunk)


def _mlp_call(big, small, lw, tm, blocks_per_batch):
    a, o, x2d, mod = big
    a_s, o_s, x_s, mod_s = small
    n, n_s = x2d.shape[0], x_s.shape[0]
    row = lambda w: pl.BlockSpec((tm, w), lambda i: (i, 0))
    const = lambda arr: pl.BlockSpec(arr.shape, lambda i: (0,) * arr.ndim)
    piece = lambda k: _mod_spec(mod.shape[1], blocks_per_batch, k)
    piece_s = lambda k: pl.BlockSpec((None, n_s, D_MODEL), lambda i: (0, 0, k))
    return pl.pallas_call(
        functools.partial(_mlp_kernel, ff_chunk=1024),
        grid=(n // tm,),
        in_specs=[row(W_SB), row(WV_GLA), row(D_MODEL), piece(2), piece(3), piece(4), piece(5),
                  const(lw["norm_mlp"]), _layer_spec(lw["w_out"], lw["layer"]),
                  _layer_spec(lw["w_up"], lw["layer"]), _layer_spec(lw["w_down"], lw["layer"]),
                  const(a_s), const(o_s), const(x_s), piece_s(2), piece_s(3), piece_s(4), piece_s(5)],
        out_specs=[row(D_MODEL), pl.BlockSpec((n_s, D_MODEL), lambda i: (0, 0))],
        out_shape=[jax.ShapeDtypeStruct((n, D_MODEL), f32), jax.ShapeDtypeStruct((n_s, D_MODEL), f32)],
        compiler_params=_params(("arbitrary",)),
        name="out_mlp",
    )(a, o, x2d, mod, mod, mod, mod, lw["norm_mlp"], lw["w_out"], lw["w_up"], lw["w_down"],
      a_s, o_s, x_s, mod_s, mod_s, mod_s, mod_s)


def _stacked_weights(w_in, w_out, w_up, w_down):
    c_g = 3 * W_SB
    c_gate = c_g + 2 * WK_GLA + WV_GLA
    w_main = jnp.concatenate(
        [w_in[:, :, c_g:c_gate], w_in[:, :, c_gate + GATE_RANK:], w_in[:, :, c_gate:c_gate + GATE_RANK],
         jnp.zeros((DEPTH, D_MODEL, GATE_PAD - GATE_RANK), f32)], axis=2).astype(bf16)
    w_qkvt = w_in[:, :, :c_g].transpose(0, 2, 1).astype(bf16)
    return {"w_main": w_main, "w_qkvt": w_qkvt, "w_out": w_out.astype(bf16), "w_up": w_up.astype(bf16),
            "w_down": w_down.astype(bf16)}


def _layer_weights(l, stacked, norm_mix, norm_mlp, q_norm, k_norm, w_gate, b_gate, gla_norm):
    wg = jnp.concatenate([w_gate[l], jnp.zeros((GATE_PAD - GATE_RANK, WK_GLA), f32)], axis=0).astype(bf16)
    return dict(
        stacked, layer=l, w_gate=wg,
        b_gate=b_gate[l].reshape(1, WK_GLA),
        q_norm=jnp.tile(q_norm[l], H_SB),
        k_norm=jnp.tile(k_norm[l], H_SB),
        norm_mix=norm_mix[l].reshape(1, D_MODEL),
        norm_mlp=norm_mlp[l].reshape(1, D_MODEL),
        gla_norm=gla_norm[l].reshape(1, DV_GLA),
    )


def _group_layer(x2d, mod, lw, cw, n_batch, t, past_kt, past_vt, layer, s0, tm, blocks_per_batch, chunk,
                 gla_bb, prev_kv=()):
    q, kt, vt, ktb, vtb, qg, kg, vg, g, og = _in_call(x2d, mod, lw, tm, blocks_per_batch, prev_kv)
    if past_kt is None:
        a = _sb_call(q, ktb, vtb, ktb, vtb, cw, n_batch=n_batch, t_q=t, prompt=True, layer=layer)
    else:
        a = _sb_call(q, ktb, vtb, past_kt, past_vt, cw, n_batch=n_batch, t_q=t, prompt=False, layer=layer)
    o, s_new = _gla_call(qg, kg, vg, g, og, s0, lw["gla_norm"], n_batch, t, chunk, gla_bb)
    return (a, o, x2d, mod), kt, vt, s_new


def _heads_last(kt, n_batch, t):
    layers, nb = kt.shape[:2]
    return kt.reshape(layers, nb, H_SB, D_SB, -1).transpose(0, 1, 4, 2, 3).reshape(layers, n_batch, t, H_SB, D_SB)


def kernel(x_prompt, x_sample, c_prompt, c_sample, cache_k, cache_v, state_gla, w_ada, b_ada, norm_mix,
           norm_mlp, w_in, q_norm, k_norm, w_gate, b_gate, gla_norm, w_out, w_up, w_down):
    bp, tp, _ = x_prompt.shape
    bs, ts, _ = x_sample.shape
    past = cache_k.shape[2]
    n_c = bp + bs
    c_rows = -(-n_c // 16) * 16
    c_all = jnp.concatenate([c_prompt, c_sample, jnp.zeros((c_rows - n_c, D_MODEL), f32)], axis=0)
    mods = _ada_call(c_all, w_ada, b_ada)

    kk = np.arange(KB)
    half = kk[:, None] >= kk[None, :]
    cw = jnp.asarray(np.concatenate([half, half], axis=0).astype(np.float32), dtype=bf16)

    cache_kt = cache_k.transpose(0, 1, 3, 4, 2).reshape(DEPTH, bs, W_SB, past)
    cache_vt = cache_v.transpose(0, 1, 3, 4, 2).reshape(DEPTH, bs, W_SB, past)

    tm_p, tm_s = 512, bs * ts
    xp = x_prompt.reshape(bp * tp, D_MODEL)
    xs = x_sample.reshape(bs * ts, D_MODEL)
    sp_l, ks_l, vs_l, ss_l = [], [], [], []
    stacked = _stacked_weights(w_in, w_out, w_up, w_down)
    prompt_kv = ()
    for l in range(DEPTH):
        lw = _layer_weights(l, stacked, norm_mix, norm_mlp, q_norm, k_norm, w_gate, b_gate, gla_norm)
        mod_p = mods[l, :bp].reshape(bp, 1, N_MOD * D_MODEL)
        mod_s = jnp.repeat(mods[l, bp:n_c], ts, axis=0).reshape(1, bs * ts, N_MOD * D_MODEL)
        s0_p = jnp.zeros((bp, H_GLA, DK_GLA, DV_GLA), f32)
        mixed_p, kp, vp, sp = _group_layer(xp, mod_p, lw, cw, bp, tp, None, None, l, s0_p, tm_p, tp // tm_p, 256,
                                           bp, prompt_kv)
        prompt_kv = (kp, vp)
        mixed_s, kn, vn, sn = _group_layer(xs, mod_s, lw, cw, bs, ts, cache_kt, cache_vt, l,
                                           state_gla[l].astype(f32), tm_s, 1, ts, 4)
        xp, xs = _mlp_call(mixed_p, mixed_s, lw, tm_p, tp // tm_p)
        sp_l.append(sp)
        ks_l.append(_heads_last(kn, bs, ts)[0])
        vs_l.append(_heads_last(vn, bs, ts)[0])
        ss_l.append(sn)
    return (xp.reshape(bp, tp, D_MODEL), xs.reshape(bs, ts, D_MODEL),
            _heads_last(kp, bp, tp), _heads_last(vp, bp, tp), jnp.stack(sp_l),
            jnp.stack(ks_l), jnp.stack(vs_l), jnp.stack(ss_l))
```

```python
import functools

import numpy as np
import jax
import jax.numpy as jnp
from jax import lax
from jax.experimental import pallas as pl
from jax.experimental.pallas import tpu as pltpu

D_MODEL = 1024
DEPTH = 2
H_SB = 8
D_SB = 64
W_SB = H_SB * D_SB
H_GLA = 4
DK_GLA = 64
DV_GLA = 128
WK_GLA = H_GLA * DK_GLA
WV_GLA = H_GLA * DV_GLA
GATE_RANK = 16
GATE_TAU = 16.0
D_FF = 4 * D_MODEL
N_MOD = 6
EPS = 1e-6

LANES = 128
KB = 128
SB_FIRST_PASS = 3
SB_LOOP_GROUP = 1
SB_QUERY_BLOCKS = 4
SB_SAMPLE_SEQS = 4
N_PAIR = H_SB // 2
GATE_PAD = LANES
VMEM_LIMIT = 56 * 1024 * 1024
SURVIVAL_ZERO = 104.0
MASKED = -1e30

f32 = jnp.float32
bf16 = jnp.bfloat16
NT = (((1,), (1,)), ((), ()))
TN = (((0,), (0,)), ((), ()))


def _params(sem):
    return pltpu.CompilerParams(dimension_semantics=sem, vmem_limit_bytes=VMEM_LIMIT)


def _log_sigmoid(x):
    return jnp.minimum(x, 0.0) - jnp.log1p(jnp.exp(-jnp.abs(x)))


def _ada_kernel(c_ref, w_ref, b_ref, o_ref):
    c = c_ref[...]
    s = (c * jax.nn.sigmoid(c)).astype(bf16)
    o_ref[...] = jnp.dot(s, w_ref[...].astype(bf16), preferred_element_type=f32) + b_ref[...]


def _ada_call(c_all, w_ada, b_ada):
    rows = c_all.shape[0]
    return pl.pallas_call(
        _ada_kernel,
        grid=(DEPTH, N_MOD),
        in_specs=[
            pl.BlockSpec((rows, D_MODEL), lambda l, p: (0, 0)),
            pl.BlockSpec((None, D_MODEL, D_MODEL), lambda l, p: (l, 0, p)),
            pl.BlockSpec((None, 1, D_MODEL), lambda l, p: (l, 0, p)),
        ],
        out_specs=pl.BlockSpec((None, rows, D_MODEL), lambda l, p: (l, 0, p)),
        out_shape=jax.ShapeDtypeStruct((DEPTH, rows, N_MOD * D_MODEL), f32),
        compiler_params=_params(("parallel", "parallel")),
        name="ada_mod",
    )(c_all, w_ada, b_ada.reshape(DEPTH, 1, N_MOD * D_MODEL))


def _mod_spec(rows_per_block, blocks_per_batch, piece):
    return pl.BlockSpec((None, rows_per_block, D_MODEL),
                        lambda i: (i // blocks_per_batch, 0, piece))


def _layer_spec(stack, layer):
    zeros = (0,) * (stack.ndim - 1)
    return pl.BlockSpec((None,) + stack.shape[1:], lambda *_: (layer,) + zeros, pipeline_mode=pl.Buffered(1))


def _head_rms(t):
    t3 = t.reshape(H_SB, D_SB, t.shape[1])
    ms = jnp.mean(t3 * t3, axis=1, keepdims=True)
    return (t3 * lax.rsqrt(ms + EPS)).reshape(t.shape)


def _in_kernel(*refs, n_prev):
    (x_ref, sh_ref, sc_ref, nw_ref, w_ref, wqkv_ref, qn_ref, kn_ref, wg_ref, bg_ref) = refs[:10]
    prev_refs = refs[10:10 + 2 * min(n_prev, 1)]
    (q_ref, kt_ref, vt_ref, ktb_ref, vtb_ref, qg_ref, kg_ref, vg_ref, g_ref, og_ref) = refs[len(refs) - 10:]
    for prev_ref, out_ref in zip(prev_refs, (kt_ref, vt_ref)):
        out_ref[:n_prev] = prev_ref[...]
    x = x_ref[...]
    ms = jnp.mean(x * x, axis=-1, keepdims=True)
    y = x * lax.rsqrt(ms + EPS) * nw_ref[...]
    h = (y * (1.0 + sc_ref[...]) + sh_ref[...]).astype(bf16)

    def proj(lo, hi):
        return jnp.dot(h, w_ref[:, lo:hi], preferred_element_type=f32)

    qkvt = lax.dot_general(wqkv_ref[...], h, NT, preferred_element_type=f32)
    qt = _head_rms(qkvt[:W_SB]) * qn_ref[...] * (D_SB ** -0.5)
    q_ref[...] = qt.T.astype(bf16)
    kt = _head_rms(qkvt[W_SB:2 * W_SB]) * kn_ref[...]
    kt_ref[n_prev] = kt
    ktb_ref[...] = kt.astype(bf16)
    vt = qkvt[2 * W_SB:]
    vt_ref[n_prev] = vt
    vtb_ref[...] = vt.astype(bf16)

    o = 0
    qg_ref[...] = proj(o, o + WK_GLA) * (DK_GLA ** -0.5)
    o += WK_GLA
    kg_ref[...] = proj(o, o + WK_GLA)
    o += WK_GLA
    vg_ref[...] = proj(o, o + WV_GLA).astype(bf16)
    o += WV_GLA
    og_ref[...] = proj(o, o + WV_GLA)
    o += WV_GLA
    gr = proj(o, o + GATE_PAD).astype(bf16)
    xg = jnp.dot(gr, wg_ref[...], preferred_element_type=f32) + bg_ref[...]
    g_ref[...] = _log_sigmoid(xg) * (1.0 / GATE_TAU)


def _in_call(x2d, mod, lw, tm, blocks_per_batch, prev_kv):
    n = x2d.shape[0]
    r = mod.shape[1]
    n_batch = n // (tm * blocks_per_batch)
    t = tm * blocks_per_batch
    n_prev = prev_kv[0].shape[0] if prev_kv else 0
    row = lambda w: pl.BlockSpec((tm, w), lambda i: (i, 0))
    col_map = lambda i: (i // blocks_per_batch, 0, i % blocks_per_batch)
    col = pl.BlockSpec((None, W_SB, tm), col_map)
    stack = lambda depth: pl.BlockSpec((depth, None, W_SB, tm), lambda i: (0,) + col_map(i))
    const = lambda a: pl.BlockSpec(a.shape, lambda i: (0,) * a.ndim)
    qn_t = jnp.broadcast_to(lw["q_norm"].reshape(W_SB, 1), (W_SB, tm))
    kn_t = jnp.broadcast_to(lw["k_norm"].reshape(W_SB, 1), (W_SB, tm))
    rows_out = [(WK_GLA, f32), (WK_GLA, f32), (WV_GLA, bf16), (WK_GLA, f32), (WV_GLA, f32)]
    kv_stack = jax.ShapeDtypeStruct((n_prev + 1, n_batch, W_SB, t), f32)
    kv_b = jax.ShapeDtypeStruct((n_batch, W_SB, t), bf16)
    return pl.pallas_call(
        functools.partial(_in_kernel, n_prev=n_prev),
        grid=(n // tm,),
        in_specs=[row(D_MODEL), _mod_spec(r, blocks_per_batch, 0), _mod_spec(r, blocks_per_batch, 1),
                  const(lw["norm_mix"]), _layer_spec(lw["w_main"], lw["layer"]),
                  _layer_spec(lw["w_qkvt"], lw["layer"]),
                  const(qn_t), const(kn_t), const(lw["w_gate"]), const(lw["b_gate"])]
        + [stack(n_prev)] * len(prev_kv),
        out_specs=[row(W_SB), stack(n_prev + 1), stack(n_prev + 1), col, col] + [row(w) for w, _ in rows_out],
        out_shape=[jax.ShapeDtypeStruct((n, W_SB), bf16), kv_stack, kv_stack, kv_b, kv_b]
        + [jax.ShapeDtypeStruct((n, w), dt) for w, dt in rows_out],
        compiler_params=_params(("parallel",)),
        name="in_proj",
    )(x2d, mod, mod, lw["norm_mix"], lw["w_main"], lw["w_qkvt"], qn_t, kn_t,
      lw["w_gate"], lw["b_gate"], *prev_kv)


def _pair_diag(a):
    z = jnp.zeros((D_SB, a.shape[1]), a.dtype)
    top = jnp.concatenate([a[:D_SB], z], axis=0)
    bot = jnp.concatenate([z, a[D_SB:]], axis=0)
    return jnp.concatenate([top, bot], axis=1)


def _sb_scores(q_pair, k_tiles, biases):
    tiles = []
    for kt, bias in zip(k_tiles, biases):
        z2 = jnp.dot(q_pair, _pair_diag(kt), preferred_element_type=f32)
        for hh in range(2):
            z = z2[:, hh * KB:(hh + 1) * KB] + bias
            sp = jnp.maximum(z, 0.0) + jnp.log(1.0 + jnp.exp(-jnp.abs(z)))
            hi = sp.astype(bf16)
            lo = (sp - hi.astype(f32)).astype(bf16)
            tiles.append((z, jnp.concatenate([hi, lo], axis=1)))
    return tiles


def _sb_apply(tiles, cs, v_tiles, carries, acc, tq):
    new_carries = []
    ws = [[None, None] for _ in v_tiles]
    for hh in range(2):
        c = carries[hh]
        for bi in range(len(v_tiles)):
            ti = bi * 2 + hh
            part = cs[ti * tq:(ti + 1) * tq]
            total = jnp.broadcast_to(part[:, :1], (tq, KB))
            ws[bi][hh] = jnp.exp(tiles[ti][0] - (part if c is None else c + part)).astype(bf16)
            c = total if c is None else c + total
        new_carries.append(c)
    for bi, vt in enumerate(v_tiles):
        w_pair = jnp.concatenate(ws[bi], axis=1)
        pv = lax.dot_general(w_pair, _pair_diag(vt), NT, preferred_element_type=f32)
        acc = pv if acc is None else acc + pv
    return acc, new_carries


def _sb_kernel(q_ref, kd_ref, vd_ref, kp_ref, vp_ref, cw_ref, *rest, tq, n_sub, seg, n_past, layer):
    if n_past is None:
        o_ref, acc_ref, carry_ref = rest
    else:
        kfar_ref, vfar_ref, o_ref, acc_ref, carry_ref, kbuf_ref, vbuf_ref, sem = rest
    b = pl.program_id(0)
    step = pl.program_id(1)
    row = lax.broadcasted_iota(jnp.int32, (tq, KB), 0)
    col = lax.broadcasted_iota(jnp.int32, (tq, KB), 1)
    cw = cw_ref[...]
    pair_rows = [slice(p * KB, (p + 1) * KB) for p in range(N_PAIR)]
    seq = (lambda s: b) if n_past is None else (lambda s: b * n_sub + s)
    seg_shift = seg.bit_length() - 1

    def first_block(s):
        own = jnp.logical_and(lax.shift_right_logical(col, seg_shift) == seq(s) % (KB // seg),
                              jnp.bitwise_and(col, seg - 1) < row)
        lanes = slice(s * KB, (s + 1) * KB) if n_past is None else slice(None)
        return kd_ref[:, lanes], vd_ref[:, lanes], jnp.where(own, 0.0, MASKED)

    def past_block(j, first_pass, s):
        if n_past is not None and first_pass:
            c0 = (j - (n_past - (SB_FIRST_PASS - 1))) * KB
            return kp_ref[s, :, c0:c0 + KB].astype(bf16), vp_ref[s, :, c0:c0 + KB].astype(bf16), 0.0
        jc = pl.multiple_of(jnp.maximum(j, 0) * KB, KB)
        if n_past is None:
            kt, vt = kp_ref[b, :, pl.ds(jc, KB)], vp_ref[b, :, pl.ds(jc, KB)]
        else:
            copies = [pltpu.make_async_copy(src.at[layer, seq(s), :, pl.ds(jc, KB)], dst, sem.at[i])
                      for i, (src, dst) in enumerate(((kfar_ref, kbuf_ref), (vfar_ref, vbuf_ref)))]
            for cp in copies:
                cp.start()
            for cp in copies:
                cp.wait()
            kt, vt = kbuf_ref[...], vbuf_ref[...]
        return kt.astype(bf16), vt.astype(bf16), jnp.where(j < 0, MASKED, 0.0)

    def run_group(subs, first):
        order = [(s, p) for s, _ in subs for p in range(N_PAIR)]
        blocks_of = dict(subs)
        tiles = {(s, p): _sb_scores(q_ref[s * tq:(s + 1) * tq, pair_rows[p]],
                                    [kt[pair_rows[p]] for kt, _, _ in blocks_of[s]],
                                    [bias for _, _, bias in blocks_of[s]]) for s, p in order}
        cs = jnp.dot(jnp.concatenate([t[1] for key in order for t in tiles[key]], axis=0), cw,
                     preferred_element_type=f32)
        start, floor = 0, {}
        for s, p in order:
            n_rows = 2 * len(blocks_of[s]) * tq
            carries = (None, None) if first else (carry_ref[s * H_SB + 2 * p], carry_ref[s * H_SB + 2 * p + 1])
            acc, cr = _sb_apply(tiles[s, p], cs[start:start + n_rows], [vt[pair_rows[p]] for _, vt, _ in blocks_of[s]],
                                carries, None if first else acc_ref[s * N_PAIR + p], tq)
            start += n_rows
            acc_ref[s * N_PAIR + p] = acc
            carry_ref[s * H_SB + 2 * p] = cr[0]
            carry_ref[s * H_SB + 2 * p + 1] = cr[1]
            m = jnp.minimum(cr[0], cr[1])
            floor[s] = m if s not in floor else jnp.minimum(floor[s], m)
        return {s: (jnp.min(m) < SURVIVAL_ZERO).astype(jnp.int32) for s, m in floor.items()}

    n_prev = [step * n_sub + s if n_past is None else n_past for s in range(n_sub)]
    first = [(s, [first_block(s)] + [past_block(n_prev[s] - 1 - d, True, s) for d in range(SB_FIRST_PASS - 1)])
             for s in range(n_sub)]
    alive0 = run_group(first, True)

    def cond(st):
        j, alive = st
        return jnp.logical_and(j >= 0, alive > 0)

    for s in range(n_sub):
        def body(st, s=s):
            j, _ = st
            blocks = [past_block(j - d, False, s) for d in range(SB_LOOP_GROUP)]
            return j - SB_LOOP_GROUP, run_group([(s, blocks)], False)[s]

        lax.while_loop(cond, body, (n_prev[s] - SB_FIRST_PASS, alive0[s]))
        o_ref[s * tq:(s + 1) * tq, :] = jnp.concatenate(
            [acc_ref[s * N_PAIR + p] for p in range(N_PAIR)], axis=1).astype(bf16)


def _sb_call(q, kd, vd, kp, vp, cw, *, n_batch, t_q, prompt, layer):
    tq = min(KB, t_q)
    n_sub = SB_QUERY_BLOCKS if prompt else SB_SAMPLE_SEQS
    nq = t_q // (tq * n_sub) if prompt else 1
    q_spec = pl.BlockSpec((n_sub * tq, W_SB), lambda b, i: (b * nq + i, 0))
    if prompt:
        d_spec = pl.BlockSpec((None, W_SB, n_sub * KB), lambda b, i: (b, 0, i))
        p_spec = pl.BlockSpec(kp.shape, lambda b, i: (0, 0, 0))
        seg, n_past = KB, None
        far_specs, far_args, far_scratch = [], (), []
    else:
        past = kp.shape[-1]
        tail = (SB_FIRST_PASS - 1) * KB
        assert past % tail == 0 and SB_LOOP_GROUP == 1
        assert t_q == tq and n_batch % n_sub == 0 and KB % (n_sub * t_q) == 0
        d_spec = pl.BlockSpec((None, W_SB, KB), lambda b, i: (0, 0, (b * n_sub * t_q) // KB))
        p_spec = pl.BlockSpec((None, n_sub, W_SB, tail), lambda b, i: (layer, b, 0, past // tail - 1))
        seg, n_past = t_q, past // KB
        far_specs, far_args = [pl.BlockSpec(memory_space=pl.ANY)] * 2, (kp, vp)
        far_scratch = [pltpu.VMEM((W_SB, KB), kp.dtype), pltpu.VMEM((W_SB, KB), vp.dtype),
                       pltpu.SemaphoreType.DMA((2,))]
    return pl.pallas_call(
        functools.partial(_sb_kernel, tq=tq, n_sub=n_sub, seg=seg, n_past=n_past, layer=layer),
        grid=(n_batch, nq) if prompt else (n_batch // n_sub, 1),
        in_specs=[q_spec, d_spec, d_spec, p_spec, p_spec, pl.BlockSpec(cw.shape, lambda b, i: (0, 0))] + far_specs,
        out_specs=q_spec,
        out_shape=jax.ShapeDtypeStruct((n_batch * t_q, W_SB), bf16),
        scratch_shapes=[pltpu.VMEM((n_sub * N_PAIR, tq, LANES), f32),
                        pltpu.VMEM((n_sub * H_SB, tq, LANES), f32)] + far_scratch,
        compiler_params=_params(("parallel", "arbitrary")),
        name="stick_breaking",
    )(q, kd, vd, kp, vp, cw, *far_args)


GLA_PAIRS = H_GLA // 2
GLA_SAFE_DECAY = 60.0


def _gla_tables(chunk):
    idx = np.arange(chunk)
    t, s = idx[:, None], idx[None, :]
    tri = jnp.asarray((s <= t).astype(np.float32), dtype=bf16)
    level = np.where(t > s, np.floor(np.log2(np.maximum(t ^ s, 1))).astype(np.int32), -1)
    level = np.where(t < s, -2, level).astype(np.int32)
    return tri, jnp.asarray(np.concatenate([level, level], axis=1))


def _level_reference(b, h):
    L = b.shape[0]
    if h >= 4:
        return jnp.concatenate(
            [jnp.broadcast_to(b[c0 + h - 1:c0 + h, :], (2 * h, LANES)) for c0 in range(0, L, 2 * h)], axis=0)
    b3 = b.reshape(L // 8, 8, LANES)
    pos = jnp.bitwise_and(lax.broadcasted_iota(jnp.int32, b3.shape, 1), 2 * h - 1)
    up = lambda n: pltpu.roll(b3, n, 1)
    if h == 1:
        ref = jnp.where(pos == 0, b3, up(1))
    else:
        ref = jnp.where(pos == 0, up(7), jnp.where(pos == 1, b3, jnp.where(pos == 2, up(1), up(2))))
    return ref.reshape(L, LANES)


def _two_heads(a, first_head):
    return jnp.concatenate([jnp.where(first_head, a, 0.0).astype(bf16),
                            jnp.where(first_head, 0.0, a).astype(bf16)], axis=0)


def _gla_row_split(L):
    return L // 2 if L >= 2 * LANES else L


def _gla_att_single(att_ref, q_in, k, b, lev2, first_head):
    L = q_in.shape[0]
    H = _gla_row_split(L)
    k2 = _two_heads(k * jnp.exp(-b), first_head)
    causal = lev2 >= -1
    if H == L:
        a = lax.dot_general(q_in, k2, NT, preferred_element_type=f32)
        att_ref[...] = jnp.where(causal, a, 0.0)
        return
    k_top = jnp.concatenate([k2[:H], k2[L:L + H]], axis=0)
    a = lax.dot_general(q_in[:H], k_top, NT, preferred_element_type=f32)
    tri = causal[:H, :H]
    att_ref[:H, :H] = jnp.where(tri, a[:, :H], 0.0)
    att_ref[:H, L:L + H] = jnp.where(tri, a[:, H:], 0.0)
    a = lax.dot_general(q_in[H:], k2, NT, preferred_element_type=f32)
    att_ref[H:] = jnp.where(causal[H:], a, 0.0)


def _gla_att_levels(q, k, b, lev2, first_head, n_lev):
    a = lax.dot_general(q.astype(bf16), _two_heads(k, first_head), NT, preferred_element_type=f32)
    att = jnp.where(lev2 == -1, a, 0.0)
    for lev in range(n_lev):
        d = b - _level_reference(b, 1 << lev)
        e = jnp.exp(-jnp.abs(d))
        is_query = d < 0.0
        q_l = (q * jnp.where(is_query, e, 1.0)).astype(bf16)
        k_l = k * jnp.where(is_query, 1.0, e)
        a = lax.dot_general(q_l, _two_heads(k_l, first_head), NT, preferred_element_type=f32)
        att = jnp.where(lev2 == lev, a, att)
    return att


def _gla_kernel(q_ref, k_ref, v_ref, g_ref, og_ref, s0_ref, tri_ref, lev_ref, gn_ref,
                o_ref, sout_ref, s_ref, att_ref, *, chunk, bb):
    c = pl.program_id(1)
    L = chunk
    n_lev = L.bit_length() - 1
    chains = [(bi, p) for bi in range(bb) for p in range(GLA_PAIRS)]
    zeros_s = jnp.zeros((DK_GLA, DV_GLA), f32)

    @pl.when(c == 0)
    def _():
        for bi, p in chains:
            s0 = s0_ref[bi, p]
            s_ref[bi, p] = jnp.concatenate(
                [jnp.concatenate([s0[:DK_GLA], zeros_s], axis=1),
                 jnp.concatenate([zeros_s, s0[DK_GLA:]], axis=1)], axis=0)

    lane = lax.broadcasted_iota(jnp.int32, (L, LANES), 1)
    first_head = lane < DK_GLA
    lev2 = lev_ref[...]
    tri = tri_ref[...]

    pre = []
    floor = None
    for bi, p in chains:
        lanes = slice(p * LANES, (p + 1) * LANES)
        q = q_ref[bi, :, lanes]
        k = k_ref[bi, :, lanes]
        g = g_ref[bi, :, lanes]
        g_hi = g.astype(bf16)
        g_lo = (g - g_hi.astype(f32)).astype(bf16)
        g2 = jnp.concatenate([g_hi, g_lo], axis=1)
        bs = jnp.dot(tri, g2, preferred_element_type=f32)
        b = bs[:, :LANES] + bs[:, LANES:]
        b_last = b[L - 1:L, :]
        q_in = (q * jnp.exp(b)).astype(bf16)
        k_end = (k * jnp.exp(b_last - b)).astype(bf16)
        pre.append((q, k, g2, b, q_in, k_end))
        floor = b_last if floor is None else jnp.minimum(floor, b_last)
    single = jnp.min(floor) >= -GLA_SAFE_DECAY

    @pl.when(single)
    def _():
        for ci, (q, k, g2, b, q_in, k_end) in enumerate(pre):
            _gla_att_single(att_ref.at[ci], q_in, k, b, lev2, first_head)

    @pl.when(jnp.logical_not(single))
    def _():
        for ci, (q, k, g2, b, q_in, k_end) in enumerate(pre):
            att_ref[ci] = _gla_att_levels(q, k, b, lev2, first_head, n_lev)

    rowi = lax.broadcasted_iota(jnp.int32, (2 * DK_GLA, 2 * DV_GLA), 0)
    coli = lax.broadcasted_iota(jnp.int32, (2 * DK_GLA, 2 * DV_GLA), 1)
    own_block = (rowi < DK_GLA) == (coli < DV_GLA)
    ones = jnp.ones((L, LANES), bf16)
    zeros_v = jnp.zeros((L, DV_GLA), bf16)
    gn = gn_ref[...]
    for ci, (bi, p) in enumerate(chains):
        q, k, g2, b, q_in, k_end = pre[ci]
        cols = slice(p * 2 * DV_GLA, (p + 1) * 2 * DV_GLA)
        v = v_ref[bi, :, cols].astype(bf16)
        v_diag = jnp.concatenate([jnp.concatenate([v[:, :DV_GLA], zeros_v], axis=1),
                                  jnp.concatenate([zeros_v, v[:, DV_GLA:]], axis=1)], axis=0)
        s_old = s_ref[bi, p]
        H = _gla_row_split(L)
        if H == L:
            o2 = jnp.dot(att_ref[ci].astype(bf16), v_diag, preferred_element_type=f32)
        else:
            att_top = jnp.concatenate([att_ref[ci, :H, :H], att_ref[ci, :H, L:L + H]], axis=1).astype(bf16)
            v_top = jnp.concatenate([v_diag[:H], v_diag[L:L + H]], axis=0)
            o2 = jnp.concatenate([jnp.dot(att_top, v_top, preferred_element_type=f32),
                                  jnp.dot(att_ref[ci, H:].astype(bf16), v_diag, preferred_element_type=f32)], axis=0)
        o2 = o2 + jnp.dot(q_in, s_old.astype(bf16), preferred_element_type=f32)
        tot = lax.dot_general(g2, ones, TN, preferred_element_type=f32)
        decay = jnp.exp(tot[:LANES] + tot[LANES:])
        upd = lax.dot_general(k_end, v, TN, preferred_element_type=f32)
        s_ref[bi, p] = jnp.concatenate([decay, decay], axis=1) * s_old + jnp.where(own_block, upd, 0.0)
        og = og_ref[bi, :, cols]
        outs = []
        for hh in range(2):
            o = o2[:, hh * DV_GLA:(hh + 1) * DV_GLA]
            ms = jnp.mean(o * o, axis=-1, keepdims=True)
            gate = og[:, hh * DV_GLA:(hh + 1) * DV_GLA]
            outs.append(o * lax.rsqrt(ms + EPS) * gn * (gate * jax.nn.sigmoid(gate)))
        o_ref[bi, :, cols] = jnp.concatenate(outs, axis=1).astype(bf16)

    @pl.when(c == pl.num_programs(1) - 1)
    def _():
        for bi, p in chains:
            s = s_ref[bi, p]
            sout_ref[bi, p] = jnp.concatenate([s[:DK_GLA, :DV_GLA], s[DK_GLA:, DV_GLA:]], axis=0)


def _gla_call(qg, kg, vg, g, og, s0, gla_norm, n_batch, t, chunk, bb):
    nc = t // chunk
    tri, lev2 = _gla_tables(chunk)
    kspec = pl.BlockSpec((bb, chunk, WK_GLA), lambda i, c: (i, c, 0))
    vspec = pl.BlockSpec((bb, chunk, WV_GLA), lambda i, c: (i, c, 0))
    sspec = pl.BlockSpec((bb, GLA_PAIRS, 2 * DK_GLA, DV_GLA), lambda i, c: (i, 0, 0, 0))
    const = lambda a: pl.BlockSpec(a.shape, lambda i, c: (0,) * a.ndim)
    s0 = s0.reshape(n_batch, GLA_PAIRS, 2 * DK_GLA, DV_GLA)
    k3 = lambda a: a.reshape(n_batch, t, WK_GLA)
    v3 = lambda a: a.reshape(n_batch, t, WV_GLA)
    o, s_out = pl.pallas_call(
        functools.partial(_gla_kernel, chunk=chunk, bb=bb),
        grid=(n_batch // bb, nc),
        in_specs=[kspec, kspec, vspec, kspec, vspec, sspec, const(tri), const(lev2), const(gla_norm)],
        out_specs=[vspec, sspec],
        out_shape=[jax.ShapeDtypeStruct((n_batch, t, WV_GLA), bf16),
                   jax.ShapeDtypeStruct(s0.shape, f32)],
        scratch_shapes=[pltpu.VMEM((bb, GLA_PAIRS, 2 * DK_GLA, 2 * DV_GLA), f32),
                        pltpu.VMEM((bb * GLA_PAIRS, chunk, 2 * chunk), f32)],
        compiler_params=_params(("parallel", "arbitrary")),
        name="gla",
    )(k3(qg), k3(kg), v3(vg), k3(g), v3(og), s0, tri, lev2, gla_norm)
    return o.reshape(n_batch * t, WV_GLA), s_out.reshape(n_batch, H_GLA, DK_GLA, DV_GLA)


def _mlp_rows(rows, weights, y_ref, ff_chunk):
    a_ref, o_ref, x_ref, mix_gate_ref, sh_ref, sc_ref, gate_ref = rows
    nw_ref, wo_ref, wu_ref, wd_ref = weights
    m = (jnp.dot(a_ref[...], wo_ref[:W_SB, :], preferred_element_type=f32)
         + jnp.dot(o_ref[...], wo_ref[W_SB:, :], preferred_element_type=f32))
    x = x_ref[...] + mix_gate_ref[...] * m
    ms = jnp.mean(x * x, axis=-1, keepdims=True)
    y = x * lax.rsqrt(ms + EPS) * nw_ref[...]
    h = (y * (1.0 + sc_ref[...]) + sh_ref[...]).astype(bf16)
    acc = jnp.zeros(x.shape, f32)
    for c0 in range(0, D_FF, ff_chunk):
        u = jnp.dot(h, wu_ref[:, c0:c0 + ff_chunk], preferred_element_type=f32)
        u = jnp.square(jnp.maximum(u, 0.0)).astype(bf16)
        acc = acc + jnp.dot(u, wd_ref[c0:c0 + ff_chunk, :], preferred_element_type=f32)
    y_ref[...] = x + gate_ref[...] * acc


def _mlp_kernel(*refs, ff_chunk):
    big, weights, small = refs[:7], refs[7:11], refs[11:18]
    y_ref, y_small_ref = refs[18:]
    _mlp_rows(big, weights, y_ref, ff_chunk)

    @pl.when(pl.program_id(0) == pl.num_programs(0) - 1)
    def _():
        _mlp_rows(small, weights, y_small_ref, ff_chunk)


def _mlp_call(big, small, lw, tm, blocks_per_batch):
    a, o, x2d, mod = big
    a_s, o_s, x_s, mod_s = small
    n, n_s = x2d.shape[0], x_s.shape[0]
    row = lambda w: pl.BlockSpec((tm, w), lambda i: (i, 0))
    const = lambda arr: pl.BlockSpec(arr.shape, lambda i: (0,) * arr.ndim)
    piece = lambda k: _mod_spec(mod.shape[1], blocks_per_batch, k)
    piece_s = lambda k: pl.BlockSpec((None, n_s, D_MODEL), lambda i: (0, 0, k))
    return pl.pallas_call(
        functools.partial(_mlp_kernel, ff_chunk=1024),
        grid=(n // tm,),
        in_specs=[row(W_SB), row(WV_GLA), row(D_MODEL), piece(2), piece(3), piece(4), piece(5),
                  const(lw["norm_mlp"]), _layer_spec(lw["w_out"], lw["layer"]),
                  _layer_spec(lw["w_up"], lw["layer"]), _layer_spec(lw["w_down"], lw["layer"]),
                  const(a_s), const(o_s), const(x_s), piece_s(2), piece_s(3), piece_s(4), piece_s(5)],
        out_specs=[row(D_MODEL), pl.BlockSpec((n_s, D_MODEL), lambda i: (0, 0))],
        out_shape=[jax.ShapeDtypeStruct((n, D_MODEL), f32), jax.ShapeDtypeStruct((n_s, D_MODEL), f32)],
        compiler_params=_params(("arbitrary",)),
        name="out_mlp",
    )(a, o, x2d, mod, mod, mod, mod, lw["norm_mlp"], lw["w_out"], lw["w_up"], lw["w_down"],
      a_s, o_s, x_s, mod_s, mod_s, mod_s, mod_s)


def _stacked_weights(w_in, w_out, w_up, w_down):
    c_g = 3 * W_SB
    c_gate = c_g + 2 * WK_GLA + WV_GLA
    w_main = jnp.concatenate(
        [w_in[:, :, c_g:c_gate], w_in[:, :, c_gate + GATE_RANK:], w_in[:, :, c_gate:c_gate + GATE_RANK],
         jnp.zeros((DEPTH, D_MODEL, GATE_PAD - GATE_RANK), f32)], axis=2).astype(bf16)
    w_qkvt = w_in[:, :, :c_g].transpose(0, 2, 1).astype(bf16)
    return {"w_main": w_main, "w_qkvt": w_qkvt, "w_out": w_out.astype(bf16), "w_up": w_up.astype(bf16),
            "w_down": w_down.astype(bf16)}


def _layer_weights(l, stacked, norm_mix, norm_mlp, q_norm, k_norm, w_gate, b_gate, gla_norm):
    wg = jnp.concatenate([w_gate[l], jnp.zeros((GATE_PAD - GATE_RANK, WK_GLA), f32)], axis=0).astype(bf16)
    return dict(
        stacked, layer=l, w_gate=wg,
        b_gate=b_gate[l].reshape(1, WK_GLA),
        q_norm=jnp.tile(q_norm[l], H_SB),
        k_norm=jnp.tile(k_norm[l], H_SB),
        norm_mix=norm_mix[l].reshape(1, D_MODEL),
        norm_mlp=norm_mlp[l].reshape(1, D_MODEL),
        gla_norm=gla_norm[l].reshape(1, DV_GLA),
    )


def _group_layer(x2d, mod, lw, cw, n_batch, t, past_kt, past_vt, layer, s0, tm, blocks_per_batch, chunk,
                 gla_bb, prev_kv=()):
    q, kt, vt, ktb, vtb, qg, kg, vg, g, og = _in_call(x2d, mod, lw, tm, blocks_per_batch, prev_kv)
    if past_kt is None:
        a = _sb_call(q, ktb, vtb, ktb, vtb, cw, n_batch=n_batch, t_q=t, prompt=True, layer=layer)
    else:
        a = _sb_call(q, ktb, vtb, past_kt, past_vt, cw, n_batch=n_batch, t_q=t, prompt=False, layer=layer)
    o, s_new = _gla_call(qg, kg, vg, g, og, s0, lw["gla_norm"], n_batch, t, chunk, gla_bb)
    return (a, o, x2d, mod), kt, vt, s_new


def _heads_last(kt, n_batch, t):
    layers, nb = kt.shape[:2]
    return kt.reshape(layers, nb, H_SB, D_SB, -1).transpose(0, 1, 4, 2, 3).reshape(layers, n_batch, t, H_SB, D_SB)


def kernel(x_prompt, x_sample, c_prompt, c_sample, cache_k, cache_v, state_gla, w_ada, b_ada, norm_mix,
           norm_mlp, w_in, q_norm, k_norm, w_gate, b_gate, gla_norm, w_out, w_up, w_down):
    bp, tp, _ = x_prompt.shape
    bs, ts, _ = x_sample.shape
    past = cache_k.shape[2]
    n_c = bp + bs
    c_rows = -(-n_c // 16) * 16
    c_all = jnp.concatenate([c_prompt, c_sample, jnp.zeros((c_rows - n_c, D_MODEL), f32)], axis=0)
    mods = _ada_call(c_all, w_ada, b_ada)

    kk = np.arange(KB)
    half = kk[:, None] >= kk[None, :]
    cw = jnp.asarray(np.concatenate([half, half], axis=0).astype(np.float32), dtype=bf16)

    cache_kt = cache_k.transpose(0, 1, 3, 4, 2).reshape(DEPTH, bs, W_SB, past)
    cache_vt = cache_v.transpose(0, 1, 3, 4, 2).reshape(DEPTH, bs, W_SB, past)

    tm_p, tm_s = 512, bs * ts
    xp = x_prompt.reshape(bp * tp, D_MODEL)
    xs = x_sample.reshape(bs * ts, D_MODEL)
    sp_l, ks_l, vs_l, ss_l = [], [], [], []
    stacked = _stacked_weights(w_in, w_out, w_up, w_down)
    prompt_kv = ()
    for l in range(DEPTH):
        lw = _layer_weights(l, stacked, norm_mix, norm_mlp, q_norm, k_norm, w_gate, b_gate, gla_norm)
        mod_p = mods[l, :bp].reshape(bp, 1, N_MOD * D_MODEL)
        mod_s = jnp.repeat(mods[l, bp:n_c], ts, axis=0).reshape(1, bs * ts, N_MOD * D_MODEL)
        s0_p = jnp.zeros((bp, H_GLA, DK_GLA, DV_GLA), f32)
        mixed_p, kp, vp, sp = _group_layer(xp, mod_p, lw, cw, bp, tp, None, None, l, s0_p, tm_p, tp // tm_p, 256,
                                           bp, prompt_kv)
        prompt_kv = (kp, vp)
        mixed_s, kn, vn, sn = _group_layer(xs, mod_s, lw, cw, bs, ts, cache_kt, cache_vt, l,
                                           state_gla[l].astype(f32), tm_s, 1, ts, 4)
        xp, xs = _mlp_call(mixed_p, mixed_s, lw, tm_p, tp // tm_p)
        sp_l.append(sp)
        ks_l.append(_heads_last(kn, bs, ts)[0])
        vs_l.append(_heads_last(vn, bs, ts)[0])
        ss_l.append(sn)
    return (xp.reshape(bp, tp, D_MODEL), xs.reshape(bs, ts, D_MODEL),
            _heads_last(kp, bp, tp), _heads_last(vp, bp, tp), jnp.stack(sp_l),
            jnp.stack(ks_l), jnp.stack(vs_l), jnp.stack(ss_l))
```

```python
import functools

import numpy as np
import jax
import jax.numpy as jnp
from jax import lax
from jax.experimental import pallas as pl
from jax.experimental.pallas import tpu as pltpu

D_MODEL = 1024
DEPTH = 2
H_SB = 8
D_SB = 64
W_SB = H_SB * D_SB
H_GLA = 4
DK_GLA = 64
DV_GLA = 128
WK_GLA = H_GLA * DK_GLA
WV_GLA = H_GLA * DV_GLA
GATE_RANK = 16
GATE_TAU = 16.0
D_FF = 4 * D_MODEL
N_MOD = 6
EPS = 1e-6

LANES = 128
KB = 128
SB_FIRST_PASS = 3
SB_LOOP_GROUP = 1
SB_QUERY_BLOCKS = 4
SB_SAMPLE_SEQS = 8
N_PAIR = H_SB // 2
GATE_PAD = LANES
VMEM_LIMIT = 56 * 1024 * 1024
SURVIVAL_ZERO = 104.0
MASKED = -1e30

f32 = jnp.float32
bf16 = jnp.bfloat16
NT = (((1,), (1,)), ((), ()))
TN = (((0,), (0,)), ((), ()))


def _params(sem):
    return pltpu.CompilerParams(dimension_semantics=sem, vmem_limit_bytes=VMEM_LIMIT)


def _log_sigmoid(x):
    return jnp.minimum(x, 0.0) - jnp.log1p(jnp.exp(-jnp.abs(x)))


def _ada_kernel(c_ref, w_ref, b_ref, o_ref):
    c = c_ref[...]
    s = (c * jax.nn.sigmoid(c)).astype(bf16)
    o_ref[...] = jnp.dot(s, w_ref[...].astype(bf16), preferred_element_type=f32) + b_ref[...]


def _ada_call(c_all, w_ada, b_ada):
    rows = c_all.shape[0]
    return pl.pallas_call(
        _ada_kernel,
        grid=(DEPTH, N_MOD),
        in_specs=[
            pl.BlockSpec((rows, D_MODEL), lambda l, p: (0, 0)),
            pl.BlockSpec((None, D_MODEL, D_MODEL), lambda l, p: (l, 0, p)),
            pl.BlockSpec((None, 1, D_MODEL), lambda l, p: (l, 0, p)),
        ],
        out_specs=pl.BlockSpec((None, rows, D_MODEL), lambda l, p: (l, 0, p)),
        out_shape=jax.ShapeDtypeStruct((DEPTH, rows, N_MOD * D_MODEL), f32),
        compiler_params=_params(("parallel", "parallel")),
        name="ada_mod",
    )(c_all, w_ada, b_ada.reshape(DEPTH, 1, N_MOD * D_MODEL))


def _mod_spec(rows_per_block, blocks_per_batch, piece):
    return pl.BlockSpec((None, rows_per_block, D_MODEL),
                        lambda i: (i // blocks_per_batch, 0, piece))


def _layer_spec(stack, layer):
    zeros = (0,) * (stack.ndim - 1)
    return pl.BlockSpec((None,) + stack.shape[1:], lambda *_: (layer,) + zeros, pipeline_mode=pl.Buffered(1))


def _head_rms(t):
    t3 = t.reshape(H_SB, D_SB, t.shape[1])
    ms = jnp.mean(t3 * t3, axis=1, keepdims=True)
    return (t3 * lax.rsqrt(ms + EPS)).reshape(t.shape)


def _in_kernel(*refs, n_prev):
    (x_ref, sh_ref, sc_ref, nw_ref, w_ref, wqkv_ref, qn_ref, kn_ref, wg_ref, bg_ref) = refs[:10]
    prev_refs = refs[10:10 + 2 * min(n_prev, 1)]
    (q_ref, kt_ref, vt_ref, ktb_ref, vtb_ref, qg_ref, kg_ref, vg_ref, g_ref, og_ref) = refs[len(refs) - 10:]
    for prev_ref, out_ref in zip(prev_refs, (kt_ref, vt_ref)):
        out_ref[:n_prev] = prev_ref[...]
    x = x_ref[...]
    ms = jnp.mean(x * x, axis=-1, keepdims=True)
    y = x * lax.rsqrt(ms + EPS) * nw_ref[...]
    h = (y * (1.0 + sc_ref[...]) + sh_ref[...]).astype(bf16)

    def proj(lo, hi):
        return jnp.dot(h, w_ref[:, lo:hi], preferred_element_type=f32)

    qkvt = lax.dot_general(wqkv_ref[...], h, NT, preferred_element_type=f32)
    qt = _head_rms(qkvt[:W_SB]) * qn_ref[...] * (D_SB ** -0.5)
    q_ref[...] = qt.T.astype(bf16)
    kt = _head_rms(qkvt[W_SB:2 * W_SB]) * kn_ref[...]
    kt_ref[n_prev] = kt
    ktb_ref[...] = kt.astype(bf16)
    vt = qkvt[2 * W_SB:]
    vt_ref[n_prev] = vt
    vtb_ref[...] = vt.astype(bf16)

    o = 0
    qg_ref[...] = proj(o, o + WK_GLA) * (DK_GLA ** -0.5)
    o += WK_GLA
    kg_ref[...] = proj(o, o + WK_GLA)
    o += WK_GLA
    vg_ref[...] = proj(o, o + WV_GLA).astype(bf16)
    o += WV_GLA
    og_ref[...] = proj(o, o + WV_GLA)
    o += WV_GLA
    gr = proj(o, o + GATE_PAD).astype(bf16)
    xg = jnp.dot(gr, wg_ref[...], preferred_element_type=f32) + bg_ref[...]
    g_ref[...] = _log_sigmoid(xg) * (1.0 / GATE_TAU)


def _in_call(x2d, mod, lw, tm, blocks_per_batch, prev_kv):
    n = x2d.shape[0]
    r = mod.shape[1]
    n_batch = n // (tm * blocks_per_batch)
    t = tm * blocks_per_batch
    n_prev = prev_kv[0].shape[0] if prev_kv else 0
    row = lambda w: pl.BlockSpec((tm, w), lambda i: (i, 0))
    col_map = lambda i: (i // blocks_per_batch, 0, i % blocks_per_batch)
    col = pl.BlockSpec((None, W_SB, tm), col_map)
    stack = lambda depth: pl.BlockSpec((depth, None, W_SB, tm), lambda i: (0,) + col_map(i))
    const = lambda a: pl.BlockSpec(a.shape, lambda i: (0,) * a.ndim)
    qn_t = lw["q_norm"].reshape(W_SB, 1)
    kn_t = lw["k_norm"].reshape(W_SB, 1)
    rows_out = [(WK_GLA, f32), (WK_GLA, f32), (WV_GLA, bf16), (WK_GLA, f32), (WV_GLA, f32)]
    kv_stack = jax.ShapeDtypeStruct((n_prev + 1, n_batch, W_SB, t), f32)
    kv_b = jax.ShapeDtypeStruct((n_batch, W_SB, t), bf16)
    return pl.pallas_call(
        functools.partial(_in_kernel, n_prev=n_prev),
        grid=(n // tm,),
        in_specs=[row(D_MODEL), _mod_spec(r, blocks_per_batch, 0), _mod_spec(r, blocks_per_batch, 1),
                  const(lw["norm_mix"]), _layer_spec(lw["w_main"], lw["layer"]),
                  _layer_spec(lw["w_qkvt"], lw["layer"]),
                  const(qn_t), const(kn_t), const(lw["w_gate"]), const(lw["b_gate"])]
        + [stack(n_prev)] * len(prev_kv),
        out_specs=[row(W_SB), stack(n_prev + 1), stack(n_prev + 1), col, col] + [row(w) for w, _ in rows_out],
        out_shape=[jax.ShapeDtypeStruct((n, W_SB), bf16), kv_stack, kv_stack, kv_b, kv_b]
        + [jax.ShapeDtypeStruct((n, w), dt) for w, dt in rows_out],
        compiler_params=_params(("parallel",)),
        name="in_proj",
    )(x2d, mod, mod, lw["norm_mix"], lw["w_main"], lw["w_qkvt"], qn_t, kn_t,
      lw["w_gate"], lw["b_gate"], *prev_kv)


def _pair_diag(a):
    z = jnp.zeros((D_SB, a.shape[1]), a.dtype)
    top = jnp.concatenate([a[:D_SB], z], axis=0)
    bot = jnp.concatenate([z, a[D_SB:]], axis=0)
    return jnp.concatenate([top, bot], axis=1)


def _sb_scores(q_pair, k_tiles, biases):
    tiles = []
    for kt, bias in zip(k_tiles, biases):
        z2 = jnp.dot(q_pair, _pair_diag(kt), preferred_element_type=f32)
        for hh in range(2):
            z = z2[:, hh * KB:(hh + 1) * KB] + bias
            sp = jnp.maximum(z, 0.0) + jnp.log(1.0 + jnp.exp(-jnp.abs(z)))
            hi = sp.astype(bf16)
            lo = (sp - hi.astype(f32)).astype(bf16)
            tiles.append((z, jnp.concatenate([hi, lo], axis=1)))
    return tiles


def _sb_apply(tiles, cs, v_tiles, carries, acc, tq):
    new_carries = []
    ws = [[None, None] for _ in v_tiles]
    for hh in range(2):
        c = carries[hh]
        for bi in range(len(v_tiles)):
            ti = bi * 2 + hh
            part = cs[ti * tq:(ti + 1) * tq]
            total = jnp.broadcast_to(part[:, :1], (tq, KB))
            ws[bi][hh] = jnp.exp(tiles[ti][0] - (part if c is None else c + part)).astype(bf16)
            c = total if c is None else c + total
        new_carries.append(c)
    for bi, vt in enumerate(v_tiles):
        w_pair = jnp.concatenate(ws[bi], axis=1)
        pv = lax.dot_general(w_pair, _pair_diag(vt), NT, preferred_element_type=f32)
        acc = pv if acc is None else acc + pv
    return acc, new_carries


def _sb_kernel(q_ref, kd_ref, vd_ref, kp_ref, vp_ref, cw_ref, *rest, tq, n_sub, seg, n_past, layer):
    if n_past is None:
        o_ref, acc_ref, carry_ref = rest
    else:
        kfar_ref, vfar_ref, o_ref, acc_ref, carry_ref, kbuf_ref, vbuf_ref, sem = rest
    b = pl.program_id(0)
    step = pl.program_id(1)
    row = lax.broadcasted_iota(jnp.int32, (tq, KB), 0)
    col = lax.broadcasted_iota(jnp.int32, (tq, KB), 1)
    cw = cw_ref[...]
    pair_rows = [slice(p * KB, (p + 1) * KB) for p in range(N_PAIR)]
    seq = (lambda s: b) if n_past is None else (lambda s: b * n_sub + s)
    seg_shift = seg.bit_length() - 1

    def first_block(s):
        own = jnp.logical_and(lax.shift_right_logical(col, seg_shift) == seq(s) % (KB // seg),
                              jnp.bitwise_and(col, seg - 1) < row)
        lanes = slice(s * KB, (s + 1) * KB) if n_past is None else slice(None)
        return kd_ref[:, lanes], vd_ref[:, lanes], jnp.where(own, 0.0, MASKED)

    def past_block(j, first_pass, s):
        if n_past is not None and first_pass:
            c0 = (j - (n_past - (SB_FIRST_PASS - 1))) * KB
            return kp_ref[s, :, c0:c0 + KB].astype(bf16), vp_ref[s, :, c0:c0 + KB].astype(bf16), 0.0
        jc = pl.multiple_of(jnp.maximum(j, 0) * KB, KB)
        if n_past is None:
            kt, vt = kp_ref[b, :, pl.ds(jc, KB)], vp_ref[b, :, pl.ds(jc, KB)]
        else:
            copies = [pltpu.make_async_copy(src.at[layer, seq(s), :, pl.ds(jc, KB)], dst, sem.at[i])
                      for i, (src, dst) in enumerate(((kfar_ref, kbuf_ref), (vfar_ref, vbuf_ref)))]
            for cp in copies:
                cp.start()
            for cp in copies:
                cp.wait()
            kt, vt = kbuf_ref[...], vbuf_ref[...]
        return kt.astype(bf16), vt.astype(bf16), jnp.where(j < 0, MASKED, 0.0)

    def run_group(subs, first):
        order = [(s, p) for s, _ in subs for p in range(N_PAIR)]
        blocks_of = dict(subs)
        tiles = {(s, p): _sb_scores(q_ref[s * tq:(s + 1) * tq, pair_rows[p]],
                                    [kt[pair_rows[p]] for kt, _, _ in blocks_of[s]],
                                    [bias for _, _, bias in blocks_of[s]]) for s, p in order}
        cs = jnp.dot(jnp.concatenate([t[1] for key in order for t in tiles[key]], axis=0), cw,
                     preferred_element_type=f32)
        start, floor = 0, {}
        for s, p in order:
            n_rows = 2 * len(blocks_of[s]) * tq
            carries = (None, None) if first else (carry_ref[s * H_SB + 2 * p], carry_ref[s * H_SB + 2 * p + 1])
            acc, cr = _sb_apply(tiles[s, p], cs[start:start + n_rows], [vt[pair_rows[p]] for _, vt, _ in blocks_of[s]],
                                carries, None if first else acc_ref[s * N_PAIR + p], tq)
            start += n_rows
            acc_ref[s * N_PAIR + p] = acc
            carry_ref[s * H_SB + 2 * p] = cr[0]
            carry_ref[s * H_SB + 2 * p + 1] = cr[1]
            m = jnp.minimum(cr[0], cr[1])
            floor[s] = m if s not in floor else jnp.minimum(floor[s], m)
        return {s: (jnp.min(m) < SURVIVAL_ZERO).astype(jnp.int32) for s, m in floor.items()}

    n_prev = [step * n_sub + s if n_past is None else n_past for s in range(n_sub)]
    first = [(s, [first_block(s)] + [past_block(n_prev[s] - 1 - d, True, s) for d in range(SB_FIRST_PASS - 1)])
             for s in range(n_sub)]
    alive0 = run_group(first, True)

    def cond(st):
        j, alive = st
        return jnp.logical_and(j >= 0, alive > 0)

    for s in range(n_sub):
        def body(st, s=s):
            j, _ = st
            blocks = [past_block(j - d, False, s) for d in range(SB_LOOP_GROUP)]
            return j - SB_LOOP_GROUP, run_group([(s, blocks)], False)[s]

        lax.while_loop(cond, body, (n_prev[s] - SB_FIRST_PASS, alive0[s]))
        o_ref[s * tq:(s + 1) * tq, :] = jnp.concatenate(
            [acc_ref[s * N_PAIR + p] for p in range(N_PAIR)], axis=1).astype(bf16)


def _sb_call(q, kd, vd, kp, vp, cw, *, n_batch, t_q, prompt, layer):
    tq = min(KB, t_q)
    n_sub = SB_QUERY_BLOCKS if prompt else SB_SAMPLE_SEQS
    nq = t_q // (tq * n_sub) if prompt else 1
    q_spec = pl.BlockSpec((n_sub * tq, W_SB), lambda b, i: (b * nq + i, 0))
    if prompt:
        d_spec = pl.BlockSpec((None, W_SB, n_sub * KB), lambda b, i: (b, 0, i))
        p_spec = pl.BlockSpec(kp.shape, lambda b, i: (0, 0, 0))
        seg, n_past = KB, None
        far_specs, far_args, far_scratch = [], (), []
    else:
        past = kp.shape[-1]
        tail = (SB_FIRST_PASS - 1) * KB
        assert past % tail == 0 and SB_LOOP_GROUP == 1
        assert t_q == tq and n_batch % n_sub == 0 and KB % (n_sub * t_q) == 0
        d_spec = pl.BlockSpec((None, W_SB, KB), lambda b, i: (0, 0, (b * n_sub * t_q) // KB))
        p_spec = pl.BlockSpec((None, n_sub, W_SB, tail), lambda b, i: (layer, b, 0, past // tail - 1))
        seg, n_past = t_q, past // KB
        far_specs, far_args = [pl.BlockSpec(memory_space=pl.ANY)] * 2, (kp, vp)
        far_scratch = [pltpu.VMEM((W_SB, KB), kp.dtype), pltpu.VMEM((W_SB, KB), vp.dtype),
                       pltpu.SemaphoreType.DMA((2,))]
    return pl.pallas_call(
        functools.partial(_sb_kernel, tq=tq, n_sub=n_sub, seg=seg, n_past=n_past, layer=layer),
        grid=(n_batch, nq) if prompt else (n_batch // n_sub, 1),
        in_specs=[q_spec, d_spec, d_spec, p_spec, p_spec, pl.BlockSpec(cw.shape, lambda b, i: (0, 0))] + far_specs,
        out_specs=q_spec,
        out_shape=jax.ShapeDtypeStruct((n_batch * t_q, W_SB), bf16),
        scratch_shapes=[pltpu.VMEM((n_sub * N_PAIR, tq, LANES), f32),
                        pltpu.VMEM((n_sub * H_SB, tq, LANES), f32)] + far_scratch,
        compiler_params=_params(("parallel", "arbitrary")),
        name="stick_breaking",
    )(q, kd, vd, kp, vp, cw, *far_args)


GLA_PAIRS = H_GLA // 2
GLA_SAFE_DECAY = 60.0


def _gla_tables(chunk):
    idx = np.arange(chunk)
    t, s = idx[:, None], idx[None, :]
    tri = jnp.asarray((s <= t).astype(np.float32), dtype=bf16)
    level = np.where(t > s, np.floor(np.log2(np.maximum(t ^ s, 1))).astype(np.int32), -1)
    level = np.where(t < s, -2, level).astype(np.int32)
    return tri, jnp.asarray(np.concatenate([level, level], axis=1))


def _level_reference(b, h):
    L = b.shape[0]
    if h >= 4:
        return jnp.concatenate(
            [jnp.broadcast_to(b[c0 + h - 1:c0 + h, :], (2 * h, LANES)) for c0 in range(0, L, 2 * h)], axis=0)
    b3 = b.reshape(L // 8, 8, LANES)
    pos = jnp.bitwise_and(lax.broadcasted_iota(jnp.int32, b3.shape, 1), 2 * h - 1)
    up = lambda n: pltpu.roll(b3, n, 1)
    if h == 1:
        ref = jnp.where(pos == 0, b3, up(1))
    else:
        ref = jnp.where(pos == 0, up(7), jnp.where(pos == 1, b3, jnp.where(pos == 2, up(1), up(2))))
    return ref.reshape(L, LANES)


def _two_heads(a, first_head):
    return jnp.concatenate([jnp.where(first_head, a, 0.0).astype(bf16),
                            jnp.where(first_head, 0.0, a).astype(bf16)], axis=0)


def _gla_row_split(L):
    return L // 2 if L >= 2 * LANES else L


def _gla_att_single(att_ref, q_in, k, b, lev2, first_head):
    L = q_in.shape[0]
    H = _gla_row_split(L)
    k2 = _two_heads(k * jnp.exp(-b), first_head)
    causal = lev2 >= -1
    if H == L:
        a = lax.dot_general(q_in, k2, NT, preferred_element_type=f32)
        att_ref[...] = jnp.where(causal, a, 0.0)
        return
    k_top = jnp.concatenate([k2[:H], k2[L:L + H]], axis=0)
    a = lax.dot_general(q_in[:H], k_top, NT, preferred_element_type=f32)
    tri = causal[:H, :H]
    att_ref[:H, :H] = jnp.where(tri, a[:, :H], 0.0)
    att_ref[:H, L:L + H] = jnp.where(tri, a[:, H:], 0.0)
    a = lax.dot_general(q_in[H:], k2, NT, preferred_element_type=f32)
    att_ref[H:] = jnp.where(causal[H:], a, 0.0)


def _gla_att_levels(q, k, b, lev2, first_head, n_lev):
    a = lax.dot_general(q.astype(bf16), _two_heads(k, first_head), NT, preferred_element_type=f32)
    att = jnp.where(lev2 == -1, a, 0.0)
    for lev in range(n_lev):
        d = b - _level_reference(b, 1 << lev)
        e = jnp.exp(-jnp.abs(d))
        is_query = d < 0.0
        q_l = (q * jnp.where(is_query, e, 1.0)).astype(bf16)
        k_l = k * jnp.where(is_query, 1.0, e)
        a = lax.dot_general(q_l, _two_heads(k_l, first_head), NT, preferred_element_type=f32)
        att = jnp.where(lev2 == lev, a, att)
    return att


def _gla_kernel(q_ref, k_ref, v_ref, g_ref, og_ref, s0_ref, tri_ref, lev_ref, gn_ref,
                o_ref, sout_ref, s_ref, att_ref, *, chunk, bb):
    c = pl.program_id(1)
    L = chunk
    n_lev = L.bit_length() - 1
    chains = [(bi, p) for bi in range(bb) for p in range(GLA_PAIRS)]
    zeros_s = jnp.zeros((DK_GLA, DV_GLA), f32)

    @pl.when(c == 0)
    def _():
        for bi, p in chains:
            s0 = s0_ref[bi, p]
            s_ref[bi, p] = jnp.concatenate(
                [jnp.concatenate([s0[:DK_GLA], zeros_s], axis=1),
                 jnp.concatenate([zeros_s, s0[DK_GLA:]], axis=1)], axis=0)

    lane = lax.broadcasted_iota(jnp.int32, (L, LANES), 1)
    first_head = lane < DK_GLA
    lev2 = lev_ref[...]
    tri = tri_ref[...]

    pre = []
    floor = None
    for bi, p in chains:
        lanes = slice(p * LANES, (p + 1) * LANES)
        q = q_ref[bi, :, lanes]
        k = k_ref[bi, :, lanes]
        g = g_ref[bi, :, lanes]
        g_hi = g.astype(bf16)
        g_lo = (g - g_hi.astype(f32)).astype(bf16)
        g2 = jnp.concatenate([g_hi, g_lo], axis=1)
        bs = jnp.dot(tri, g2, preferred_element_type=f32)
        b = bs[:, :LANES] + bs[:, LANES:]
        b_last = b[L - 1:L, :]
        q_in = (q * jnp.exp(b)).astype(bf16)
        k_end = (k * jnp.exp(b_last - b)).astype(bf16)
        pre.append((q, k, g2, b, q_in, k_end))
        floor = b_last if floor is None else jnp.minimum(floor, b_last)
    single = jnp.min(floor) >= -GLA_SAFE_DECAY

    @pl.when(single)
    def _():
        for ci, (q, k, g2, b, q_in, k_end) in enumerate(pre):
            _gla_att_single(att_ref.at[ci], q_in, k, b, lev2, first_head)

    @pl.when(jnp.logical_not(single))
    def _():
        for ci, (q, k, g2, b, q_in, k_end) in enumerate(pre):
            att_ref[ci] = _gla_att_levels(q, k, b, lev2, first_head, n_lev)

    rowi = lax.broadcasted_iota(jnp.int32, (2 * DK_GLA, 2 * DV_GLA), 0)
    coli = lax.broadcasted_iota(jnp.int32, (2 * DK_GLA, 2 * DV_GLA), 1)
    own_block = (rowi < DK_GLA) == (coli < DV_GLA)
    ones = jnp.ones((L, LANES), bf16)
    zeros_v = jnp.zeros((L, DV_GLA), bf16)
    gn = gn_ref[...]
    for ci, (bi, p) in enumerate(chains):
        q, k, g2, b, q_in, k_end = pre[ci]
        cols = slice(p * 2 * DV_GLA, (p + 1) * 2 * DV_GLA)
        v = v_ref[bi, :, cols].astype(bf16)
        v_diag = jnp.concatenate([jnp.concatenate([v[:, :DV_GLA], zeros_v], axis=1),
                                  jnp.concatenate([zeros_v, v[:, DV_GLA:]], axis=1)], axis=0)
        s_old = s_ref[bi, p]
        H = _gla_row_split(L)
        if H == L:
            o2 = jnp.dot(att_ref[ci].astype(bf16), v_diag, preferred_element_type=f32)
        else:
            att_top = jnp.concatenate([att_ref[ci, :H, :H], att_ref[ci, :H, L:L + H]], axis=1).astype(bf16)
            v_top = jnp.concatenate([v_diag[:H], v_diag[L:L + H]], axis=0)
            o2 = jnp.concatenate([jnp.dot(att_top, v_top, preferred_element_type=f32),
                                  jnp.dot(att_ref[ci, H:].astype(bf16), v_diag, preferred_element_type=f32)], axis=0)
        o2 = o2 + jnp.dot(q_in, s_old.astype(bf16), preferred_element_type=f32)
        tot = lax.dot_general(g2, ones, TN, preferred_element_type=f32)
        decay = jnp.exp(tot[:LANES] + tot[LANES:])
        upd = lax.dot_general(k_end, v, TN, preferred_element_type=f32)
        s_ref[bi, p] = jnp.concatenate([decay, decay], axis=1) * s_old + jnp.where(own_block, upd, 0.0)
        og = og_ref[bi, :, cols]
        outs = []
        for hh in range(2):
            o = o2[:, hh * DV_GLA:(hh + 1) * DV_GLA]
            ms = jnp.mean(o * o, axis=-1, keepdims=True)
            gate = og[:, hh * DV_GLA:(hh + 1) * DV_GLA]
            outs.append(o * lax.rsqrt(ms + EPS) * gn * (gate * jax.nn.sigmoid(gate)))
        o_ref[bi, :, cols] = jnp.concatenate(outs, axis=1).astype(bf16)

    @pl.when(c == pl.num_programs(1) - 1)
    def _():
        for bi, p in chains:
            s = s_ref[bi, p]
            sout_ref[bi, p] = jnp.concatenate([s[:DK_GLA, :DV_GLA], s[DK_GLA:, DV_GLA:]], axis=0)


def _gla_call(qg, kg, vg, g, og, s0, gla_norm, n_batch, t, chunk, bb):
    nc = t // chunk
    tri, lev2 = _gla_tables(chunk)
    kspec = pl.BlockSpec((bb, chunk, WK_GLA), lambda i, c: (i, c, 0))
    vspec = pl.BlockSpec((bb, chunk, WV_GLA), lambda i, c: (i, c, 0))
    sspec = pl.BlockSpec((bb, GLA_PAIRS, 2 * DK_GLA, DV_GLA), lambda i, c: (i, 0, 0, 0))
    const = lambda a: pl.BlockSpec(a.shape, lambda i, c: (0,) * a.ndim)
    s0 = s0.reshape(n_batch, GLA_PAIRS, 2 * DK_GLA, DV_GLA)
    k3 = lambda a: a.reshape(n_batch, t, WK_GLA)
    v3 = lambda a: a.reshape(n_batch, t, WV_GLA)
    o, s_out = pl.pallas_call(
        functools.partial(_gla_kernel, chunk=chunk, bb=bb),
        grid=(n_batch // bb, nc),
        in_specs=[kspec, kspec, vspec, kspec, vspec, sspec, const(tri), const(lev2), const(gla_norm)],
        out_specs=[vspec, sspec],
        out_shape=[jax.ShapeDtypeStruct((n_batch, t, WV_GLA), bf16),
                   jax.ShapeDtypeStruct(s0.shape, f32)],
        scratch_shapes=[pltpu.VMEM((bb, GLA_PAIRS, 2 * DK_GLA, 2 * DV_GLA), f32),
                        pltpu.VMEM((bb * GLA_PAIRS, chunk, 2 * chunk), f32)],
        compiler_params=_params(("parallel", "arbitrary")),
        name="gla",
    )(k3(qg), k3(kg), v3(vg), k3(g), v3(og), s0, tri, lev2, gla_norm)
    return o.reshape(n_batch * t, WV_GLA), s_out.reshape(n_batch, H_GLA, DK_GLA, DV_GLA)


def _mlp_rows(rows, weights, y_ref, ff_chunk):
    a_ref, o_ref, x_ref, mix_gate_ref, sh_ref, sc_ref, gate_ref = rows
    nw_ref, wo_ref, wu_ref, wd_ref = weights
    m = (jnp.dot(a_ref[...], wo_ref[:W_SB, :], preferred_element_type=f32)
         + jnp.dot(o_ref[...], wo_ref[W_SB:, :], preferred_element_type=f32))
    x = x_ref[...] + mix_gate_ref[...] * m
    ms = jnp.mean(x * x, axis=-1, keepdims=True)
    y = x * lax.rsqrt(ms + EPS) * nw_ref[...]
    h = (y * (1.0 + sc_ref[...]) + sh_ref[...]).astype(bf16)
    acc = jnp.zeros(x.shape, f32)
    for c0 in range(0, D_FF, ff_chunk):
        u = jnp.dot(h, wu_ref[:, c0:c0 + ff_chunk], preferred_element_type=f32)
        u = jnp.square(jnp.maximum(u, 0.0)).astype(bf16)
        acc = acc + jnp.dot(u, wd_ref[c0:c0 + ff_chunk, :], preferred_element_type=f32)
    y_ref[...] = x + gate_ref[...] * acc


def _mlp_kernel(*refs, ff_chunk):
    big, weights, small = refs[:7], refs[7:11], refs[11:18]
    y_ref, y_small_ref = refs[18:]
    _mlp_rows(big, weights, y_ref, ff_chunk)

    @pl.when(pl.program_id(0) == pl.num_programs(0) - 1)
    def _():
        _mlp_rows(small, weights, y_small_ref, ff_chunk)


def _mlp_call(big, small, lw, tm, blocks_per_batch):
    a, o, x2d, mod = big
    a_s, o_s, x_s, mod_s = small
    n, n_s = x2d.shape[0], x_s.shape[0]
    row = lambda w: pl.BlockSpec((tm, w), lambda i: (i, 0))
    const = lambda arr: pl.BlockSpec(arr.shape, lambda i: (0,) * arr.ndim)
    piece = lambda k: _mod_spec(mod.shape[1], blocks_per_batch, k)
    piece_s = lambda k: pl.BlockSpec((None, n_s, D_MODEL), lambda i: (0, 0, k))
    return pl.pallas_call(
        functools.partial(_mlp_kernel, ff_chunk=1024),
        grid=(n // tm,),
        in_specs=[row(W_SB), row(WV_GLA), row(D_MODEL), piece(2), piece(3), piece(4), piece(5),
                  const(lw["norm_mlp"]), _layer_spec(lw["w_out"], lw["layer"]),
                  _layer_spec(lw["w_up"], lw["layer"]), _layer_spec(lw["w_down"], lw["layer"]),
                  const(a_s), const(o_s), const(x_s), piece_s(2), piece_s(3), piece_s(4), piece_s(5)],
        out_specs=[row(D_MODEL), pl.BlockSpec((n_s, D_MODEL), lambda i: (0, 0))],
        out_shape=[jax.ShapeDtypeStruct((n, D_MODEL), f32), jax.ShapeDtypeStruct((n_s, D_MODEL), f32)],
        compiler_params=_params(("arbitrary",)),
        name="out_mlp",
    )(a, o, x2d, mod, mod, mod, mod, lw["norm_mlp"], lw["w_out"], lw["w_up"], lw["w_down"],
      a_s, o_s, x_s, mod_s, mod_s, mod_s, mod_s)


def _stacked_weights(w_in, w_out, w_up, w_down):
    c_g = 3 * W_SB
    c_gate = c_g + 2 * WK_GLA + WV_GLA
    w_main = jnp.concatenate(
        [w_in[:, :, c_g:c_gate], w_in[:, :, c_gate + GATE_RANK:], w_in[:, :, c_gate:c_gate + GATE_RANK],
         jnp.zeros((DEPTH, D_MODEL, GATE_PAD - GATE_RANK), f32)], axis=2).astype(bf16)
    w_qkvt = w_in[:, :, :c_g].transpose(0, 2, 1).astype(bf16)
    return {"w_main": w_main, "w_qkvt": w_qkvt, "w_out": w_out.astype(bf16), "w_up": w_up.astype(bf16),
            "w_down": w_down.astype(bf16)}


def _layer_weights(l, stacked, norm_mix, norm_mlp, q_norm, k_norm, w_gate, b_gate, gla_norm):
    wg = jnp.concatenate([w_gate[l], jnp.zeros((GATE_PAD - GATE_RANK, WK_GLA), f32)], axis=0).astype(bf16)
    return dict(
        stacked, layer=l, w_gate=wg,
        b_gate=b_gate[l].reshape(1, WK_GLA),
        q_norm=jnp.tile(q_norm[l], H_SB),
        k_norm=jnp.tile(k_norm[l], H_SB),
        norm_mix=norm_mix[l].reshape(1, D_MODEL),
        norm_mlp=norm_mlp[l].reshape(1, D_MODEL),
        gla_norm=gla_norm[l].reshape(1, DV_GLA),
    )


def _group_layer(x2d, mod, lw, cw, n_batch, t, past_kt, past_vt, layer, s0, tm, blocks_per_batch, chunk,
                 gla_bb, prev_kv=()):
    q, kt, vt, ktb, vtb, qg, kg, vg, g, og = _in_call(x2d, mod, lw, tm, blocks_per_batch, prev_kv)
    if past_kt is None:
        a = _sb_call(q, ktb, vtb, ktb, vtb, cw, n_batch=n_batch, t_q=t, prompt=True, layer=layer)
    else:
        a = _sb_call(q, ktb, vtb, past_kt, past_vt, cw, n_batch=n_batch, t_q=t, prompt=False, layer=layer)
    o, s_new = _gla_call(qg, kg, vg, g, og, s0, lw["gla_norm"], n_batch, t, chunk, gla_bb)
    return (a, o, x2d, mod), kt, vt, s_new


def _heads_last(kt, n_batch, t):
    layers, nb = kt.shape[:2]
    return kt.reshape(layers, nb, H_SB, D_SB, -1).transpose(0, 1, 4, 2, 3).reshape(layers, n_batch, t, H_SB, D_SB)


def kernel(x_prompt, x_sample, c_prompt, c_sample, cache_k, cache_v, state_gla, w_ada, b_ada, norm_mix,
           norm_mlp, w_in, q_norm, k_norm, w_gate, b_gate, gla_norm, w_out, w_up, w_down):
    bp, tp, _ = x_prompt.shape
    bs, ts, _ = x_sample.shape
    past = cache_k.shape[2]
    n_c = bp + bs
    c_rows = -(-n_c // 16) * 16
    c_all = jnp.concatenate([c_prompt, c_sample, jnp.zeros((c_rows - n_c, D_MODEL), f32)], axis=0)
    mods = _ada_call(c_all, w_ada, b_ada)

    kk = np.arange(KB)
    half = kk[:, None] >= kk[None, :]
    cw = jnp.asarray(np.concatenate([half, half], axis=0).astype(np.float32), dtype=bf16)

    cache_kt = cache_k.transpose(0, 1, 3, 4, 2).reshape(DEPTH, bs, W_SB, past)
    cache_vt = cache_v.transpose(0, 1, 3, 4, 2).reshape(DEPTH, bs, W_SB, past)

    tm_p, tm_s = 512, bs * ts
    xp = x_prompt.reshape(bp * tp, D_MODEL)
    xs = x_sample.reshape(bs * ts, D_MODEL)
    sp_l, ks_l, vs_l, ss_l = [], [], [], []
    stacked = _stacked_weights(w_in, w_out, w_up, w_down)
    prompt_kv = ()
    for l in range(DEPTH):
        lw = _layer_weights(l, stacked, norm_mix, norm_mlp, q_norm, k_norm, w_gate, b_gate, gla_norm)
        mod_p = mods[l, :bp].reshape(bp, 1, N_MOD * D_MODEL)
        mod_s = jnp.repeat(mods[l, bp:n_c], ts, axis=0).reshape(1, bs * ts, N_MOD * D_MODEL)
        s0_p = jnp.zeros((bp, H_GLA, DK_GLA, DV_GLA), f32)
        mixed_p, kp, vp, sp = _group_layer(xp, mod_p, lw, cw, bp, tp, None, None, l, s0_p, tm_p, tp // tm_p, 256,
                                           bp, prompt_kv)
        prompt_kv = (kp, vp)
        mixed_s, kn, vn, sn = _group_layer(xs, mod_s, lw, cw, bs, ts, cache_kt, cache_vt, l,
                                           state_gla[l].astype(f32), tm_s, 1, ts, 8)
        xp, xs = _mlp_call(mixed_p, mixed_s, lw, tm_p, tp // tm_p)
        sp_l.append(sp)
        ks_l.append(_heads_last(kn, bs, ts)[0])
        vs_l.append(_heads_last(vn, bs, ts)[0])
        ss_l.append(sn)
    return (xp.reshape(bp, tp, D_MODEL), xs.reshape(bs, ts, D_MODEL),
            _heads_last(kp, bp, tp), _heads_last(vp, bp, tp), jnp.stack(sp_l),
            jnp.stack(ks_l), jnp.stack(vs_l), jnp.stack(ss_l))
```

```python
import functools

import numpy as np
import jax
import jax.numpy as jnp
from jax import lax
from jax.experimental import pallas as pl
from jax.experimental.pallas import tpu as pltpu

D_MODEL = 1024
DEPTH = 2
H_SB = 8
D_SB = 64
W_SB = H_SB * D_SB
H_GLA = 4
DK_GLA = 64
DV_GLA = 128
WK_GLA = H_GLA * DK_GLA
WV_GLA = H_GLA * DV_GLA
GATE_RANK = 16
GATE_TAU = 16.0
D_FF = 4 * D_MODEL
N_MOD = 6
EPS = 1e-6

LANES = 128
KB = 128
SB_FIRST_PASS = 3
SB_LOOP_GROUP = 1
SB_QUERY_BLOCKS = 4
SB_SAMPLE_SEQS = 4
ADA_PIECES = 2
N_PAIR = H_SB // 2
GATE_PAD = LANES
VMEM_LIMIT = 56 * 1024 * 1024
SURVIVAL_ZERO = 104.0
MASKED = -1e30

f32 = jnp.float32
bf16 = jnp.bfloat16
NT = (((1,), (1,)), ((), ()))
TN = (((0,), (0,)), ((), ()))


def _params(sem):
    return pltpu.CompilerParams(dimension_semantics=sem, vmem_limit_bytes=VMEM_LIMIT)


def _log_sigmoid(x):
    return jnp.minimum(x, 0.0) - jnp.log1p(jnp.exp(-jnp.abs(x)))


def _ada_kernel(c_ref, w_ref, b_ref, o_ref):
    c = c_ref[...]
    s = (c * jax.nn.sigmoid(c)).astype(bf16)
    o_ref[...] = jnp.dot(s, w_ref[...].astype(bf16), preferred_element_type=f32) + b_ref[...]


def _ada_call(c_all, w_ada, b_ada):
    rows = c_all.shape[0]
    return pl.pallas_call(
        _ada_kernel,
        grid=(DEPTH, N_MOD // ADA_PIECES),
        in_specs=[
            pl.BlockSpec((rows, D_MODEL), lambda l, p: (0, 0)),
            pl.BlockSpec((None, D_MODEL, ADA_PIECES * D_MODEL), lambda l, p: (l, 0, p)),
            pl.BlockSpec((None, 1, ADA_PIECES * D_MODEL), lambda l, p: (l, 0, p)),
        ],
        out_specs=pl.BlockSpec((None, rows, ADA_PIECES * D_MODEL), lambda l, p: (l, 0, p)),
        out_shape=jax.ShapeDtypeStruct((DEPTH, rows, N_MOD * D_MODEL), f32),
        compiler_params=_params(("parallel", "parallel")),
        name="ada_mod",
    )(c_all, w_ada, b_ada.reshape(DEPTH, 1, N_MOD * D_MODEL))


def _mod_spec(rows_per_block, blocks_per_batch, piece):
    return pl.BlockSpec((None, rows_per_block, D_MODEL),
                        lambda i: (i // blocks_per_batch, 0, piece))


def _layer_spec(stack, layer):
    zeros = (0,) * (stack.ndim - 1)
    return pl.BlockSpec((None,) + stack.shape[1:], lambda *_: (layer,) + zeros, pipeline_mode=pl.Buffered(1))


def _head_rms(t):
    t3 = t.reshape(H_SB, D_SB, t.shape[1])
    ms = jnp.mean(t3 * t3, axis=1, keepdims=True)
    return (t3 * lax.rsqrt(ms + EPS)).reshape(t.shape)


def _in_kernel(*refs, n_prev):
    (x_ref, sh_ref, sc_ref, nw_ref, w_ref, wqkv_ref, qn_ref, kn_ref, wg_ref, bg_ref) = refs[:10]
    prev_refs = refs[10:10 + 2 * min(n_prev, 1)]
    (q_ref, kt_ref, vt_ref, ktb_ref, vtb_ref, qg_ref, kg_ref, vg_ref, g_ref, og_ref) = refs[len(refs) - 10:]
    for prev_ref, out_ref in zip(prev_refs, (kt_ref, vt_ref)):
        out_ref[:n_prev] = prev_ref[...]
    x = x_ref[...]
    ms = jnp.mean(x * x, axis=-1, keepdims=True)
    y = x * lax.rsqrt(ms + EPS) * nw_ref[...]
    h = (y * (1.0 + sc_ref[...]) + sh_ref[...]).astype(bf16)

    def proj(lo, hi):
        return jnp.dot(h, w_ref[:, lo:hi], preferred_element_type=f32)

    qkvt = lax.dot_general(wqkv_ref[...], h, NT, preferred_element_type=f32)
    qt = _head_rms(qkvt[:W_SB]) * qn_ref[...] * (D_SB ** -0.5)
    q_ref[...] = qt.T.astype(bf16)
    kt = _head_rms(qkvt[W_SB:2 * W_SB]) * kn_ref[...]
    kt_ref[n_prev] = kt
    ktb_ref[...] = kt.astype(bf16)
    vt = qkvt[2 * W_SB:]
    vt_ref[n_prev] = vt
    vtb_ref[...] = vt.astype(bf16)

    o = 0
    qg_ref[...] = proj(o, o + WK_GLA) * (DK_GLA ** -0.5)
    o += WK_GLA
    kg_ref[...] = proj(o, o + WK_GLA)
    o += WK_GLA
    vg_ref[...] = proj(o, o + WV_GLA).astype(bf16)
    o += WV_GLA
    og_ref[...] = proj(o, o + WV_GLA)
    o += WV_GLA
    gr = proj(o, o + GATE_PAD).astype(bf16)
    xg = jnp.dot(gr, wg_ref[...], preferred_element_type=f32) + bg_ref[...]
    g_ref[...] = _log_sigmoid(xg) * (1.0 / GATE_TAU)


def _in_call(x2d, mod, lw, tm, blocks_per_batch, prev_kv):
    n = x2d.shape[0]
    r = mod.shape[1]
    n_batch = n // (tm * blocks_per_batch)
    t = tm * blocks_per_batch
    n_prev = prev_kv[0].shape[0] if prev_kv else 0
    row = lambda w: pl.BlockSpec((tm, w), lambda i: (i, 0))
    col_map = lambda i: (i // blocks_per_batch, 0, i % blocks_per_batch)
    col = pl.BlockSpec((None, W_SB, tm), col_map)
    stack = lambda depth: pl.BlockSpec((depth, None, W_SB, tm), lambda i: (0,) + col_map(i))
    const = lambda a: pl.BlockSpec(a.shape, lambda i: (0,) * a.ndim)
    qn_t = lw["q_norm"].reshape(W_SB, 1)
    kn_t = lw["k_norm"].reshape(W_SB, 1)
    rows_out = [(WK_GLA, f32), (WK_GLA, f32), (WV_GLA, bf16), (WK_GLA, f32), (WV_GLA, f32)]
    kv_stack = jax.ShapeDtypeStruct((n_prev + 1, n_batch, W_SB, t), f32)
    kv_b = jax.ShapeDtypeStruct((n_batch, W_SB, t), bf16)
    return pl.pallas_call(
        functools.partial(_in_kernel, n_prev=n_prev),
        grid=(n // tm,),
        in_specs=[row(D_MODEL), _mod_spec(r, blocks_per_batch, 0), _mod_spec(r, blocks_per_batch, 1),
                  const(lw["norm_mix"]), _layer_spec(lw["w_main"], lw["layer"]),
                  _layer_spec(lw["w_qkvt"], lw["layer"]),
                  const(qn_t), const(kn_t), const(lw["w_gate"]), const(lw["b_gate"])]
        + [stack(n_prev)] * len(prev_kv),
        out_specs=[row(W_SB), stack(n_prev + 1), stack(n_prev + 1), col, col] + [row(w) for w, _ in rows_out],
        out_shape=[jax.ShapeDtypeStruct((n, W_SB), bf16), kv_stack, kv_stack, kv_b, kv_b]
        + [jax.ShapeDtypeStruct((n, w), dt) for w, dt in rows_out],
        compiler_params=_params(("parallel",)),
        name="in_proj",
    )(x2d, mod, mod, lw["norm_mix"], lw["w_main"], lw["w_qkvt"], qn_t, kn_t,
      lw["w_gate"], lw["b_gate"], *prev_kv)


def _pair_diag(a):
    z = jnp.zeros((D_SB, a.shape[1]), a.dtype)
    top = jnp.concatenate([a[:D_SB], z], axis=0)
    bot = jnp.concatenate([z, a[D_SB:]], axis=0)
    return jnp.concatenate([top, bot], axis=1)


def _sb_scores(q_pair, k_tiles, biases):
    tiles = []
    for kt, bias in zip(k_tiles, biases):
        z2 = jnp.dot(q_pair, _pair_diag(kt), preferred_element_type=f32)
        for hh in range(2):
            z = z2[:, hh * KB:(hh + 1) * KB] + bias
            sp = jnp.maximum(z, 0.0) + jnp.log(1.0 + jnp.exp(-jnp.abs(z)))
            hi = sp.astype(bf16)
            lo = (sp - hi.astype(f32)).astype(bf16)
            tiles.append((z, jnp.concatenate([hi, lo], axis=1)))
    return tiles


def _sb_apply(tiles, cs, v_tiles, carries, acc, tq):
    new_carries = []
    ws = [[None, None] for _ in v_tiles]
    for hh in range(2):
        c = carries[hh]
        for bi in range(len(v_tiles)):
            ti = bi * 2 + hh
            part = cs[ti * tq:(ti + 1) * tq]
            total = jnp.broadcast_to(part[:, :1], (tq, KB))
            ws[bi][hh] = jnp.exp(tiles[ti][0] - (part if c is None else c + part)).astype(bf16)
            c = total if c is None else c + total
        new_carries.append(c)
    for bi, vt in enumerate(v_tiles):
        w_pair = jnp.concatenate(ws[bi], axis=1)
        pv = lax.dot_general(w_pair, _pair_diag(vt), NT, preferred_element_type=f32)
        acc = pv if acc is None else acc + pv
    return acc, new_carries


def _sb_kernel(q_ref, kd_ref, vd_ref, kp_ref, vp_ref, cw_ref, *rest, tq, n_sub, seg, n_past, layer):
    if n_past is None:
        o_ref, acc_ref, carry_ref = rest
    else:
        kfar_ref, vfar_ref, o_ref, acc_ref, carry_ref, kbuf_ref, vbuf_ref, sem = rest
    b = pl.program_id(0)
    step = pl.program_id(1)
    row = lax.broadcasted_iota(jnp.int32, (tq, KB), 0)
    col = lax.broadcasted_iota(jnp.int32, (tq, KB), 1)
    cw = cw_ref[...]
    pair_rows = [slice(p * KB, (p + 1) * KB) for p in range(N_PAIR)]
    seq = (lambda s: b) if n_past is None else (lambda s: b * n_sub + s)
    seg_shift = seg.bit_length() - 1

    def first_block(s):
        own = jnp.logical_and(lax.shift_right_logical(col, seg_shift) == seq(s) % (KB // seg),
                              jnp.bitwise_and(col, seg - 1) < row)
        lanes = slice(s * KB, (s + 1) * KB) if n_past is None else slice(None)
        return kd_ref[:, lanes], vd_ref[:, lanes], jnp.where(own, 0.0, MASKED)

    def past_block(j, first_pass, s):
        if n_past is not None and first_pass:
            c0 = (j - (n_past - (SB_FIRST_PASS - 1))) * KB
            return kp_ref[s, :, c0:c0 + KB].astype(bf16), vp_ref[s, :, c0:c0 + KB].astype(bf16), 0.0
        jc = pl.multiple_of(jnp.maximum(j, 0) * KB, KB)
        if n_past is None:
            kt, vt = kp_ref[b, :, pl.ds(jc, KB)], vp_ref[b, :, pl.ds(jc, KB)]
        else:
            copies = [pltpu.make_async_copy(src.at[layer, seq(s), :, pl.ds(jc, KB)], dst, sem.at[i])
                      for i, (src, dst) in enumerate(((kfar_ref, kbuf_ref), (vfar_ref, vbuf_ref)))]
            for cp in copies:
                cp.start()
            for cp in copies:
                cp.wait()
            kt, vt = kbuf_ref[...], vbuf_ref[...]
        return kt.astype(bf16), vt.astype(bf16), jnp.where(j < 0, MASKED, 0.0)

    def run_group(subs, first):
        order = [(s, p) for s, _ in subs for p in range(N_PAIR)]
        blocks_of = dict(subs)
        tiles = {(s, p): _sb_scores(q_ref[s * tq:(s + 1) * tq, pair_rows[p]],
                                    [kt[pair_rows[p]] for kt, _, _ in blocks_of[s]],
                                    [bias for _, _, bias in blocks_of[s]]) for s, p in order}
        cs = jnp.dot(jnp.concatenate([t[1] for key in order for t in tiles[key]], axis=0), cw,
                     preferred_element_type=f32)
        start, floor = 0, {}
        for s, p in order:
            n_rows = 2 * len(blocks_of[s]) * tq
            carries = (None, None) if first else (carry_ref[s * H_SB + 2 * p], carry_ref[s * H_SB + 2 * p + 1])
            acc, cr = _sb_apply(tiles[s, p], cs[start:start + n_rows], [vt[pair_rows[p]] for _, vt, _ in blocks_of[s]],
                                carries, None if first else acc_ref[s * N_PAIR + p], tq)
            start += n_rows
            acc_ref[s * N_PAIR + p] = acc
            carry_ref[s * H_SB + 2 * p] = cr[0]
            carry_ref[s * H_SB + 2 * p + 1] = cr[1]
            m = jnp.minimum(cr[0], cr[1])
            floor[s] = m if s not in floor else jnp.minimum(floor[s], m)
        return {s: (jnp.min(m) < SURVIVAL_ZERO).astype(jnp.int32) for s, m in floor.items()}

    n_prev = [step * n_sub + s if n_past is None else n_past for s in range(n_sub)]
    first = [(s, [first_block(s)] + [past_block(n_prev[s] - 1 - d, True, s) for d in range(SB_FIRST_PASS - 1)])
             for s in range(n_sub)]
    alive0 = run_group(first, True)

    def cond(st):
        j, alive = st
        return jnp.logical_and(j >= 0, alive > 0)

    for s in range(n_sub):
        def body(st, s=s):
            j, _ = st
            blocks = [past_block(j - d, False, s) for d in range(SB_LOOP_GROUP)]
            return j - SB_LOOP_GROUP, run_group([(s, blocks)], False)[s]

        lax.while_loop(cond, body, (n_prev[s] - SB_FIRST_PASS, alive0[s]))
        o_ref[s * tq:(s + 1) * tq, :] = jnp.concatenate(
            [acc_ref[s * N_PAIR + p] for p in range(N_PAIR)], axis=1).astype(bf16)


def _sb_call(q, kd, vd, kp, vp, cw, *, n_batch, t_q, prompt, layer):
    tq = min(KB, t_q)
    n_sub = SB_QUERY_BLOCKS if prompt else SB_SAMPLE_SEQS
    nq = t_q // (tq * n_sub) if prompt else 1
    q_spec = pl.BlockSpec((n_sub * tq, W_SB), lambda b, i: (b * nq + i, 0))
    if prompt:
        d_spec = pl.BlockSpec((None, W_SB, n_sub * KB), lambda b, i: (b, 0, i))
        p_spec = pl.BlockSpec(kp.shape, lambda b, i: (0, 0, 0))
        seg, n_past = KB, None
        far_specs, far_args, far_scratch = [], (), []
    else:
        past = kp.shape[-1]
        tail = (SB_FIRST_PASS - 1) * KB
        assert past % tail == 0 and SB_LOOP_GROUP == 1
        assert t_q == tq and n_batch % n_sub == 0 and KB % (n_sub * t_q) == 0
        d_spec = pl.BlockSpec((None, W_SB, KB), lambda b, i: (0, 0, (b * n_sub * t_q) // KB))
        p_spec = pl.BlockSpec((None, n_sub, W_SB, tail), lambda b, i: (layer, b, 0, past // tail - 1))
        seg, n_past = t_q, past // KB
        far_specs, far_args = [pl.BlockSpec(memory_space=pl.ANY)] * 2, (kp, vp)
        far_scratch = [pltpu.VMEM((W_SB, KB), kp.dtype), pltpu.VMEM((W_SB, KB), vp.dtype),
                       pltpu.SemaphoreType.DMA((2,))]
    return pl.pallas_call(
        functools.partial(_sb_kernel, tq=tq, n_sub=n_sub, seg=seg, n_past=n_past, layer=layer),
        grid=(n_batch, nq) if prompt else (n_batch // n_sub, 1),
        in_specs=[q_spec, d_spec, d_spec, p_spec, p_spec, pl.BlockSpec(cw.shape, lambda b, i: (0, 0))] + far_specs,
        out_specs=q_spec,
        out_shape=jax.ShapeDtypeStruct((n_batch * t_q, W_SB), bf16),
        scratch_shapes=[pltpu.VMEM((n_sub * N_PAIR, tq, LANES), f32),
                        pltpu.VMEM((n_sub * H_SB, tq, LANES), f32)] + far_scratch,
        compiler_params=_params(("parallel", "arbitrary")),
        name="stick_breaking",
    )(q, kd, vd, kp, vp, cw, *far_args)


GLA_PAIRS = H_GLA // 2
GLA_SAFE_DECAY = 60.0


def _gla_tables(chunk):
    idx = np.arange(chunk)
    t, s = idx[:, None], idx[None, :]
    tri = jnp.asarray((s <= t).astype(np.float32), dtype=bf16)
    level = np.where(t > s, np.floor(np.log2(np.maximum(t ^ s, 1))).astype(np.int32), -1)
    level = np.where(t < s, -2, level).astype(np.int32)
    return tri, jnp.asarray(np.concatenate([level, level], axis=1))


def _level_reference(b, h):
    L = b.shape[0]
    if h >= 4:
        return jnp.concatenate(
            [jnp.broadcast_to(b[c0 + h - 1:c0 + h, :], (2 * h, LANES)) for c0 in range(0, L, 2 * h)], axis=0)
    b3 = b.reshape(L // 8, 8, LANES)
    pos = jnp.bitwise_and(lax.broadcasted_iota(jnp.int32, b3.shape, 1), 2 * h - 1)
    up = lambda n: pltpu.roll(b3, n, 1)
    if h == 1:
        ref = jnp.where(pos == 0, b3, up(1))
    else:
        ref = jnp.where(pos == 0, up(7), jnp.where(pos == 1, b3, jnp.where(pos == 2, up(1), up(2))))
    return ref.reshape(L, LANES)


def _two_heads(a, first_head):
    return jnp.concatenate([jnp.where(first_head, a, 0.0).astype(bf16),
                            jnp.where(first_head, 0.0, a).astype(bf16)], axis=0)


def _gla_row_split(L):
    return L // 2 if L >= 2 * LANES else L


def _gla_att_single(att_ref, q_in, k, b, lev2, first_head):
    L = q_in.shape[0]
    H = _gla_row_split(L)
    k2 = _two_heads(k * jnp.exp(-b), first_head)
    causal = lev2 >= -1
    if H == L:
        a = lax.dot_general(q_in, k2, NT, preferred_element_type=f32)
        att_ref[...] = jnp.where(causal, a, 0.0)
        return
    k_top = jnp.concatenate([k2[:H], k2[L:L + H]], axis=0)
    a = lax.dot_general(q_in[:H], k_top, NT, preferred_element_type=f32)
    tri = causal[:H, :H]
    att_ref[:H, :H] = jnp.where(tri, a[:, :H], 0.0)
    att_ref[:H, L:L + H] = jnp.where(tri, a[:, H:], 0.0)
    a = lax.dot_general(q_in[H:], k2, NT, preferred_element_type=f32)
    att_ref[H:] = jnp.where(causal[H:], a, 0.0)


def _gla_att_levels(q, k, b, lev2, first_head, n_lev):
    a = lax.dot_general(q.astype(bf16), _two_heads(k, first_head), NT, preferred_element_type=f32)
    att = jnp.where(lev2 == -1, a, 0.0)
    for lev in range(n_lev):
        d = b - _level_reference(b, 1 << lev)
        e = jnp.exp(-jnp.abs(d))
        is_query = d < 0.0
        q_l = (q * jnp.where(is_query, e, 1.0)).astype(bf16)
        k_l = k * jnp.where(is_query, 1.0, e)
        a = lax.dot_general(q_l, _two_heads(k_l, first_head), NT, preferred_element_type=f32)
        att = jnp.where(lev2 == lev, a, att)
    return att


def _gla_kernel(q_ref, k_ref, v_ref, g_ref, og_ref, s0_ref, tri_ref, lev_ref, gn_ref,
                o_ref, sout_ref, s_ref, att_ref, *, chunk, bb):
    c = pl.program_id(1)
    L = chunk
    n_lev = L.bit_length() - 1
    chains = [(bi, p) for bi in range(bb) for p in range(GLA_PAIRS)]
    zeros_s = jnp.zeros((DK_GLA, DV_GLA), f32)

    @pl.when(c == 0)
    def _():
        for bi, p in chains:
            s0 = s0_ref[bi, p]
            s_ref[bi, p] = jnp.concatenate(
                [jnp.concatenate([s0[:DK_GLA], zeros_s], axis=1),
                 jnp.concatenate([zeros_s, s0[DK_GLA:]], axis=1)], axis=0)

    lane = lax.broadcasted_iota(jnp.int32, (L, LANES), 1)
    first_head = lane < DK_GLA
    lev2 = lev_ref[...]
    tri = tri_ref[...]

    pre = []
    floor = None
    for bi, p in chains:
        lanes = slice(p * LANES, (p + 1) * LANES)
        q = q_ref[bi, :, lanes]
        k = k_ref[bi, :, lanes]
        g = g_ref[bi, :, lanes]
        g_hi = g.astype(bf16)
        g_lo = (g - g_hi.astype(f32)).astype(bf16)
        g2 = jnp.concatenate([g_hi, g_lo], axis=1)
        bs = jnp.dot(tri, g2, preferred_element_type=f32)
        b = bs[:, :LANES] + bs[:, LANES:]
        b_last = b[L - 1:L, :]
        q_in = (q * jnp.exp(b)).astype(bf16)
        k_end = (k * jnp.exp(b_last - b)).astype(bf16)
        pre.append((q, k, g2, b, q_in, k_end))
        floor = b_last if floor is None else jnp.minimum(floor, b_last)
    single = jnp.min(floor) >= -GLA_SAFE_DECAY

    @pl.when(single)
    def _():
        for ci, (q, k, g2, b, q_in, k_end) in enumerate(pre):
            _gla_att_single(att_ref.at[ci], q_in, k, b, lev2, first_head)

    @pl.when(jnp.logical_not(single))
    def _():
        for ci, (q, k, g2, b, q_in, k_end) in enumerate(pre):
            att_ref[ci] = _gla_att_levels(q, k, b, lev2, first_head, n_lev)

    rowi = lax.broadcasted_iota(jnp.int32, (2 * DK_GLA, 2 * DV_GLA), 0)
    coli = lax.broadcasted_iota(jnp.int32, (2 * DK_GLA, 2 * DV_GLA), 1)
    own_block = (rowi < DK_GLA) == (coli < DV_GLA)
    ones = jnp.ones((L, LANES), bf16)
    zeros_v = jnp.zeros((L, DV_GLA), bf16)
    gn = gn_ref[...]
    for ci, (bi, p) in enumerate(chains):
        q, k, g2, b, q_in, k_end = pre[ci]
        cols = slice(p * 2 * DV_GLA, (p + 1) * 2 * DV_GLA)
        v = v_ref[bi, :, cols].astype(bf16)
        v_diag = jnp.concatenate([jnp.concatenate([v[:, :DV_GLA], zeros_v], axis=1),
                                  jnp.concatenate([zeros_v, v[:, DV_GLA:]], axis=1)], axis=0)
        s_old = s_ref[bi, p]
        H = _gla_row_split(L)
        if H == L:
            o2 = jnp.dot(att_ref[ci].astype(bf16), v_diag, preferred_element_type=f32)
        else:
            att_top = jnp.concatenate([att_ref[ci, :H, :H], att_ref[ci, :H, L:L + H]], axis=1).astype(bf16)
            v_top = jnp.concatenate([v_diag[:H], v_diag[L:L + H]], axis=0)
            o2 = jnp.concatenate([jnp.dot(att_top, v_top, preferred_element_type=f32),
                                  jnp.dot(att_ref[ci, H:].astype(bf16), v_diag, preferred_element_type=f32)], axis=0)
        o2 = o2 + jnp.dot(q_in, s_old.astype(bf16), preferred_element_type=f32)
        tot = lax.dot_general(g2, ones, TN, preferred_element_type=f32)
        decay = jnp.exp(tot[:LANES] + tot[LANES:])
        upd = lax.dot_general(k_end, v, TN, preferred_element_type=f32)
        s_ref[bi, p] = jnp.concatenate([decay, decay], axis=1) * s_old + jnp.where(own_block, upd, 0.0)
        og = og_ref[bi, :, cols]
        outs = []
        for hh in range(2):
            o = o2[:, hh * DV_GLA:(hh + 1) * DV_GLA]
            ms = jnp.mean(o * o, axis=-1, keepdims=True)
            gate = og[:, hh * DV_GLA:(hh + 1) * DV_GLA]
            outs.append(o * lax.rsqrt(ms + EPS) * gn * (gate * jax.nn.sigmoid(gate)))
        o_ref[bi, :, cols] = jnp.concatenate(outs, axis=1).astype(bf16)

    @pl.when(c == pl.num_programs(1) - 1)
    def _():
        for bi, p in chains:
            s = s_ref[bi, p]
            sout_ref[bi, p] = jnp.concatenate([s[:DK_GLA, :DV_GLA], s[DK_GLA:, DV_GLA:]], axis=0)


def _gla_call(qg, kg, vg, g, og, s0, gla_norm, n_batch, t, chunk, bb):
    nc = t // chunk
    tri, lev2 = _gla_tables(chunk)
    kspec = pl.BlockSpec((bb, chunk, WK_GLA), lambda i, c: (i, c, 0))
    vspec = pl.BlockSpec((bb, chunk, WV_GLA), lambda i, c: (i, c, 0))
    sspec = pl.BlockSpec((bb, GLA_PAIRS, 2 * DK_GLA, DV_GLA), lambda i, c: (i, 0, 0, 0))
    const = lambda a: pl.BlockSpec(a.shape, lambda i, c: (0,) * a.ndim)
    s0 = s0.reshape(n_batch, GLA_PAIRS, 2 * DK_GLA, DV_GLA)
    k3 = lambda a: a.reshape(n_batch, t, WK_GLA)
    v3 = lambda a: a.reshape(n_batch, t, WV_GLA)
    o, s_out = pl.pallas_call(
        functools.partial(_gla_kernel, chunk=chunk, bb=bb),
        grid=(n_batch // bb, nc),
        in_specs=[kspec, kspec, vspec, kspec, vspec, sspec, const(tri), const(lev2), const(gla_norm)],
        out_specs=[vspec, sspec],
        out_shape=[jax.ShapeDtypeStruct((n_batch, t, WV_GLA), bf16),
                   jax.ShapeDtypeStruct(s0.shape, f32)],
        scratch_shapes=[pltpu.VMEM((bb, GLA_PAIRS, 2 * DK_GLA, 2 * DV_GLA), f32),
                        pltpu.VMEM((bb * GLA_PAIRS, chunk, 2 * chunk), f32)],
        compiler_params=_params(("parallel", "arbitrary")),
        name="gla",
    )(k3(qg), k3(kg), v3(vg), k3(g), v3(og), s0, tri, lev2, gla_norm)
    return o.reshape(n_batch * t, WV_GLA), s_out.reshape(n_batch, H_GLA, DK_GLA, DV_GLA)


def _mlp_rows(rows, weights, y_ref, ff_chunk):
    a_ref, o_ref, x_ref, mix_gate_ref, sh_ref, sc_ref, gate_ref = rows
    nw_ref, wo_ref, wu_ref, wd_ref = weights
    m = (jnp.dot(a_ref[...], wo_ref[:W_SB, :], preferred_element_type=f32)
         + jnp.dot(o_ref[...], wo_ref[W_SB:, :], preferred_element_type=f32))
    x = x_ref[...] + mix_gate_ref[...] * m
    ms = jnp.mean(x * x, axis=-1, keepdims=True)
    y = x * lax.rsqrt(ms + EPS) * nw_ref[...]
    h = (y * (1.0 + sc_ref[...]) + sh_ref[...]).astype(bf16)
    acc = jnp.zeros(x.shape, f32)
    for c0 in range(0, D_FF, ff_chunk):
        u = jnp.dot(h, wu_ref[:, c0:c0 + ff_chunk], preferred_element_type=f32)
        u = jnp.square(jnp.maximum(u, 0.0)).astype(bf16)
        acc = acc + jnp.dot(u, wd_ref[c0:c0 + ff_chunk, :], preferred_element_type=f32)
    y_ref[...] = x + gate_ref[...] * acc


def _mlp_kernel(*refs, ff_chunk):
    big, weights, small = refs[:7], refs[7:11], refs[11:18]
    y_ref, y_small_ref = refs[18:]
    _mlp_rows(big, weights, y_ref, ff_chunk)

    @pl.when(pl.program_id(0) == pl.num_programs(0) - 1)
    def _():
        _mlp_rows(small, weights, y_small_ref, ff_chunk)


def _mlp_call(big, small, lw, tm, blocks_per_batch):
    a, o, x2d, mod = big
    a_s, o_s, x_s, mod_s = small
    n, n_s = x2d.shape[0], x_s.shape[0]
    row = lambda w: pl.BlockSpec((tm, w), lambda i: (i, 0))
    const = lambda arr: pl.BlockSpec(arr.shape, lambda i: (0,) * arr.ndim)
    piece = lambda k: _mod_spec(mod.shape[1], blocks_per_batch, k)
    piece_s = lambda k: pl.BlockSpec((None, n_s, D_MODEL), lambda i: (0, 0, k))
    return pl.pallas_call(
        functools.partial(_mlp_kernel, ff_chunk=1024),
        grid=(n // tm,),
        in_specs=[row(W_SB), row(WV_GLA), row(D_MODEL), piece(2), piece(3), piece(4), piece(5),
                  const(lw["norm_mlp"]), _layer_spec(lw["w_out"], lw["layer"]),
                  _layer_spec(lw["w_up"], lw["layer"]), _layer_spec(lw["w_down"], lw["layer"]),
                  const(a_s), const(o_s), const(x_s), piece_s(2), piece_s(3), piece_s(4), piece_s(5)],
        out_specs=[row(D_MODEL), pl.BlockSpec((n_s, D_MODEL), lambda i: (0, 0))],
        out_shape=[jax.ShapeDtypeStruct((n, D_MODEL), f32), jax.ShapeDtypeStruct((n_s, D_MODEL), f32)],
        compiler_params=_params(("arbitrary",)),
        name="out_mlp",
    )(a, o, x2d, mod, mod, mod, mod, lw["norm_mlp"], lw["w_out"], lw["w_up"], lw["w_down"],
      a_s, o_s, x_s, mod_s, mod_s, mod_s, mod_s)


def _stacked_weights(w_in, w_out, w_up, w_down):
    c_g = 3 * W_SB
    c_gate = c_g + 2 * WK_GLA + WV_GLA
    w_main = jnp.concatenate(
        [w_in[:, :, c_g:c_gate], w_in[:, :, c_gate + GATE_RANK:], w_in[:, :, c_gate:c_gate + GATE_RANK],
         jnp.zeros((DEPTH, D_MODEL, GATE_PAD - GATE_RANK), f32)], axis=2).astype(bf16)
    w_qkvt = w_in[:, :, :c_g].transpose(0, 2, 1).astype(bf16)
    return {"w_main": w_main, "w_qkvt": w_qkvt, "w_out": w_out.astype(bf16), "w_up": w_up.astype(bf16),
            "w_down": w_down.astype(bf16)}


def _layer_weights(l, stacked, norm_mix, norm_mlp, q_norm, k_norm, w_gate, b_gate, gla_norm):
    wg = jnp.concatenate([w_gate[l], jnp.zeros((GATE_PAD - GATE_RANK, WK_GLA), f32)], axis=0).astype(bf16)
    return dict(
        stacked, layer=l, w_gate=wg,
        b_gate=b_gate[l].reshape(1, WK_GLA),
        q_norm=jnp.tile(q_norm[l], H_SB),
        k_norm=jnp.tile(k_norm[l], H_SB),
        norm_mix=norm_mix[l].reshape(1, D_MODEL),
        norm_mlp=norm_mlp[l].reshape(1, D_MODEL),
        gla_norm=gla_norm[l].reshape(1, DV_GLA),
    )


def _group_layer(x2d, mod, lw, cw, n_batch, t, past_kt, past_vt, layer, s0, tm, blocks_per_batch, chunk,
                 gla_bb, prev_kv=()):
    q, kt, vt, ktb, vtb, qg, kg, vg, g, og = _in_call(x2d, mod, lw, tm, blocks_per_batch, prev_kv)
    if past_kt is None:
        a = _sb_call(q, ktb, vtb, ktb, vtb, cw, n_batch=n_batch, t_q=t, prompt=True, layer=layer)
    else:
        a = _sb_call(q, ktb, vtb, past_kt, past_vt, cw, n_batch=n_batch, t_q=t, prompt=False, layer=layer)
    o, s_new = _gla_call(qg, kg, vg, g, og, s0, lw["gla_norm"], n_batch, t, chunk, gla_bb)
    return (a, o, x2d, mod), kt, vt, s_new


def _heads_last(kt, n_batch, t):
    layers, nb = kt.shape[:2]
    return kt.reshape(layers, nb, H_SB, D_SB, -1).transpose(0, 1, 4, 2, 3).reshape(layers, n_batch, t, H_SB, D_SB)


def kernel(x_prompt, x_sample, c_prompt, c_sample, cache_k, cache_v, state_gla, w_ada, b_ada, norm_mix,
           norm_mlp, w_in, q_norm, k_norm, w_gate, b_gate, gla_norm, w_out, w_up, w_down):
    bp, tp, _ = x_prompt.shape
    bs, ts, _ = x_sample.shape
    past = cache_k.shape[2]
    n_c = bp + bs
    c_rows = -(-n_c // 16) * 16
    c_all = jnp.concatenate([c_prompt, c_sample, jnp.zeros((c_rows - n_c, D_MODEL), f32)], axis=0)
    mods = _ada_call(c_all, w_ada, b_ada)

    kk = np.arange(KB)
    half = kk[:, None] >= kk[None, :]
    cw = jnp.asarray(np.concatenate([half, half], axis=0).astype(np.float32), dtype=bf16)

    cache_kt = cache_k.transpose(0, 1, 3, 4, 2).reshape(DEPTH, bs, W_SB, past)
    cache_vt = cache_v.transpose(0, 1, 3, 4, 2).reshape(DEPTH, bs, W_SB, past)

    tm_p, tm_s = 512, bs * ts
    xp = x_prompt.reshape(bp * tp, D_MODEL)
    xs = x_sample.reshape(bs * ts, D_MODEL)
    sp_l, ks_l, vs_l, ss_l = [], [], [], []
    stacked = _stacked_weights(w_in, w_out, w_up, w_down)
    prompt_kv = ()
    for l in range(DEPTH):
        lw = _layer_weights(l, stacked, norm_mix, norm_mlp, q_norm, k_norm, w_gate, b_gate, gla_norm)
        mod_p = mods[l, :bp].reshape(bp, 1, N_MOD * D_MODEL)
        mod_s = jnp.repeat(mods[l, bp:n_c], ts, axis=0).reshape(1, bs * ts, N_MOD * D_MODEL)
        s0_p = jnp.zeros((bp, H_GLA, DK_GLA, DV_GLA), f32)
        mixed_p, kp, vp, sp = _group_layer(xp, mod_p, lw, cw, bp, tp, None, None, l, s0_p, tm_p, tp // tm_p, 256,
                                           bp, prompt_kv)
        prompt_kv = (kp, vp)
        mixed_s, kn, vn, sn = _group_layer(xs, mod_s, lw, cw, bs, ts, cache_kt, cache_vt, l,
                                           state_gla[l].astype(f32), tm_s, 1, ts, 8)
        xp, xs = _mlp_call(mixed_p, mixed_s, lw, tm_p, tp // tm_p)
        sp_l.append(sp)
        ks_l.append(_heads_last(kn, bs, ts)[0])
        vs_l.append(_heads_last(vn, bs, ts)[0])
        ss_l.append(sn)
    return (xp.reshape(bp, tp, D_MODEL), xs.reshape(bs, ts, D_MODEL),
            _heads_last(kp, bp, tp), _heads_last(vp, bp, tp), jnp.stack(sp_l),
            jnp.stack(ks_l), jnp.stack(vs_l), jnp.stack(ss_l))
```

```python
import functools

import numpy as np
import jax
import jax.numpy as jnp
from jax import lax
from jax.experimental import pallas as pl
from jax.experimental.pallas import tpu as pltpu

D_MODEL = 1024
DEPTH = 2
H_SB = 8
D_SB = 64
W_SB = H_SB * D_SB
H_GLA = 4
DK_GLA = 64
DV_GLA = 128
WK_GLA = H_GLA * DK_GLA
WV_GLA = H_GLA * DV_GLA
GATE_RANK = 16
GATE_TAU = 16.0
D_FF = 4 * D_MODEL
N_MOD = 6
EPS = 1e-6

LANES = 128
KB = 128
SB_FIRST_PASS = 3
SB_LOOP_GROUP = 1
SB_QUERY_BLOCKS = 4
SB_SAMPLE_SEQS = 4
ADA_PIECES = 2
N_PAIR = H_SB // 2
GATE_PAD = LANES
VMEM_LIMIT = 56 * 1024 * 1024
SURVIVAL_ZERO = 104.0
MASKED = -1e30

f32 = jnp.float32
bf16 = jnp.bfloat16
NT = (((1,), (1,)), ((), ()))
TN = (((0,), (0,)), ((), ()))


def _params(sem):
    return pltpu.CompilerParams(dimension_semantics=sem, vmem_limit_bytes=VMEM_LIMIT)


def _log_sigmoid(x):
    return jnp.minimum(x, 0.0) - jnp.log1p(jnp.exp(-jnp.abs(x)))


def _ada_kernel(c_ref, w_ref, b_ref, o_ref):
    c = c_ref[...]
    s = (c * jax.nn.sigmoid(c)).astype(bf16)
    o_ref[...] = jnp.dot(s, w_ref[...].astype(bf16), preferred_element_type=f32) + b_ref[...]


def _ada_call(c_all, w_ada, b_ada):
    rows = c_all.shape[0]
    return pl.pallas_call(
        _ada_kernel,
        grid=(DEPTH, N_MOD // ADA_PIECES),
        in_specs=[
            pl.BlockSpec((rows, D_MODEL), lambda l, p: (0, 0)),
            pl.BlockSpec((None, D_MODEL, ADA_PIECES * D_MODEL), lambda l, p: (l, 0, p)),
            pl.BlockSpec((None, 1, ADA_PIECES * D_MODEL), lambda l, p: (l, 0, p)),
        ],
        out_specs=pl.BlockSpec((None, rows, ADA_PIECES * D_MODEL), lambda l, p: (l, 0, p)),
        out_shape=jax.ShapeDtypeStruct((DEPTH, rows, N_MOD * D_MODEL), f32),
        compiler_params=_params(("parallel", "parallel")),
        name="ada_mod",
    )(c_all, w_ada, b_ada.reshape(DEPTH, 1, N_MOD * D_MODEL))


def _mod_spec(rows_per_block, blocks_per_batch, piece):
    return pl.BlockSpec((None, rows_per_block, D_MODEL),
                        lambda i: (i // blocks_per_batch, 0, piece))


def _head_rms(t):
    t3 = t.reshape(H_SB, D_SB, t.shape[1])
    ms = jnp.mean(t3 * t3, axis=1, keepdims=True)
    return (t3 * lax.rsqrt(ms + EPS)).reshape(t.shape)


def _in_kernel(*refs, n_prev):
    (x_ref, sh_ref, sc_ref, nw_ref, w_ref, wqkv_ref, qn_ref, kn_ref, wg_ref, bg_ref) = refs[:10]
    prev_refs = refs[10:10 + 2 * min(n_prev, 1)]
    (q_ref, kt_ref, vt_ref, ktb_ref, vtb_ref, qg_ref, kg_ref, vg_ref, g_ref, og_ref) = refs[len(refs) - 10:]
    for prev_ref, out_ref in zip(prev_refs, (kt_ref, vt_ref)):
        out_ref[:n_prev] = prev_ref[...]
    x = x_ref[...]
    ms = jnp.mean(x * x, axis=-1, keepdims=True)
    y = x * lax.rsqrt(ms + EPS) * nw_ref[...]
    h = (y * (1.0 + sc_ref[...]) + sh_ref[...]).astype(bf16)

    def proj(lo, hi):
        return jnp.dot(h, w_ref[:, lo:hi], preferred_element_type=f32)

    qkvt = lax.dot_general(wqkv_ref[...], h, NT, preferred_element_type=f32)
    qt = _head_rms(qkvt[:W_SB]) * qn_ref[...] * (D_SB ** -0.5)
    q_ref[...] = qt.T.astype(bf16)
    kt = _head_rms(qkvt[W_SB:2 * W_SB]) * kn_ref[...]
    kt_ref[n_prev] = kt
    ktb_ref[...] = kt.astype(bf16)
    vt = qkvt[2 * W_SB:]
    vt_ref[n_prev] = vt
    vtb_ref[...] = vt.astype(bf16)

    o = 0
    qg_ref[...] = proj(o, o + WK_GLA) * (DK_GLA ** -0.5)
    o += WK_GLA
    kg_ref[...] = proj(o, o + WK_GLA)
    o += WK_GLA
    vg_ref[...] = proj(o, o + WV_GLA).astype(bf16)
    o += WV_GLA
    og_ref[...] = proj(o, o + WV_GLA)
    o += WV_GLA
    gr = proj(o, o + GATE_PAD).astype(bf16)
    xg = jnp.dot(gr, wg_ref[...], preferred_element_type=f32) + bg_ref[...]
    g_ref[...] = _log_sigmoid(xg) * (1.0 / GATE_TAU)


def _in_call(x2d, mod, lw, tm, blocks_per_batch, prev_kv):
    n = x2d.shape[0]
    r = mod.shape[1]
    n_batch = n // (tm * blocks_per_batch)
    t = tm * blocks_per_batch
    n_prev = prev_kv[0].shape[0] if prev_kv else 0
    row = lambda w: pl.BlockSpec((tm, w), lambda i: (i, 0))
    col_map = lambda i: (i // blocks_per_batch, 0, i % blocks_per_batch)
    col = pl.BlockSpec((None, W_SB, tm), col_map)
    stack = lambda depth: pl.BlockSpec((depth, None, W_SB, tm), lambda i: (0,) + col_map(i))
    const = lambda a: pl.BlockSpec(a.shape, lambda i: (0,) * a.ndim)
    qn_t = lw["q_norm"].reshape(W_SB, 1)
    kn_t = lw["k_norm"].reshape(W_SB, 1)
    rows_out = [(WK_GLA, f32), (WK_GLA, f32), (WV_GLA, bf16), (WK_GLA, f32), (WV_GLA, f32)]
    kv_stack = jax.ShapeDtypeStruct((n_prev + 1, n_batch, W_SB, t), f32)
    kv_b = jax.ShapeDtypeStruct((n_batch, W_SB, t), bf16)
    return pl.pallas_call(
        functools.partial(_in_kernel, n_prev=n_prev),
        grid=(n // tm,),
        in_specs=[row(D_MODEL), _mod_spec(r, blocks_per_batch, 0), _mod_spec(r, blocks_per_batch, 1),
                  const(lw["norm_mix"]), const(lw["w_main"]), const(lw["w_qkvt"]),
                  const(qn_t), const(kn_t), const(lw["w_gate"]), const(lw["b_gate"])]
        + [stack(n_prev)] * len(prev_kv),
        out_specs=[row(W_SB), stack(n_prev + 1), stack(n_prev + 1), col, col] + [row(w) for w, _ in rows_out],
        out_shape=[jax.ShapeDtypeStruct((n, W_SB), bf16), kv_stack, kv_stack, kv_b, kv_b]
        + [jax.ShapeDtypeStruct((n, w), dt) for w, dt in rows_out],
        compiler_params=_params(("parallel",)),
        name="in_proj",
    )(x2d, mod, mod, lw["norm_mix"], lw["w_main"], lw["w_qkvt"], qn_t, kn_t,
      lw["w_gate"], lw["b_gate"], *prev_kv)


def _pair_diag(a):
    z = jnp.zeros((D_SB, a.shape[1]), a.dtype)
    top = jnp.concatenate([a[:D_SB], z], axis=0)
    bot = jnp.concatenate([z, a[D_SB:]], axis=0)
    return jnp.concatenate([top, bot], axis=1)


def _sb_scores(q_pair, k_tiles, biases):
    tiles = []
    for kt, bias in zip(k_tiles, biases):
        z2 = jnp.dot(q_pair, _pair_diag(kt), preferred_element_type=f32)
        for hh in range(2):
            z = z2[:, hh * KB:(hh + 1) * KB] + bias
            sp = jnp.maximum(z, 0.0) + jnp.log(1.0 + jnp.exp(-jnp.abs(z)))
            hi = sp.astype(bf16)
            lo = (sp - hi.astype(f32)).astype(bf16)
            tiles.append((z, jnp.concatenate([hi, lo], axis=1)))
    return tiles


def _sb_apply(tiles, cs, v_tiles, carries, acc, tq):
    new_carries = []
    ws = [[None, None] for _ in v_tiles]
    for hh in range(2):
        c = carries[hh]
        for bi in range(len(v_tiles)):
            ti = bi * 2 + hh
            part = cs[ti * tq:(ti + 1) * tq]
            total = jnp.broadcast_to(part[:, :1], (tq, KB))
            ws[bi][hh] = jnp.exp(tiles[ti][0] - (part if c is None else c + part)).astype(bf16)
            c = total if c is None else c + total
        new_carries.append(c)
    for bi, vt in enumerate(v_tiles):
        w_pair = jnp.concatenate(ws[bi], axis=1)
        pv = lax.dot_general(w_pair, _pair_diag(vt), NT, preferred_element_type=f32)
        acc = pv if acc is None else acc + pv
    return acc, new_carries


def _sb_kernel(q_ref, kd_ref, vd_ref, kp_ref, vp_ref, cw_ref, *rest, tq, n_sub, seg, n_past, layer):
    if n_past is None:
        o_ref, acc_ref, carry_ref = rest
    else:
        kfar_ref, vfar_ref, o_ref, acc_ref, carry_ref, kbuf_ref, vbuf_ref, sem = rest
    b = pl.program_id(0)
    step = pl.program_id(1)
    row = lax.broadcasted_iota(jnp.int32, (tq, KB), 0)
    col = lax.broadcasted_iota(jnp.int32, (tq, KB), 1)
    cw = cw_ref[...]
    pair_rows = [slice(p * KB, (p + 1) * KB) for p in range(N_PAIR)]
    seq = (lambda s: b) if n_past is None else (lambda s: b * n_sub + s)
    seg_shift = seg.bit_length() - 1

    def first_block(s):
        own = jnp.logical_and(lax.shift_right_logical(col, seg_shift) == seq(s) % (KB // seg),
                              jnp.bitwise_and(col, seg - 1) < row)
        lanes = slice(s * KB, (s + 1) * KB) if n_past is None else slice(None)
        return kd_ref[:, lanes], vd_ref[:, lanes], jnp.where(own, 0.0, MASKED)

    def past_block(j, first_pass, s):
        if n_past is not None and first_pass:
            c0 = (j - (n_past - (SB_FIRST_PASS - 1))) * KB
            return kp_ref[s, :, c0:c0 + KB].astype(bf16), vp_ref[s, :, c0:c0 + KB].astype(bf16), 0.0
        jc = pl.multiple_of(jnp.maximum(j, 0) * KB, KB)
        if n_past is None:
            kt, vt = kp_ref[b, :, pl.ds(jc, KB)], vp_ref[b, :, pl.ds(jc, KB)]
        else:
            copies = [pltpu.make_async_copy(src.at[layer, seq(s), :, pl.ds(jc, KB)], dst, sem.at[i])
                      for i, (src, dst) in enumerate(((kfar_ref, kbuf_ref), (vfar_ref, vbuf_ref)))]
            for cp in copies:
                cp.start()
            for cp in copies:
                cp.wait()
            kt, vt = kbuf_ref[...], vbuf_ref[...]
        return kt.astype(bf16), vt.astype(bf16), jnp.where(j < 0, MASKED, 0.0)

    def run_group(subs, first):
        order = [(s, p) for s, _ in subs for p in range(N_PAIR)]
        blocks_of = dict(subs)
        tiles = {(s, p): _sb_scores(q_ref[s * tq:(s + 1) * tq, pair_rows[p]],
                                    [kt[pair_rows[p]] for kt, _, _ in blocks_of[s]],
                                    [bias for _, _, bias in blocks_of[s]]) for s, p in order}
        cs = jnp.dot(jnp.concatenate([t[1] for key in order for t in tiles[key]], axis=0), cw,
                     preferred_element_type=f32)
        start, floor = 0, {}
        for s, p in order:
            n_rows = 2 * len(blocks_of[s]) * tq
            carries = (None, None) if first else (carry_ref[s * H_SB + 2 * p], carry_ref[s * H_SB + 2 * p + 1])
            acc, cr = _sb_apply(tiles[s, p], cs[start:start + n_rows], [vt[pair_rows[p]] for _, vt, _ in blocks_of[s]],
                                carries, None if first else acc_ref[s * N_PAIR + p], tq)
            start += n_rows
            acc_ref[s * N_PAIR + p] = acc
            carry_ref[s * H_SB + 2 * p] = cr[0]
            carry_ref[s * H_SB + 2 * p + 1] = cr[1]
            m = jnp.minimum(cr[0], cr[1])
            floor[s] = m if s not in floor else jnp.minimum(floor[s], m)
        return {s: (jnp.min(m) < SURVIVAL_ZERO).astype(jnp.int32) for s, m in floor.items()}

    n_prev = [step * n_sub + s if n_past is None else n_past for s in range(n_sub)]
    first = [(s, [first_block(s)] + [past_block(n_prev[s] - 1 - d, True, s) for d in range(SB_FIRST_PASS - 1)])
             for s in range(n_sub)]
    alive0 = run_group(first, True)

    def cond(st):
        j, alive = st
        return jnp.logical_and(j >= 0, alive > 0)

    for s in range(n_sub):
        def body(st, s=s):
            j, _ = st
            blocks = [past_block(j - d, False, s) for d in range(SB_LOOP_GROUP)]
            return j - SB_LOOP_GROUP, run_group([(s, blocks)], False)[s]

        lax.while_loop(cond, body, (n_prev[s] - SB_FIRST_PASS, alive0[s]))
        o_ref[s * tq:(s + 1) * tq, :] = jnp.concatenate(
            [acc_ref[s * N_PAIR + p] for p in range(N_PAIR)], axis=1).astype(bf16)


def _sb_call(q, kd, vd, kp, vp, cw, *, n_batch, t_q, prompt, layer):
    tq = min(KB, t_q)
    n_sub = SB_QUERY_BLOCKS if prompt else SB_SAMPLE_SEQS
    nq = t_q // (tq * n_sub) if prompt else 1
    q_spec = pl.BlockSpec((n_sub * tq, W_SB), lambda b, i: (b * nq + i, 0))
    if prompt:
        d_spec = pl.BlockSpec((None, W_SB, n_sub * KB), lambda b, i: (b, 0, i))
        p_spec = pl.BlockSpec(kp.shape, lambda b, i: (0, 0, 0))
        seg, n_past = KB, None
        far_specs, far_args, far_scratch = [], (), []
    else:
        past = kp.shape[-1]
        tail = (SB_FIRST_PASS - 1) * KB
        assert past % tail == 0 and SB_LOOP_GROUP == 1
        assert t_q == tq and n_batch % n_sub == 0 and KB % (n_sub * t_q) == 0
        d_spec = pl.BlockSpec((None, W_SB, KB), lambda b, i: (0, 0, (b * n_sub * t_q) // KB))
        p_spec = pl.BlockSpec((None, n_sub, W_SB, tail), lambda b, i: (layer, b, 0, past // tail - 1))
        seg, n_past = t_q, past // KB
        far_specs, far_args = [pl.BlockSpec(memory_space=pl.ANY)] * 2, (kp, vp)
        far_scratch = [pltpu.VMEM((W_SB, KB), kp.dtype), pltpu.VMEM((W_SB, KB), vp.dtype),
                       pltpu.SemaphoreType.DMA((2,))]
    return pl.pallas_call(
        functools.partial(_sb_kernel, tq=tq, n_sub=n_sub, seg=seg, n_past=n_past, layer=layer),
        grid=(n_batch, nq) if prompt else (n_batch // n_sub, 1),
        in_specs=[q_spec, d_spec, d_spec, p_spec, p_spec, pl.BlockSpec(cw.shape, lambda b, i: (0, 0))] + far_specs,
        out_specs=q_spec,
        out_shape=jax.ShapeDtypeStruct((n_batch * t_q, W_SB), bf16),
        scratch_shapes=[pltpu.VMEM((n_sub * N_PAIR, tq, LANES), f32),
                        pltpu.VMEM((n_sub * H_SB, tq, LANES), f32)] + far_scratch,
        compiler_params=_params(("parallel", "arbitrary")),
        name="stick_breaking",
    )(q, kd, vd, kp, vp, cw, *far_args)


GLA_PAIRS = H_GLA // 2
GLA_SAFE_DECAY = 60.0


def _gla_tables(chunk):
    idx = np.arange(chunk)
    t, s = idx[:, None], idx[None, :]
    tri = jnp.asarray((s <= t).astype(np.float32), dtype=bf16)
    level = np.where(t > s, np.floor(np.log2(np.maximum(t ^ s, 1))).astype(np.int32), -1)
    level = np.where(t < s, -2, level).astype(np.int32)
    return tri, jnp.asarray(np.concatenate([level, level], axis=1))


def _level_reference(b, h):
    L = b.shape[0]
    if h >= 4:
        return jnp.concatenate(
            [jnp.broadcast_to(b[c0 + h - 1:c0 + h, :], (2 * h, LANES)) for c0 in range(0, L, 2 * h)], axis=0)
    b3 = b.reshape(L // 8, 8, LANES)
    pos = jnp.bitwise_and(lax.broadcasted_iota(jnp.int32, b3.shape, 1), 2 * h - 1)
    up = lambda n: pltpu.roll(b3, n, 1)
    if h == 1:
        ref = jnp.where(pos == 0, b3, up(1))
    else:
        ref = jnp.where(pos == 0, up(7), jnp.where(pos == 1, b3, jnp.where(pos == 2, up(1), up(2))))
    return ref.reshape(L, LANES)


def _two_heads(a, first_head):
    return jnp.concatenate([jnp.where(first_head, a, 0.0).astype(bf16),
                            jnp.where(first_head, 0.0, a).astype(bf16)], axis=0)


def _gla_row_split(L):
    return L // 2 if L >= 2 * LANES else L


def _gla_att_single(att_ref, q_in, k, b, lev2, first_head):
    L = q_in.shape[0]
    H = _gla_row_split(L)
    k2 = _two_heads(k * jnp.exp(-b), first_head)
    causal = lev2 >= -1
    if H == L:
        a = lax.dot_general(q_in, k2, NT, preferred_element_type=f32)
        att_ref[...] = jnp.where(causal, a, 0.0)
        return
    k_top = jnp.concatenate([k2[:H], k2[L:L + H]], axis=0)
    a = lax.dot_general(q_in[:H], k_top, NT, preferred_element_type=f32)
    tri = causal[:H, :H]
    att_ref[:H, :H] = jnp.where(tri, a[:, :H], 0.0)
    att_ref[:H, L:L + H] = jnp.where(tri, a[:, H:], 0.0)
    a = lax.dot_general(q_in[H:], k2, NT, preferred_element_type=f32)
    att_ref[H:] = jnp.where(causal[H:], a, 0.0)


def _gla_att_levels(q, k, b, lev2, first_head, n_lev):
    a = lax.dot_general(q.astype(bf16), _two_heads(k, first_head), NT, preferred_element_type=f32)
    att = jnp.where(lev2 == -1, a, 0.0)
    for lev in range(n_lev):
        d = b - _level_reference(b, 1 << lev)
        e = jnp.exp(-jnp.abs(d))
        is_query = d < 0.0
        q_l = (q * jnp.where(is_query, e, 1.0)).astype(bf16)
        k_l = k * jnp.where(is_query, 1.0, e)
        a = lax.dot_general(q_l, _two_heads(k_l, first_head), NT, preferred_element_type=f32)
        att = jnp.where(lev2 == lev, a, att)
    return att


def _gla_kernel(q_ref, k_ref, v_ref, g_ref, og_ref, s0_ref, tri_ref, lev_ref, gn_ref,
                o_ref, sout_ref, s_ref, att_ref, *, chunk, bb):
    c = pl.program_id(1)
    L = chunk
    n_lev = L.bit_length() - 1
    chains = [(bi, p) for bi in range(bb) for p in range(GLA_PAIRS)]
    zeros_s = jnp.zeros((DK_GLA, DV_GLA), f32)

    @pl.when(c == 0)
    def _():
        for bi, p in chains:
            s0 = s0_ref[bi, p]
            s_ref[bi, p] = jnp.concatenate(
                [jnp.concatenate([s0[:DK_GLA], zeros_s], axis=1),
                 jnp.concatenate([zeros_s, s0[DK_GLA:]], axis=1)], axis=0)

    lane = lax.broadcasted_iota(jnp.int32, (L, LANES), 1)
    first_head = lane < DK_GLA
    lev2 = lev_ref[...]
    tri = tri_ref[...]

    pre = []
    floor = None
    for bi, p in chains:
        lanes = slice(p * LANES, (p + 1) * LANES)
        q = q_ref[bi, :, lanes]
        k = k_ref[bi, :, lanes]
        g = g_ref[bi, :, lanes]
        g_hi = g.astype(bf16)
        g_lo = (g - g_hi.astype(f32)).astype(bf16)
        g2 = jnp.concatenate([g_hi, g_lo], axis=1)
        bs = jnp.dot(tri, g2, preferred_element_type=f32)
        b = bs[:, :LANES] + bs[:, LANES:]
        b_last = b[L - 1:L, :]
        q_in = (q * jnp.exp(b)).astype(bf16)
        k_end = (k * jnp.exp(b_last - b)).astype(bf16)
        pre.append((q, k, g2, b, q_in, k_end))
        floor = b_last if floor is None else jnp.minimum(floor, b_last)
    single = jnp.min(floor) >= -GLA_SAFE_DECAY

    @pl.when(single)
    def _():
        for ci, (q, k, g2, b, q_in, k_end) in enumerate(pre):
            _gla_att_single(att_ref.at[ci], q_in, k, b, lev2, first_head)

    @pl.when(jnp.logical_not(single))
    def _():
        for ci, (q, k, g2, b, q_in, k_end) in enumerate(pre):
            att_ref[ci] = _gla_att_levels(q, k, b, lev2, first_head, n_lev)

    rowi = lax.broadcasted_iota(jnp.int32, (2 * DK_GLA, 2 * DV_GLA), 0)
    coli = lax.broadcasted_iota(jnp.int32, (2 * DK_GLA, 2 * DV_GLA), 1)
    own_block = (rowi < DK_GLA) == (coli < DV_GLA)
    ones = jnp.ones((L, LANES), bf16)
    zeros_v = jnp.zeros((L, DV_GLA), bf16)
    gn = gn_ref[...]
    for ci, (bi, p) in enumerate(chains):
        q, k, g2, b, q_in, k_end = pre[ci]
        cols = slice(p * 2 * DV_GLA, (p + 1) * 2 * DV_GLA)
        v = v_ref[bi, :, cols].astype(bf16)
        v_diag = jnp.concatenate([jnp.concatenate([v[:, :DV_GLA], zeros_v], axis=1),
                                  jnp.concatenate([zeros_v, v[:, DV_GLA:]], axis=1)], axis=0)
        s_old = s_ref[bi, p]
        H = _gla_row_split(L)
        if H == L:
            o2 = jnp.dot(att_ref[ci].astype(bf16), v_diag, preferred_element_type=f32)
        else:
            att_top = jnp.concatenate([att_ref[ci, :H, :H], att_ref[ci, :H, L:L + H]], axis=1).astype(bf16)
            v_top = jnp.concatenate([v_diag[:H], v_diag[L:L + H]], axis=0)
            o2 = jnp.concatenate([jnp.dot(att_top, v_top, preferred_element_type=f32),
                                  jnp.dot(att_ref[ci, H:].astype(bf16), v_diag, preferred_element_type=f32)], axis=0)
        o2 = o2 + jnp.dot(q_in, s_old.astype(bf16), preferred_element_type=f32)
        tot = lax.dot_general(g2, ones, TN, preferred_element_type=f32)
        decay = jnp.exp(tot[:LANES] + tot[LANES:])
        upd = lax.dot_general(k_end, v, TN, preferred_element_type=f32)
        s_ref[bi, p] = jnp.concatenate([decay, decay], axis=1) * s_old + jnp.where(own_block, upd, 0.0)
        og = og_ref[bi, :, cols]
        outs = []
        for hh in range(2):
            o = o2[:, hh * DV_GLA:(hh + 1) * DV_GLA]
            ms = jnp.mean(o * o, axis=-1, keepdims=True)
            gate = og[:, hh * DV_GLA:(hh + 1) * DV_GLA]
            outs.append(o * lax.rsqrt(ms + EPS) * gn * (gate * jax.nn.sigmoid(gate)))
        o_ref[bi, :, cols] = jnp.concatenate(outs, axis=1).astype(bf16)

    @pl.when(c == pl.num_programs(1) - 1)
    def _():
        for bi, p in chains:
            s = s_ref[bi, p]
            sout_ref[bi, p] = jnp.concatenate([s[:DK_GLA, :DV_GLA], s[DK_GLA:, DV_GLA:]], axis=0)


def _gla_call(qg, kg, vg, g, og, s0, gla_norm, n_batch, t, chunk, bb):
    nc = t // chunk
    tri, lev2 = _gla_tables(chunk)
    kspec = pl.BlockSpec((bb, chunk, WK_GLA), lambda i, c: (i, c, 0))
    vspec = pl.BlockSpec((bb, chunk, WV_GLA), lambda i, c: (i, c, 0))
    sspec = pl.BlockSpec((bb, GLA_PAIRS, 2 * DK_GLA, DV_GLA), lambda i, c: (i, 0, 0, 0))
    const = lambda a: pl.BlockSpec(a.shape, lambda i, c: (0,) * a.ndim)
    s0 = s0.reshape(n_batch, GLA_PAIRS, 2 * DK_GLA, DV_GLA)
    k3 = lambda a: a.reshape(n_batch, t, WK_GLA)
    v3 = lambda a: a.reshape(n_batch, t, WV_GLA)
    o, s_out = pl.pallas_call(
        functools.partial(_gla_kernel, chunk=chunk, bb=bb),
        grid=(n_batch // bb, nc),
        in_specs=[kspec, kspec, vspec, kspec, vspec, sspec, const(tri), const(lev2), const(gla_norm)],
        out_specs=[vspec, sspec],
        out_shape=[jax.ShapeDtypeStruct((n_batch, t, WV_GLA), bf16),
                   jax.ShapeDtypeStruct(s0.shape, f32)],
        scratch_shapes=[pltpu.VMEM((bb, GLA_PAIRS, 2 * DK_GLA, 2 * DV_GLA), f32),
                        pltpu.VMEM((bb * GLA_PAIRS, chunk, 2 * chunk), f32)],
        compiler_params=_params(("parallel", "arbitrary")),
        name="gla",
    )(k3(qg), k3(kg), v3(vg), k3(g), v3(og), s0, tri, lev2, gla_norm)
    return o.reshape(n_batch * t, WV_GLA), s_out.reshape(n_batch, H_GLA, DK_GLA, DV_GLA)


def _mlp_rows(rows, weights, y_ref, ff_chunk):
    a_ref, o_ref, x_ref, mix_gate_ref, sh_ref, sc_ref, gate_ref = rows
    nw_ref, wo_ref, wu_ref, wd_ref = weights
    m = (jnp.dot(a_ref[...], wo_ref[:W_SB, :], preferred_element_type=f32)
         + jnp.dot(o_ref[...], wo_ref[W_SB:, :], preferred_element_type=f32))
    x = x_ref[...] + mix_gate_ref[...] * m
    ms = jnp.mean(x * x, axis=-1, keepdims=True)
    y = x * lax.rsqrt(ms + EPS) * nw_ref[...]
    h = (y * (1.0 + sc_ref[...]) + sh_ref[...]).astype(bf16)
    acc = jnp.zeros(x.shape, f32)
    for c0 in range(0, D_FF, ff_chunk):
        u = jnp.dot(h, wu_ref[:, c0:c0 + ff_chunk], preferred_element_type=f32)
        u = jnp.square(jnp.maximum(u, 0.0)).astype(bf16)
        acc = acc + jnp.dot(u, wd_ref[c0:c0 + ff_chunk, :], preferred_element_type=f32)
    y_ref[...] = x + gate_ref[...] * acc


def _mlp_kernel(*refs, ff_chunk):
    big, weights, small = refs[:7], refs[7:11], refs[11:18]
    y_ref, y_small_ref = refs[18:]
    _mlp_rows(big, weights, y_ref, ff_chunk)

    @pl.when(pl.program_id(0) == pl.num_programs(0) - 1)
    def _():
        _mlp_rows(small, weights, y_small_ref, ff_chunk)


def _mlp_call(big, small, lw, tm, blocks_per_batch):
    a, o, x2d, mod = big
    a_s, o_s, x_s, mod_s = small
    n, n_s = x2d.shape[0], x_s.shape[0]
    row = lambda w: pl.BlockSpec((tm, w), lambda i: (i, 0))
    const = lambda arr: pl.BlockSpec(arr.shape, lambda i: (0,) * arr.ndim)
    piece = lambda k: _mod_spec(mod.shape[1], blocks_per_batch, k)
    piece_s = lambda k: pl.BlockSpec((None, n_s, D_MODEL), lambda i: (0, 0, k))
    return pl.pallas_call(
        functools.partial(_mlp_kernel, ff_chunk=1024),
        grid=(n // tm,),
        in_specs=[row(W_SB), row(WV_GLA), row(D_MODEL), piece(2), piece(3), piece(4), piece(5),
                  const(lw["norm_mlp"]), const(lw["w_out"]), const(lw["w_up"]), const(lw["w_down"]),
                  const(a_s), const(o_s), const(x_s), piece_s(2), piece_s(3), piece_s(4), piece_s(5)],
        out_specs=[row(D_MODEL), pl.BlockSpec((n_s, D_MODEL), lambda i: (0, 0))],
        out_shape=[jax.ShapeDtypeStruct((n, D_MODEL), f32), jax.ShapeDtypeStruct((n_s, D_MODEL), f32)],
        compiler_params=_params(("arbitrary",)),
        name="out_mlp",
    )(a, o, x2d, mod, mod, mod, mod, lw["norm_mlp"], lw["w_out"], lw["w_up"], lw["w_down"],
      a_s, o_s, x_s, mod_s, mod_s, mod_s, mod_s)


def _layer_weights(l, w_in, w_out, w_up, w_down, norm_mix, norm_mlp, q_norm, k_norm, w_gate, b_gate, gla_norm):
    wi = w_in[l]
    c_g = 3 * W_SB
    c_gate = c_g + 2 * WK_GLA + WV_GLA
    w_main = jnp.concatenate(
        [wi[:, c_g:c_gate], wi[:, c_gate + GATE_RANK:], wi[:, c_gate:c_gate + GATE_RANK],
         jnp.zeros((D_MODEL, GATE_PAD - GATE_RANK), f32)], axis=1).astype(bf16)
    wg = jnp.concatenate([w_gate[l], jnp.zeros((GATE_PAD - GATE_RANK, WK_GLA), f32)], axis=0).astype(bf16)
    return dict(
        w_main=w_main, w_qkvt=wi[:, :c_g].T.astype(bf16),
        w_out=w_out[l].astype(bf16), w_up=w_up[l].astype(bf16), w_down=w_down[l].astype(bf16), w_gate=wg,
        b_gate=b_gate[l].reshape(1, WK_GLA),
        q_norm=jnp.tile(q_norm[l], H_SB),
        k_norm=jnp.tile(k_norm[l], H_SB),
        norm_mix=norm_mix[l].reshape(1, D_MODEL),
        norm_mlp=norm_mlp[l].reshape(1, D_MODEL),
        gla_norm=gla_norm[l].reshape(1, DV_GLA),
    )


def _group_layer(x2d, mod, lw, cw, n_batch, t, past_kt, past_vt, layer, s0, tm, blocks_per_batch, chunk,
                 gla_bb, prev_kv=()):
    q, kt, vt, ktb, vtb, qg, kg, vg, g, og = _in_call(x2d, mod, lw, tm, blocks_per_batch, prev_kv)
    if past_kt is None:
        a = _sb_call(q, ktb, vtb, ktb, vtb, cw, n_batch=n_batch, t_q=t, prompt=True, layer=layer)
    else:
        a = _sb_call(q, ktb, vtb, past_kt, past_vt, cw, n_batch=n_batch, t_q=t, prompt=False, layer=layer)
    o, s_new = _gla_call(qg, kg, vg, g, og, s0, lw["gla_norm"], n_batch, t, chunk, gla_bb)
    return (a, o, x2d, mod), kt, vt, s_new


def _heads_last(kt, n_batch, t):
    layers, nb = kt.shape[:2]
    return kt.reshape(layers, nb, H_SB, D_SB, -1).transpose(0, 1, 4, 2, 3).reshape(layers, n_batch, t, H_SB, D_SB)


def kernel(x_prompt, x_sample, c_prompt, c_sample, cache_k, cache_v, state_gla, w_ada, b_ada, norm_mix,
           norm_mlp, w_in, q_norm, k_norm, w_gate, b_gate, gla_norm, w_out, w_up, w_down):
    bp, tp, _ = x_prompt.shape
    bs, ts, _ = x_sample.shape
    past = cache_k.shape[2]
    n_c = bp + bs
    c_rows = -(-n_c // 16) * 16
    c_all = jnp.concatenate([c_prompt, c_sample, jnp.zeros((c_rows - n_c, D_MODEL), f32)], axis=0)
    mods = _ada_call(c_all, w_ada, b_ada)

    kk = np.arange(KB)
    half = kk[:, None] >= kk[None, :]
    cw = jnp.asarray(np.concatenate([half, half], axis=0).astype(np.float32), dtype=bf16)

    cache_kt = cache_k.transpose(0, 1, 3, 4, 2).reshape(DEPTH, bs, W_SB, past)
    cache_vt = cache_v.transpose(0, 1, 3, 4, 2).reshape(DEPTH, bs, W_SB, past)

    tm_p, tm_s = 512, bs * ts
    xp = x_prompt.reshape(bp * tp, D_MODEL)
    xs = x_sample.reshape(bs * ts, D_MODEL)
    sp_l, ks_l, vs_l, ss_l = [], [], [], []
    prompt_kv = ()
    for l in range(DEPTH):
        lw = _layer_weights(l, w_in, w_out, w_up, w_down, norm_mix, norm_mlp, q_norm, k_norm, w_gate, b_gate,
                            gla_norm)
        mod_p = mods[l, :bp].reshape(bp, 1, N_MOD * D_MODEL)
        mod_s = jnp.repeat(mods[l, bp:n_c], ts, axis=0).reshape(1, bs * ts, N_MOD * D_MODEL)
        s0_p = jnp.zeros((bp, H_GLA, DK_GLA, DV_GLA), f32)
        mixed_p, kp, vp, sp = _group_layer(xp, mod_p, lw, cw, bp, tp, None, None, l, s0_p, tm_p, tp // tm_p, 256,
                                           bp, prompt_kv)
        prompt_kv = (kp, vp)
        mixed_s, kn, vn, sn = _group_layer(xs, mod_s, lw, cw, bs, ts, cache_kt, cache_vt, l,
                                           state_gla[l].astype(f32), tm_s, 1, ts, 8)
        xp, xs = _mlp_call(mixed_p, mixed_s, lw, tm_p, tp // tm_p)
        sp_l.append(sp)
        ks_l.append(_heads_last(kn, bs, ts)[0])
        vs_l.append(_heads_last(vn, bs, ts)[0])
        ss_l.append(sn)
    return (xp.reshape(bp, tp, D_MODEL), xs.reshape(bs, ts, D_MODEL),
            _heads_last(kp, bp, tp), _heads_last(vp, bp, tp), jnp.stack(sp_l),
            jnp.stack(ks_l), jnp.stack(vs_l), jnp.stack(ss_l))
```
